```python
import math
import jax, jax.numpy as jnp
from jax import lax
import numpy as np

D_MODEL = 1024
BATCH = 8
SEQ = 2048
DEPTH = 1

CHUNK = 64
Q_BLOCK = 128
MEM_LEN = 256
POOL_WIDTH = D_MODEL
POOL_WINDOWS = (2, 4, 8, 16)
POOL_GROUPS = len(POOL_WINDOWS)
POOL_GROUP_DIM = POOL_WIDTH // POOL_GROUPS
DIFF_HEADS = 8
DIFF_HEAD_DIM = 64
DIFF_WIDTH = DIFF_HEADS * 2 * DIFF_HEAD_DIM
LAMBDA_INIT_BASE = 0.8
LAMBDA_INIT_AMP = 0.6
LAMBDA_INIT_RATE = 0.3
REL_BUCKETS = 32
REL_MAX_DIST = 128
CROSS_HEADS = 4
CROSS_HEAD_DIM = D_MODEL // CROSS_HEADS
D_FF = 4 * D_MODEL
IN_WIDTH = POOL_WIDTH + 3 * DIFF_WIDTH
EPS = 1e-6
NEG_INF = -1e30

kernel_name = "hybrid_pool_diffattn_gated_encoder"


def rmsnorm(x, g):
    xf = x.astype(jnp.float32)
    y = xf * lax.rsqrt(jnp.mean(xf * xf, axis=-1, keepdims=True) + EPS)
    return (y * g.astype(jnp.float32)).astype(x.dtype)


def multiscale_pool(u, w_group, scale):
    B, S, _ = u.shape
    uf = u.reshape(B, S, POOL_GROUPS, POOL_GROUP_DIM).astype(jnp.float32)
    cs = jnp.concatenate([jnp.zeros_like(uf[:, :1]), jnp.cumsum(uf, axis=1)], axis=1)
    t = jnp.arange(S)
    outs = []
    for g, w in enumerate(POOL_WINDOWS):
        c = cs[:, :, g]
        hi = c[:, 1:]
        lo = jnp.pad(c, ((0, 0), (w - 1, 0), (0, 0)))[:, :S]
        count = jnp.minimum(t + 1, w).astype(jnp.float32)[None, :, None]
        outs.append((hi - lo) / count - uf[:, :, g])
    pooled = jnp.stack(outs, axis=2).astype(u.dtype)
    y = jnp.einsum('bsgc,gcd->bsgd', pooled, w_group).reshape(B, S, POOL_WIDTH)
    return y * scale


def t5_bucket(rel):
    nb = REL_BUCKETS // 2
    ret = (rel > 0).astype(jnp.int32) * nb
    n = jnp.abs(rel)
    max_exact = nb // 2
    nf = jnp.maximum(n, 1).astype(jnp.float32)
    large = max_exact + (jnp.log(nf / max_exact) / math.log(REL_MAX_DIST / max_exact)
                         * (nb - max_exact)).astype(jnp.int32)
    large = jnp.minimum(large, nb - 1)
    return ret + jnp.where(n < max_exact, n, large)


def diff_attention(q, k, v, rel_bias, lam, lambda_init, subln_g):
    B, S = q.shape[:2]
    scale = DIFF_HEAD_DIM ** -0.5
    pos = jnp.arange(S, dtype=jnp.int32)
    outs = []
    for blk in range(S // Q_BLOCK):
        q0 = blk * Q_BLOCK
        kl = q0 + Q_BLOCK
        qp, kp = pos[q0:kl], pos[:kl]
        s = jnp.einsum('bqhmd,bkhmd->bhmqk', q[:, q0:kl], k[:, :kl]).astype(jnp.float32) * scale
        bias = jnp.transpose(rel_bias[t5_bucket(kp[None, :] - qp[:, None])], (2, 0, 1))
        s = s + bias[None, :, None].astype(jnp.float32)
        allowed = (kp[None, :] // CHUNK) <= (qp[:, None] // CHUNK)
        s = jnp.where(allowed, s, NEG_INF)
        p = jax.nn.softmax(s, axis=-1)
        a = p[:, :, 0] - lam * p[:, :, 1]
        outs.append(jnp.einsum('bhqk,bkhe->bqhe', a.astype(v.dtype), v[:, :kl]))
    o = jnp.concatenate(outs, axis=1)
    o = rmsnorm(o, subln_g) * (1.0 - lambda_init)
    return o.reshape(B, S, DIFF_WIDTH)


def cross_attention(hn, mn, w_cq, w_ckv, w_co):
    B, S, _ = hn.shape
    M = mn.shape[1]
    cq = (hn @ w_cq).reshape(B, S, CROSS_HEADS, CROSS_HEAD_DIM)
    ckv = (mn @ w_ckv).reshape(B, M, 2, CROSS_HEADS, CROSS_HEAD_DIM)
    ck, cv = ckv[:, :, 0], ckv[:, :, 1]
    s = jnp.einsum('bshd,bmhd->bhsm', cq, ck).astype(jnp.float32) * (CROSS_HEAD_DIM ** -0.5)
    p = jax.nn.softmax(s, axis=-1).astype(cv.dtype)
    o = jnp.einsum('bhsm,bmhd->bshd', p, cv).reshape(B, S, D_MODEL)
    return o @ w_co


def setup_inputs(seed: int = 0) -> dict:
    key = jax.random.key(seed)
    ks = jax.random.split(key, 32)
    f32 = jnp.float32

    def nrm(k, shape, fan_in):
        return jax.random.normal(k, shape, f32) * (fan_in ** -0.5)

    def gain(k, shape):
        return 1.0 + 0.05 * jax.random.normal(k, shape, f32)

    L = DEPTH
    return {
        "x": jax.random.normal(ks[0], (BATCH, SEQ, D_MODEL), f32),
        "mem": jax.random.normal(ks[1], (BATCH, MEM_LEN, D_MODEL), f32),
        "norm_mix_g": gain(ks[2], (L, D_MODEL)),
        "w_in": nrm(ks[3], (L, D_MODEL, IN_WIDTH), D_MODEL),
        "w_pool_group": nrm(ks[4], (L, POOL_GROUPS, POOL_GROUP_DIM, POOL_GROUP_DIM), POOL_GROUP_DIM),
        "pool_scale": gain(ks[5], (L, POOL_WIDTH)),
        "w_a_proj": nrm(ks[6], (L, POOL_WIDTH, D_MODEL), POOL_WIDTH),
        "lambda_q1": 0.1 * jax.random.normal(ks[7], (L, DIFF_HEAD_DIM), f32),
        "lambda_k1": 0.1 * jax.random.normal(ks[8], (L, DIFF_HEAD_DIM), f32),
        "lambda_q2": 0.1 * jax.random.normal(ks[9], (L, DIFF_HEAD_DIM), f32),
        "lambda_k2": 0.1 * jax.random.normal(ks[10], (L, DIFF_HEAD_DIM), f32),
        "subln_g": gain(ks[11], (L, 2 * DIFF_HEAD_DIM)),
        "rel_bias": 0.1 * jax.random.normal(ks[12], (REL_BUCKETS, DIFF_HEADS), f32),
        "w_b_proj": nrm(ks[13], (L, DIFF_WIDTH, D_MODEL), DIFF_WIDTH),
        "w_gate": nrm(ks[14], (L, D_MODEL, 2 * D_MODEL), D_MODEL),
        "b_gate": 0.01 * jax.random.normal(ks[15], (L, 2 * D_MODEL), f32),
        "w_out": nrm(ks[16], (L, D_MODEL, D_MODEL), D_MODEL),
        "norm_cross_g": gain(ks[17], (L, D_MODEL)),
        "norm_mem_g": gain(ks[18], (L, D_MODEL)),
        "w_cq": nrm(ks[19], (L, D_MODEL, D_MODEL), D_MODEL),
        "w_ckv": nrm(ks[20], (L, D_MODEL, 2 * D_MODEL), D_MODEL),
        "w_co": nrm(ks[21], (L, D_MODEL, D_MODEL), D_MODEL),
        "norm_mlp_g": gain(ks[22], (L, D_MODEL)),
        "w_ff1": nrm(ks[23], (L, D_MODEL, D_FF), D_MODEL),
        "w_ff2": nrm(ks[24], (L, D_FF, D_MODEL), D_FF),
        "final_norm_g": gain(ks[25], (D_MODEL,)),
    }


def reference(x, mem, norm_mix_g, w_in, w_pool_group, pool_scale, w_a_proj,
              lambda_q1, lambda_k1, lambda_q2, lambda_k2, subln_g, rel_bias, w_b_proj,
              w_gate, b_gate, w_out, norm_cross_g, norm_mem_g, w_cq, w_ckv, w_co,
              norm_mlp_g, w_ff1, w_ff2, final_norm_g):
    B, S, _ = x.shape
    h = x
    for l in range(DEPTH):
        xn = rmsnorm(h, norm_mix_g[l])
        proj = xn @ w_in[l]
        u = proj[..., :POOL_WIDTH]
        q = proj[..., POOL_WIDTH:POOL_WIDTH + DIFF_WIDTH].reshape(B, S, DIFF_HEADS, 2, DIFF_HEAD_DIM)
        k = proj[..., POOL_WIDTH + DIFF_WIDTH:POOL_WIDTH + 2 * DIFF_WIDTH].reshape(
            B, S, DIFF_HEADS, 2, DIFF_HEAD_DIM)
        v = proj[..., POOL_WIDTH + 2 * DIFF_WIDTH:].reshape(B, S, DIFF_HEADS, 2 * DIFF_HEAD_DIM)

        y_a = multiscale_pool(u, w_pool_group[l], pool_scale[l]) @ w_a_proj[l]

        lambda_init = LAMBDA_INIT_BASE - LAMBDA_INIT_AMP * math.exp(-LAMBDA_INIT_RATE * l)
        lam = (jnp.exp(jnp.sum(lambda_q1[l].astype(jnp.float32) * lambda_k1[l].astype(jnp.float32)))
               - jnp.exp(jnp.sum(lambda_q2[l].astype(jnp.float32) * lambda_k2[l].astype(jnp.float32)))
               + lambda_init)
        y_b = diff_attention(q, k, v, rel_bias, lam, lambda_init, subln_g[l]) @ w_b_proj[l]

        gates = jax.nn.sigmoid((xn @ w_gate[l] + b_gate[l]).astype(jnp.float32)).astype(h.dtype)
        g_a, g_b = gates[..., :D_MODEL], gates[..., D_MODEL:]
        h = h + (g_a * y_a + g_b * y_b) @ w_out[l]

        h = h + cross_attention(rmsnorm(h, norm_cross_g[l]), rmsnorm(mem, norm_mem_g[l]),
                                w_cq[l], w_ckv[l], w_co[l])

        hn = rmsnorm(h, norm_mlp_g[l])
        h = h + jnp.square(jax.nn.relu(hn @ w_ff1[l])) @ w_ff2[l]
    return rmsnorm(h, final_norm_g)
```

```python
import functools
import math

import jax
import jax.numpy as jnp
from jax import lax
from jax.experimental import pallas as pl
from jax.experimental.pallas import tpu as pltpu

CHUNK = 64
POOL_WINDOWS = (2, 4, 8, 16)
DIFF_HEADS = 8
DIFF_HEAD_DIM = 64
LAMBDA_INIT_BASE = 0.8
LAMBDA_INIT_AMP = 0.6
LAMBDA_INIT_RATE = 0.3
REL_BUCKETS = 32
REL_MAX_DIST = 128
CROSS_HEADS = 4
EPS = 1e-6
NEG_INF = -1e30
LOG2E = math.log2(math.e)

POOL_HALO = 16
TOKEN_TILE = 512
ATTN_TILE = 256
FF_CHUNK = 1024
VMEM_LIMIT = 56 * 1024 * 1024

BF16 = jnp.bfloat16
F32 = jnp.float32


def _resident(shape):
    n = len(shape)
    return pl.BlockSpec(shape, lambda *_: (0,) * n, pipeline_mode=pl.Buffered(1))


def _rms(x, g):
    return x * lax.rsqrt(jnp.mean(x * x, axis=-1, keepdims=True) + EPS) * g


def _dot(a, b):
    return jnp.dot(a, b, preferred_element_type=F32)


def _dot_nt(a, b):
    return lax.dot_general(a, b, (((1,), (1,)), ((), ())), preferred_element_type=F32)


def _mixer_front_kernel(x_ref, g_ref, wu_ref, wqkv_ref, wg_ref, bg_ref, wp_ref, ps_ref, wa_ref,
                        q_ref, k_ref, v_ref, gaya_ref, gb_ref, ubuf_ref, *, tiles_per_seq, q_scale):
    tm, d = x_ref.shape
    i = pl.program_id(0)
    tile_in_seq = i % tiles_per_seq
    xn = _rms(x_ref[...], g_ref[...]).astype(BF16)

    width = q_ref.shape[1]
    q_ref[...] = (_dot(xn, wqkv_ref[:, 0:width]) * q_scale).astype(BF16)
    k_ref[...] = _dot(xn, wqkv_ref[:, width:2 * width]).astype(BF16)
    v_ref[...] = _dot(xn, wqkv_ref[:, 2 * width:3 * width]).astype(BF16)

    @pl.when(tile_in_seq == 0)
    def _():
        ubuf_ref[0:POOL_HALO, :] = jnp.zeros((POOL_HALO, d), F32)

    ubuf_ref[POOL_HALO:POOL_HALO + tm, :] = _dot(xn, wu_ref[...])
    pos = tile_in_seq * tm + lax.broadcasted_iota(jnp.int32, (tm, 1), 0)
    gd = d // len(POOL_WINDOWS)
    mapped = []
    for g, w in enumerate(POOL_WINDOWS):
        cols = slice(g * gd, (g + 1) * gd)
        u_g = ubuf_ref[POOL_HALO:POOL_HALO + tm, cols]
        acc = u_g
        for j in range(1, w):
            acc = acc + ubuf_ref[POOL_HALO - j:POOL_HALO - j + tm, cols]
        inv_count = 1.0 / jnp.minimum(pos + 1, w).astype(F32)
        pooled = acc * inv_count - u_g
        mapped.append(_dot(pooled.astype(BF16), wp_ref[g]))
    ubuf_ref[0:POOL_HALO, :] = ubuf_ref[tm:tm + POOL_HALO, :]
    y = jnp.concatenate(mapped, axis=-1) * ps_ref[...]
    y_a = _dot(y.astype(BF16), wa_ref[...])

    z = _dot(xn, wg_ref[...]) + bg_ref[...]
    gates = 1.0 / (1.0 + jnp.exp(-z))
    gaya_ref[...] = gates[:, :d] * y_a
    gb_ref[...] = gates[:, d:]


def _mixer_front(x2, g, wu, wqkv, wg, bg, wp, ps, wa, *, seq, q_scale):
    n, d = x2.shape
    tm = TOKEN_TILE
    width = wqkv.shape[1] // 3
    row = lambda c: pl.BlockSpec((tm, c), lambda i: (i, 0))
    kern = functools.partial(_mixer_front_kernel, tiles_per_seq=seq // tm, q_scale=q_scale)
    return pl.pallas_call(
        kern,
        grid=(n // tm,),
        in_specs=[row(d), _resident(g.shape), _resident(wu.shape), _resident(wqkv.shape), _resident(wg.shape),
                  _resident(bg.shape), _resident(wp.shape), _resident(ps.shape), _resident(wa.shape)],
        out_specs=[row(width), row(width), row(width), row(d), row(d)],
        out_shape=[jax.ShapeDtypeStruct((n, width), BF16)] * 3 + [jax.ShapeDtypeStruct((n, d), F32)] * 2,
        scratch_shapes=[pltpu.VMEM((POOL_HALO + tm, d), F32)],
        compiler_params=pltpu.CompilerParams(dimension_semantics=("arbitrary",), vmem_limit_bytes=VMEM_LIMIT),
        name="mixer_front",
    )(x2, g, wu, wqkv, wg, bg, wp, ps, wa)


def _rel_bucket(rel):
    nb = REL_BUCKETS // 2
    ret = (rel > 0).astype(jnp.int32) * nb
    n = jnp.abs(rel)
    max_exact = nb // 2
    nf = jnp.maximum(n, 1).astype(F32)
    large = max_exact + (jnp.log(nf / max_exact) / math.log(REL_MAX_DIST / max_exact)
                         * (nb - max_exact)).astype(jnp.int32)
    large = jnp.minimum(large, nb - 1)
    return ret + jnp.where(n < max_exact, n, large)


def _rel_bias_kernel(table_ref, bucket_ref, out_ref, *, far_bucket):
    h = pl.program_id(0)
    bucket = bucket_ref[...]
    far = table_ref[far_bucket, h]
    acc = jnp.zeros(bucket.shape, F32)
    for b in range(REL_BUCKETS):
        acc = jnp.where(bucket == b, table_ref[b, h] - far, acc)
    out_ref[...] = acc * LOG2E


def _rel_bias_tiles(rel_bias):
    t = ATTN_TILE
    assert t > REL_MAX_DIST
    heads = rel_bias.shape[1]
    r = jnp.arange(t, dtype=jnp.int32)[:, None]
    c = jnp.arange(t, dtype=jnp.int32)[None, :]
    bucket = jnp.stack([_rel_bucket(c - t - r), _rel_bucket(c - r)])
    kern = functools.partial(_rel_bias_kernel, far_bucket=REL_BUCKETS // 2 - 1)
    return pl.pallas_call(
        kern,
        grid=(heads,),
        in_specs=[pl.BlockSpec(memory_space=pltpu.SMEM), pl.BlockSpec((2, t, t), lambda h: (0, 0, 0))],
        out_specs=pl.BlockSpec((None, 2, t, t), lambda h: (h, 0, 0, 0)),
        out_shape=jax.ShapeDtypeStruct((heads, 2, t, t), F32),
        name="rel_bias",
    )(rel_bias, bucket)


def _diff_attn_kernel(lam_ref, q_ref, k_ref, v_ref, nb_ref, sg_ref, o_ref,
                      s_ref, mx_ref, l_ref, acc_ref, *, lambda_init):
    t, hd = q_ref.shape
    i = pl.program_id(2)
    q = q_ref[...]
    lane = lax.broadcasted_iota(jnp.int32, (t, hd), 1)
    zero = jnp.zeros_like(q)
    qq = jnp.concatenate([jnp.where(lane < hd // 2, q, zero), jnp.where(lane >= hd // 2, q, zero)], axis=0)

    mx_ref[...] = jnp.full(mx_ref.shape, NEG_INF, F32)

    def scores(j, bias):
        kt = k_ref[pl.ds(pl.multiple_of(j * t, t), t), :]
        s = _dot_nt(qq, kt)
        if bias is not None:
            s = s + jnp.concatenate([bias, bias], axis=0)
        return s

    def record(j, s):
        s_ref[j] = s
        mx_ref[...] = jnp.maximum(mx_ref[...], jnp.maximum(s[:, :t // 2], s[:, t // 2:]))

    def far_body(j, carry):
        record(j, scores(j, None))
        return carry

    lax.fori_loop(0, jnp.maximum(i - 1, 0), far_body, 0)

    @pl.when(i > 0)
    def _():
        record(i - 1, scores(i - 1, nb_ref[0]))

    s = scores(i, nb_ref[1])
    r = lax.broadcasted_iota(jnp.int32, (2 * t, t), 0) % t
    c = lax.broadcasted_iota(jnp.int32, (2 * t, t), 1)
    record(i, jnp.where(c // CHUNK <= r // CHUNK, s, NEG_INF))

    m = jnp.max(mx_ref[...], axis=-1, keepdims=True)
    l_ref[...] = jnp.zeros(l_ref.shape, F32)
    acc_ref[...] = jnp.zeros(acc_ref.shape, F32)

    def pv_body(j, carry):
        p = jnp.exp2(s_ref[j] - m)
        l_ref[...] += p[:, :t // 2] + p[:, t // 2:]
        vt = v_ref[pl.ds(pl.multiple_of(j * t, t), t), :]
        acc_ref[...] += _dot(p.astype(BF16), vt)
        return carry

    lax.fori_loop(0, i + 1, pv_body, 0)

    l = jnp.sum(l_ref[...], axis=-1, keepdims=True)
    o_maps = acc_ref[...] / l
    lam = (jnp.exp(jnp.sum(lam_ref[0:1, :] * lam_ref[1:2, :], axis=-1, keepdims=True))
           - jnp.exp(jnp.sum(lam_ref[2:3, :] * lam_ref[3:4, :], axis=-1, keepdims=True)) + lambda_init)
    o = o_maps[:t] - lam * o_maps[t:]
    o_ref[...] = (_rms(o, sg_ref[...]) * (1.0 - lambda_init)).astype(o_ref.dtype)


def _diff_attn(lam_vecs, q, k, v, near_bias, subln_g, *, batch, seq, lambda_init):
    n, width = q.shape
    t = ATTN_TILE
    hd = width // DIFF_HEADS
    nq = seq // t
    q3, k3, v3 = (a.reshape(batch, seq, width) for a in (q, k, v))
    kern = functools.partial(_diff_attn_kernel, lambda_init=lambda_init)
    out = pl.pallas_call(
        kern,
        grid=(batch, DIFF_HEADS, nq),
        in_specs=[
            pl.BlockSpec(lam_vecs.shape, lambda b, h, i: (0, 0)),
            pl.BlockSpec((None, t, hd), lambda b, h, i: (b, i, h)),
            pl.BlockSpec((None, seq, hd), lambda b, h, i: (b, 0, h)),
            pl.BlockSpec((None, seq, hd), lambda b, h, i: (b, 0, h)),
            pl.BlockSpec((None, 2, t, t), lambda b, h, i: (h, 0, 0, 0)),
            pl.BlockSpec(subln_g.shape, lambda b, h, i: (0, 0)),
        ],
        out_specs=pl.BlockSpec((None, t, hd), lambda b, h, i: (b, i, h)),
        out_shape=jax.ShapeDtypeStruct((batch, seq, width), BF16),
        scratch_shapes=[pltpu.VMEM((nq, 2 * t, t), F32), pltpu.VMEM((2 * t, t // 2), F32),
                        pltpu.VMEM((2 * t, t // 2), F32), pltpu.VMEM((2 * t, hd), F32)],
        compiler_params=pltpu.CompilerParams(dimension_semantics=("arbitrary",) * 3, vmem_limit_bytes=VMEM_LIMIT),
        name="diff_attn",
    )(lam_vecs, q3, k3, v3, near_bias, subln_g)
    return out.reshape(n, width)


def _mixer_merge_kernel(x_ref, o_ref, gaya_ref, gb_ref, wb_ref, wo_ref, h_ref):
    y_b = _dot(o_ref[...], wb_ref[...])
    merged = gaya_ref[...] + gb_ref[...] * y_b
    h_ref[...] = x_ref[...] + _dot(merged.astype(BF16), wo_ref[...])


def _mixer_merge(x2, o, gaya, gb, wb, wo):
    n, d = x2.shape
    tm = TOKEN_TILE
    row = lambda c: pl.BlockSpec((tm, c), lambda i: (i, 0))
    return pl.pallas_call(
        _mixer_merge_kernel,
        grid=(n // tm,),
        in_specs=[row(d), row(o.shape[1]), row(d), row(d), _resident(wb.shape), _resident(wo.shape)],
        out_specs=row(d),
        out_shape=jax.ShapeDtypeStruct((n, d), F32),
        compiler_params=pltpu.CompilerParams(dimension_semantics=("arbitrary",), vmem_limit_bytes=VMEM_LIMIT),
        name="mixer_merge",
    )(x2, o, gaya, gb, wb, wo)


def _mem_kv_kernel(mem_ref, g_ref, w_ref, kv_ref):
    mn = _rms(mem_ref[...], g_ref[...]).astype(BF16)
    kv_ref[...] = _dot(mn, w_ref[...]).astype(BF16)


def _mem_kv(mem2, g, w, *, mem_len):
    n, d = mem2.shape
    return pl.pallas_call(
        _mem_kv_kernel,
        grid=(n // mem_len,),
        in_specs=[pl.BlockSpec((mem_len, d), lambda i: (i, 0)), _resident(g.shape), _resident(w.shape)],
        out_specs=pl.BlockSpec((mem_len, w.shape[1]), lambda i: (i, 0)),
        out_shape=jax.ShapeDtypeStruct((n, w.shape[1]), BF16),
        compiler_params=pltpu.CompilerParams(dimension_semantics=("arbitrary",), vmem_limit_bytes=VMEM_LIMIT),
        name="mem_kv",
    )(mem2, g, w)


def _cross_attn_kernel(h_ref, g_ref, wq_ref, ck_ref, cv_ref, wo_ref, out_ref, *, q_scale):
    tm, d = h_ref.shape
    h = h_ref[...]
    hn = _rms(h, g_ref[...]).astype(BF16)
    cq = (_dot(hn, wq_ref[...]) * q_scale).astype(BF16)
    hd = d // CROSS_HEADS
    heads = []
    for a in range(CROSS_HEADS):
        cols = slice(a * hd, (a + 1) * hd)
        s = _dot_nt(cq[:, cols], ck_ref[:, cols])
        p = jnp.exp2(s - jnp.max(s, axis=-1, keepdims=True))
        l = jnp.sum(p, axis=-1, keepdims=True)
        heads.append(_dot(p.astype(BF16), cv_ref[:, cols]) / l)
    o = jnp.concatenate(heads, axis=-1).astype(BF16)
    out_ref[...] = h + _dot(o, wo_ref[...])


def _cross_attn(h, g, wq, ckv, wo, *, seq, mem_len):
    n, d = h.shape
    tm = TOKEN_TILE
    tiles_per_seq = seq // tm
    row = pl.BlockSpec((tm, d), lambda i: (i, 0))
    kern = functools.partial(_cross_attn_kernel, q_scale=(d // CROSS_HEADS) ** -0.5 * LOG2E)
    return pl.pallas_call(
        kern,
        grid=(n // tm,),
        in_specs=[row, _resident(g.shape), _resident(wq.shape),
                  pl.BlockSpec((mem_len, d), lambda i: (i // tiles_per_seq, 0)),
                  pl.BlockSpec((mem_len, d), lambda i: (i // tiles_per_seq, 1)),
                  _resident(wo.shape)],
        out_specs=row,
        out_shape=jax.ShapeDtypeStruct((n, d), F32),
        compiler_params=pltpu.CompilerParams(dimension_semantics=("arbitrary",), vmem_limit_bytes=VMEM_LIMIT),
        name="cross_attn",
    )(h, g, wq, ckv, ckv, wo)


def _mlp_kernel(h_ref, g_ref, w1_ref, w2_ref, gf_ref, out_ref, *, final_norm):
    h = h_ref[...]
    hn = _rms(h, g_ref[...]).astype(BF16)
    acc = h
    for c in range(w1_ref.shape[1] // FF_CHUNK):
        cols = slice(c * FF_CHUNK, (c + 1) * FF_CHUNK)
        a = jnp.maximum(_dot(hn, w1_ref[:, cols]), 0.0)
        acc = acc + _dot((a * a).astype(BF16), w2_ref[cols, :])
    out_ref[...] = _rms(acc, gf_ref[...]) if final_norm else acc


def _mlp(h, g, w1, w2, gf, *, final_norm):
    n, d = h.shape
    tm = TOKEN_TILE
    row = pl.BlockSpec((tm, d), lambda i: (i, 0))
    kern = functools.partial(_mlp_kernel, final_norm=final_norm)
    return pl.pallas_call(
        kern,
        grid=(n // tm,),
        in_specs=[row, _resident(g.shape), _resident(w1.shape), _resident(w2.shape), _resident(gf.shape)],
        out_specs=row,
        out_shape=jax.ShapeDtypeStruct((n, d), F32),
        compiler_params=pltpu.CompilerParams(dimension_semantics=("arbitrary",), vmem_limit_bytes=VMEM_LIMIT),
        name="mlp",
    )(h, g, w1, w2, gf)


def kernel(x, mem, norm_mix_g, w_in, w_pool_group, pool_scale, w_a_proj, lambda_q1, lambda_k1, lambda_q2, lambda_k2, subln_g, rel_bias, w_b_proj, w_gate, b_gate, w_out, norm_cross_g, norm_mem_g, w_cq, w_ckv, w_co, norm_mlp_g, w_ff1, w_ff2, final_norm_g):
    batch, seq, d = x.shape
    mem_len = mem.shape[1]
    depth = w_in.shape[0]
    pool_width = w_a_proj.shape[1]
    assert pool_width == d and seq % TOKEN_TILE == 0 and seq % ATTN_TILE == 0
    row = lambda a: a.reshape(1, -1).astype(F32)
    bf = lambda a: a.astype(BF16)

    near_bias = _rel_bias_tiles(rel_bias.astype(F32))
    h = x.reshape(batch * seq, d)
    mem2 = mem.reshape(batch * mem_len, d)
    for l in range(depth):
        lambda_init = LAMBDA_INIT_BASE - LAMBDA_INIT_AMP * math.exp(-LAMBDA_INIT_RATE * l)
        q, k, v, gaya, gb = _mixer_front(
            h, row(norm_mix_g[l]), bf(w_in[l, :, :pool_width]), bf(w_in[l, :, pool_width:]), bf(w_gate[l]),
            row(b_gate[l]), bf(w_pool_group[l]), row(pool_scale[l]), bf(w_a_proj[l]),
            seq=seq, q_scale=DIFF_HEAD_DIM ** -0.5 * LOG2E)
        lam_vecs = jnp.stack([lambda_q1[l], lambda_k1[l], lambda_q2[l], lambda_k2[l]]).astype(F32)
        o = _diff_attn(lam_vecs, q, k, v, near_bias, row(subln_g[l]), batch=batch, seq=seq, lambda_init=lambda_init)
        h = _mixer_merge(h, o, gaya, gb, bf(w_b_proj[l]), bf(w_out[l]))
        ckv = _mem_kv(mem2, row(norm_mem_g[l]), bf(w_ckv[l]), mem_len=mem_len)
        h = _cross_attn(h, row(norm_cross_g[l]), bf(w_cq[l]), ckv, bf(w_co[l]), seq=seq, mem_len=mem_len)
        h = _mlp(h, row(norm_mlp_g[l]), bf(w_ff1[l]), bf(w_ff2[l]), row(final_norm_g), final_norm=(l == depth - 1))
    return h.reshape(batch, seq, d)
```

```python
import functools
import math

import jax
import jax.numpy as jnp
from jax import lax
from jax.experimental import pallas as pl
from jax.experimental.pallas import tpu as pltpu

CHUNK = 64
POOL_WINDOWS = (2, 4, 8, 16)
DIFF_HEADS = 8
DIFF_HEAD_DIM = 64
LAMBDA_INIT_BASE = 0.8
LAMBDA_INIT_AMP = 0.6
LAMBDA_INIT_RATE = 0.3
REL_BUCKETS = 32
REL_MAX_DIST = 128
CROSS_HEADS = 4
EPS = 1e-6
NEG_INF = -1e30
LOG2E = math.log2(math.e)

POOL_HALO = 16
TOKEN_TILE = 512
ATTN_TILE = 256
FF_CHUNK = 1024
VMEM_LIMIT = 56 * 1024 * 1024

BF16 = jnp.bfloat16
F32 = jnp.float32


def _resident(shape):
    n = len(shape)
    return pl.BlockSpec(shape, lambda *_: (0,) * n, pipeline_mode=pl.Buffered(1))


def _rms(x, g):
    return x * lax.rsqrt(jnp.mean(x * x, axis=-1, keepdims=True) + EPS) * g


def _dot(a, b):
    return jnp.dot(a, b, preferred_element_type=F32)


def _dot_nt(a, b):
    return lax.dot_general(a, b, (((1,), (1,)), ((), ())), preferred_element_type=F32)


def _mixer_front_kernel(x_ref, g_ref, wu_ref, wqkv_ref, wg_ref, bg_ref, wp_ref, ps_ref, wa_ref,
                        q_ref, k_ref, v_ref, gaya_ref, gb_ref, ubuf_ref, *, tiles_per_seq, q_scale):
    tm, d = x_ref.shape
    i = pl.program_id(0)
    tile_in_seq = i % tiles_per_seq
    xn = _rms(x_ref[...], g_ref[...]).astype(BF16)

    width = q_ref.shape[1]
    q_ref[...] = (_dot(xn, wqkv_ref[:, 0:width]) * q_scale).astype(BF16)
    k_ref[...] = _dot(xn, wqkv_ref[:, width:2 * width]).astype(BF16)
    v_ref[...] = _dot(xn, wqkv_ref[:, 2 * width:3 * width]).astype(BF16)

    @pl.when(tile_in_seq == 0)
    def _():
        ubuf_ref[0:POOL_HALO, :] = jnp.zeros((POOL_HALO, d), F32)

    ubuf_ref[POOL_HALO:POOL_HALO + tm, :] = _dot(xn, wu_ref[...])
    pos = tile_in_seq * tm + lax.broadcasted_iota(jnp.int32, (tm, 1), 0)
    gd = d // len(POOL_WINDOWS)
    mapped = []
    for g, w in enumerate(POOL_WINDOWS):
        cols = slice(g * gd, (g + 1) * gd)
        u_g = ubuf_ref[POOL_HALO:POOL_HALO + tm, cols]
        acc = u_g
        for j in range(1, w):
            acc = acc + ubuf_ref[POOL_HALO - j:POOL_HALO - j + tm, cols]
        inv_count = 1.0 / jnp.minimum(pos + 1, w).astype(F32)
        pooled = acc * inv_count - u_g
        mapped.append(_dot(pooled.astype(BF16), wp_ref[g]))
    ubuf_ref[0:POOL_HALO, :] = ubuf_ref[tm:tm + POOL_HALO, :]
    y = jnp.concatenate(mapped, axis=-1) * ps_ref[...]
    y_a = _dot(y.astype(BF16), wa_ref[...])

    z = _dot(xn, wg_ref[...]) + bg_ref[...]
    gates = 1.0 / (1.0 + jnp.exp(-z))
    gaya_ref[...] = gates[:, :d] * y_a
    gb_ref[...] = gates[:, d:]


def _mixer_front(x2, g, wu, wqkv, wg, bg, wp, ps, wa, *, seq, q_scale):
    n, d = x2.shape
    tm = TOKEN_TILE
    width = wqkv.shape[1] // 3
    row = lambda c: pl.BlockSpec((tm, c), lambda i: (i, 0))
    kern = functools.partial(_mixer_front_kernel, tiles_per_seq=seq // tm, q_scale=q_scale)
    return pl.pallas_call(
        kern,
        grid=(n // tm,),
        in_specs=[row(d), _resident(g.shape), _resident(wu.shape), _resident(wqkv.shape), _resident(wg.shape),
                  _resident(bg.shape), _resident(wp.shape), _resident(ps.shape), _resident(wa.shape)],
        out_specs=[row(width), row(width), row(width), row(d), row(d)],
        out_shape=[jax.ShapeDtypeStruct((n, width), BF16)] * 3 + [jax.ShapeDtypeStruct((n, d), F32)] * 2,
        scratch_shapes=[pltpu.VMEM((POOL_HALO + tm, d), F32)],
        compiler_params=pltpu.CompilerParams(dimension_semantics=("arbitrary",), vmem_limit_bytes=VMEM_LIMIT),
        name="mixer_front",
    )(x2, g, wu, wqkv, wg, bg, wp, ps, wa)


def _rel_bucket(rel):
    nb = REL_BUCKETS // 2
    ret = (rel > 0).astype(jnp.int32) * nb
    n = jnp.abs(rel)
    max_exact = nb // 2
    nf = jnp.maximum(n, 1).astype(F32)
    large = max_exact + (jnp.log(nf / max_exact) / math.log(REL_MAX_DIST / max_exact)
                         * (nb - max_exact)).astype(jnp.int32)
    large = jnp.minimum(large, nb - 1)
    return ret + jnp.where(n < max_exact, n, large)


def _rel_bias_kernel(table_ref, bucket_ref, out_ref, *, far_bucket):
    h = pl.program_id(0)
    bucket = bucket_ref[...]
    far = table_ref[far_bucket, h]
    acc = jnp.zeros(bucket.shape, F32)
    for b in range(REL_BUCKETS):
        acc = jnp.where(bucket == b, table_ref[b, h] - far, acc)
    out_ref[...] = acc * LOG2E


def _rel_bias_tiles(rel_bias):
    t = ATTN_TILE
    assert t > REL_MAX_DIST
    heads = rel_bias.shape[1]
    r = jnp.arange(t, dtype=jnp.int32)[:, None]
    c = jnp.arange(t, dtype=jnp.int32)[None, :]
    bucket = jnp.stack([_rel_bucket(c - t - r), _rel_bucket(c - r)])
    bucket = jnp.concatenate([bucket, bucket], axis=1)
    kern = functools.partial(_rel_bias_kernel, far_bucket=REL_BUCKETS // 2 - 1)
    return pl.pallas_call(
        kern,
        grid=(heads,),
        in_specs=[pl.BlockSpec(memory_space=pltpu.SMEM), pl.BlockSpec((2, 2 * t, t), lambda h: (0, 0, 0))],
        out_specs=pl.BlockSpec((None, 2, 2 * t, t), lambda h: (h, 0, 0, 0)),
        out_shape=jax.ShapeDtypeStruct((heads, 2, 2 * t, t), F32),
        name="rel_bias",
    )(rel_bias, bucket)


def _diff_attn_kernel(lam_ref, q_ref, k_ref, v_ref, nb_ref, sg_ref, o_ref, s_ref, *, lambda_init):
    seq, hd = q_ref.shape
    t = ATTN_TILE
    half = t // 2
    lane = lax.broadcasted_iota(jnp.int32, (t, hd), 1)
    r = lax.broadcasted_iota(jnp.int32, (2 * t, t), 0) % t
    c = lax.broadcasted_iota(jnp.int32, (2 * t, t), 1)
    allowed = c // CHUNK <= r // CHUNK
    lam = (jnp.exp(jnp.sum(lam_ref[0:1, :] * lam_ref[1:2, :], axis=-1, keepdims=True))
           - jnp.exp(jnp.sum(lam_ref[2:3, :] * lam_ref[3:4, :], axis=-1, keepdims=True)) + lambda_init)

    for i in range(seq // t):
        rows = slice(i * t, (i + 1) * t)
        q = q_ref[rows, :]
        zero = jnp.zeros_like(q)
        qq = jnp.concatenate([jnp.where(lane < hd // 2, q, zero), jnp.where(lane >= hd // 2, q, zero)], axis=0)
        slot = i % 2
        mx = None
        for j in range(i + 1):
            s = _dot_nt(qq, k_ref[j * t:(j + 1) * t, :])
            if j == i:
                s = jnp.where(allowed, s + nb_ref[1], NEG_INF)
            elif j == i - 1:
                s = s + nb_ref[0]
            s_ref[slot, j] = s
            tile_max = jnp.maximum(s[:, :half], s[:, half:])
            mx = tile_max if mx is None else jnp.maximum(mx, tile_max)
        m = jnp.broadcast_to(jnp.max(mx, axis=-1, keepdims=True), (2 * t, half))

        l = jnp.zeros((2 * t, half), F32)
        acc = jnp.zeros((2 * t, hd), F32)
        for j in range(i + 1):
            s = s_ref[slot, j]
            p_lo = jnp.exp2(s[:, :half] - m)
            p_hi = jnp.exp2(s[:, half:] - m)
            l = l + (p_lo + p_hi)
            p = jnp.concatenate([p_lo, p_hi], axis=1).astype(BF16)
            acc = acc + _dot(p, v_ref[j * t:(j + 1) * t, :])

        o_maps = acc / jnp.sum(l, axis=-1, keepdims=True)
        o = o_maps[:t] - lam * o_maps[t:]
        o_ref[rows, :] = (_rms(o, sg_ref[...]) * (1.0 - lambda_init)).astype(o_ref.dtype)


def _diff_attn(lam_vecs, q, k, v, near_bias, subln_g, *, batch, seq, lambda_init):
    n, width = q.shape
    t = ATTN_TILE
    hd = width // DIFF_HEADS
    nq = seq // t
    q3, k3, v3 = (a.reshape(batch, seq, width) for a in (q, k, v))
    head_block = pl.BlockSpec((None, seq, hd), lambda b, h: (b, 0, h))
    kern = functools.partial(_diff_attn_kernel, lambda_init=lambda_init)
    out = pl.pallas_call(
        kern,
        grid=(batch, DIFF_HEADS),
        in_specs=[
            pl.BlockSpec(lam_vecs.shape, lambda b, h: (0, 0)),
            head_block, head_block, head_block,
            pl.BlockSpec((None, 2, 2 * t, t), lambda b, h: (h, 0, 0, 0)),
            pl.BlockSpec(subln_g.shape, lambda b, h: (0, 0)),
        ],
        out_specs=head_block,
        out_shape=jax.ShapeDtypeStruct((batch, seq, width), BF16),
        scratch_shapes=[pltpu.VMEM((2, nq, 2 * t, t), F32)],
        compiler_params=pltpu.CompilerParams(dimension_semantics=("arbitrary",) * 2, vmem_limit_bytes=VMEM_LIMIT),
        name="diff_attn",
    )(lam_vecs, q3, k3, v3, near_bias, subln_g)
    return out.reshape(n, width)


def _mixer_merge_kernel(x_ref, o_ref, gaya_ref, gb_ref, wb_ref, wo_ref, h_ref):
    y_b = _dot(o_ref[...], wb_ref[...])
    merged = gaya_ref[...] + gb_ref[...] * y_b
    h_ref[...] = x_ref[...] + _dot(merged.astype(BF16), wo_ref[...])


def _mixer_merge(x2, o, gaya, gb, wb, wo):
    n, d = x2.shape
    tm = TOKEN_TILE
    row = lambda c: pl.BlockSpec((tm, c), lambda i: (i, 0))
    return pl.pallas_call(
        _mixer_merge_kernel,
        grid=(n // tm,),
        in_specs=[row(d), row(o.shape[1]), row(d), row(d), _resident(wb.shape), _resident(wo.shape)],
        out_specs=row(d),
        out_shape=jax.ShapeDtypeStruct((n, d), F32),
        compiler_params=pltpu.CompilerParams(dimension_semantics=("arbitrary",), vmem_limit_bytes=VMEM_LIMIT),
        name="mixer_merge",
    )(x2, o, gaya, gb, wb, wo)


def _mem_kv_kernel(mem_ref, g_ref, w_ref, kv_ref):
    mn = _rms(mem_ref[...], g_ref[...]).astype(BF16)
    kv_ref[...] = _dot(mn, w_ref[...]).astype(BF16)


def _mem_kv(mem2, g, w, *, mem_len):
    n, d = mem2.shape
    return pl.pallas_call(
        _mem_kv_kernel,
        grid=(n // mem_len,),
        in_specs=[pl.BlockSpec((mem_len, d), lambda i: (i, 0)), _resident(g.shape), _resident(w.shape)],
        out_specs=pl.BlockSpec((mem_len, w.shape[1]), lambda i: (i, 0)),
        out_shape=jax.ShapeDtypeStruct((n, w.shape[1]), BF16),
        compiler_params=pltpu.CompilerParams(dimension_semantics=("arbitrary",), vmem_limit_bytes=VMEM_LIMIT),
        name="mem_kv",
    )(mem2, g, w)


def _cross_attn_kernel(h_ref, g_ref, wq_ref, ck_ref, cv_ref, wo_ref, out_ref, *, q_scale):
    tm, d = h_ref.shape
    h = h_ref[...]
    hn = _rms(h, g_ref[...]).astype(BF16)
    cq = (_dot(hn, wq_ref[...]) * q_scale).astype(BF16)
    hd = d // CROSS_HEADS
    heads = []
    for a in range(CROSS_HEADS):
        cols = slice(a * hd, (a + 1) * hd)
        s = _dot_nt(cq[:, cols], ck_ref[:, cols])
        p = jnp.exp2(s - jnp.max(s, axis=-1, keepdims=True))
        l = jnp.sum(p, axis=-1, keepdims=True)
        heads.append(_dot(p.astype(BF16), cv_ref[:, cols]) / l)
    o = jnp.concatenate(heads, axis=-1).astype(BF16)
    out_ref[...] = h + _dot(o, wo_ref[...])


def _cross_attn(h, g, wq, ckv, wo, *, seq, mem_len):
    n, d = h.shape
    tm = TOKEN_TILE
    tiles_per_seq = seq // tm
    row = pl.BlockSpec((tm, d), lambda i: (i, 0))
    kern = functools.partial(_cross_attn_kernel, q_scale=(d // CROSS_HEADS) ** -0.5 * LOG2E)
    return pl.pallas_call(
        kern,
        grid=(n // tm,),
        in_specs=[row, _resident(g.shape), _resident(wq.shape),
                  pl.BlockSpec((mem_len, d), lambda i: (i // tiles_per_seq, 0)),
                  pl.BlockSpec((mem_len, d), lambda i: (i // tiles_per_seq, 1)),
                  _resident(wo.shape)],
        out_specs=row,
        out_shape=jax.ShapeDtypeStruct((n, d), F32),
        compiler_params=pltpu.CompilerParams(dimension_semantics=("arbitrary",), vmem_limit_bytes=VMEM_LIMIT),
        name="cross_attn",
    )(h, g, wq, ckv, ckv, wo)


def _mlp_kernel(h_ref, g_ref, w1_ref, w2_ref, gf_ref, out_ref, *, final_norm):
    h = h_ref[...]
    hn = _rms(h, g_ref[...]).astype(BF16)
    acc = h
    for c in range(w1_ref.shape[1] // FF_CHUNK):
        cols = slice(c * FF_CHUNK, (c + 1) * FF_CHUNK)
        a = jnp.maximum(_dot(hn, w1_ref[:, cols]), 0.0)
        acc = acc + _dot((a * a).astype(BF16), w2_ref[cols, :])
    out_ref[...] = _rms(acc, gf_ref[...]) if final_norm else acc


def _mlp(h, g, w1, w2, gf, *, final_norm):
    n, d = h.shape
    tm = TOKEN_TILE
    row = pl.BlockSpec((tm, d), lambda i: (i, 0))
    kern = functools.partial(_mlp_kernel, final_norm=final_norm)
    return pl.pallas_call(
        kern,
        grid=(n // tm,),
        in_specs=[row, _resident(g.shape), _resident(w1.shape), _resident(w2.shape), _resident(gf.shape)],
        out_specs=row,
        out_shape=jax.ShapeDtypeStruct((n, d), F32),
        compiler_params=pltpu.CompilerParams(dimension_semantics=("arbitrary",), vmem_limit_bytes=VMEM_LIMIT),
        name="mlp",
    )(h, g, w1, w2, gf)


def kernel(x, mem, norm_mix_g, w_in, w_pool_group, pool_scale, w_a_proj, lambda_q1, lambda_k1, lambda_q2, lambda_k2, subln_g, rel_bias, w_b_proj, w_gate, b_gate, w_out, norm_cross_g, norm_mem_g, w_cq, w_ckv, w_co, norm_mlp_g, w_ff1, w_ff2, final_norm_g):
    batch, seq, d = x.shape
    mem_len = mem.shape[1]
    depth = w_in.shape[0]
    pool_width = w_a_proj.shape[1]
    assert pool_width == d and seq % TOKEN_TILE == 0 and seq % ATTN_TILE == 0
    row = lambda a: a.reshape(1, -1).astype(F32)
    bf = lambda a: a.astype(BF16)

    near_bias = _rel_bias_tiles(rel_bias.astype(F32))
    h = x.reshape(batch * seq, d)
    mem2 = mem.reshape(batch * mem_len, d)
    for l in range(depth):
        lambda_init = LAMBDA_INIT_BASE - LAMBDA_INIT_AMP * math.exp(-LAMBDA_INIT_RATE * l)
        q, k, v, gaya, gb = _mixer_front(
            h, row(norm_mix_g[l]), bf(w_in[l, :, :pool_width]), bf(w_in[l, :, pool_width:]), bf(w_gate[l]),
            row(b_gate[l]), bf(w_pool_group[l]), row(pool_scale[l]), bf(w_a_proj[l]),
            seq=seq, q_scale=DIFF_HEAD_DIM ** -0.5 * LOG2E)
        lam_vecs = jnp.stack([lambda_q1[l], lambda_k1[l], lambda_q2[l], lambda_k2[l]]).astype(F32)
        o = _diff_attn(lam_vecs, q, k, v, near_bias, row(subln_g[l]), batch=batch, seq=seq, lambda_init=lambda_init)
        h = _mixer_merge(h, o, gaya, gb, bf(w_b_proj[l]), bf(w_out[l]))
        ckv = _mem_kv(mem2, row(norm_mem_g[l]), bf(w_ckv[l]), mem_len=mem_len)
        h = _cross_attn(h, row(norm_cross_g[l]), bf(w_cq[l]), ckv, bf(w_co[l]), seq=seq, mem_len=mem_len)
        h = _mlp(h, row(norm_mlp_g[l]), bf(w_ff1[l]), bf(w_ff2[l]), row(final_norm_g), final_norm=(l == depth - 1))
    return h.reshape(batch, seq, d)
```

```python
import functools
import math

import jax
import jax.numpy as jnp
from jax import lax
from jax.experimental import pallas as pl
from jax.experimental.pallas import tpu as pltpu

CHUNK = 64
POOL_WINDOWS = (2, 4, 8, 16)
DIFF_HEADS = 8
DIFF_HEAD_DIM = 64
LAMBDA_INIT_BASE = 0.8
LAMBDA_INIT_AMP = 0.6
LAMBDA_INIT_RATE = 0.3
REL_BUCKETS = 32
REL_MAX_DIST = 128
CROSS_HEADS = 4
EPS = 1e-6
NEG_INF = -1e30
LOG2E = math.log2(math.e)

POOL_HALO = 16
TOKEN_TILE = 512
ATTN_TILE = 256
ONES_ROWS = 16
ATTN_LEAD = 2
FINISH_LAG = 3
FF_CHUNK = 1024
VMEM_LIMIT = 56 * 1024 * 1024

BF16 = jnp.bfloat16
F32 = jnp.float32


def _resident(shape):
    n = len(shape)
    return pl.BlockSpec(shape, lambda *_: (0,) * n, pipeline_mode=pl.Buffered(1))


def _rms(x, g):
    return x * lax.rsqrt(jnp.mean(x * x, axis=-1, keepdims=True) + EPS) * g


def _dot(a, b):
    return jnp.dot(a, b, preferred_element_type=F32)


def _dot_nt(a, b):
    return lax.dot_general(a, b, (((1,), (1,)), ((), ())), preferred_element_type=F32)


def _mixer_front_kernel(x_ref, g_ref, wu_ref, wqkv_ref, wg_ref, bg_ref, wp_ref, ps_ref, wa_ref,
                        q_ref, k_ref, v_ref, gaya_ref, gb_ref, ubuf_ref, *, tiles_per_seq, q_scale):
    tm, d = x_ref.shape
    i = pl.program_id(0)
    tile_in_seq = i % tiles_per_seq
    xn = _rms(x_ref[...], g_ref[...]).astype(BF16)

    width = q_ref.shape[1]
    q_ref[...] = (_dot(xn, wqkv_ref[:, 0:width]) * q_scale).astype(BF16)
    k_ref[...] = _dot(xn, wqkv_ref[:, width:2 * width]).astype(BF16)
    v_ref[...] = _dot(xn, wqkv_ref[:, 2 * width:3 * width]).astype(BF16)

    @pl.when(tile_in_seq == 0)
    def _():
        ubuf_ref[0:POOL_HALO, :] = jnp.zeros((POOL_HALO, d), F32)

    ubuf_ref[POOL_HALO:POOL_HALO + tm, :] = _dot(xn, wu_ref[...])
    pos = tile_in_seq * tm + lax.broadcasted_iota(jnp.int32, (tm, 1), 0)
    gd = d // len(POOL_WINDOWS)
    mapped = []
    for g, w in enumerate(POOL_WINDOWS):
        cols = slice(g * gd, (g + 1) * gd)
        u_g = ubuf_ref[POOL_HALO:POOL_HALO + tm, cols]
        acc = u_g
        for j in range(1, w):
            acc = acc + ubuf_ref[POOL_HALO - j:POOL_HALO - j + tm, cols]
        inv_count = 1.0 / jnp.minimum(pos + 1, w).astype(F32)
        pooled = acc * inv_count - u_g
        mapped.append(_dot(pooled.astype(BF16), wp_ref[g]))
    ubuf_ref[0:POOL_HALO, :] = ubuf_ref[tm:tm + POOL_HALO, :]
    y = jnp.concatenate(mapped, axis=-1) * ps_ref[...]
    y_a = _dot(y.astype(BF16), wa_ref[...])

    z = _dot(xn, wg_ref[...]) + bg_ref[...]
    gates = 1.0 / (1.0 + jnp.exp(-z))
    gaya_ref[...] = gates[:, :d] * y_a
    gb_ref[...] = gates[:, d:]


def _mixer_front(x2, g, wu, wqkv, wg, bg, wp, ps, wa, *, seq, q_scale):
    n, d = x2.shape
    tm = TOKEN_TILE
    width = wqkv.shape[1] // 3
    row = lambda c: pl.BlockSpec((tm, c), lambda i: (i, 0))
    kern = functools.partial(_mixer_front_kernel, tiles_per_seq=seq // tm, q_scale=q_scale)
    return pl.pallas_call(
        kern,
        grid=(n // tm,),
        in_specs=[row(d), _resident(g.shape), _resident(wu.shape), _resident(wqkv.shape), _resident(wg.shape),
                  _resident(bg.shape), _resident(wp.shape), _resident(ps.shape), _resident(wa.shape)],
        out_specs=[row(width), row(width), row(width), row(d), row(d)],
        out_shape=[jax.ShapeDtypeStruct((n, width), BF16)] * 3 + [jax.ShapeDtypeStruct((n, d), F32)] * 2,
        scratch_shapes=[pltpu.VMEM((POOL_HALO + tm, d), F32)],
        compiler_params=pltpu.CompilerParams(dimension_semantics=("arbitrary",), vmem_limit_bytes=VMEM_LIMIT),
        name="mixer_front",
    )(x2, g, wu, wqkv, wg, bg, wp, ps, wa)


def _rel_bucket(rel):
    nb = REL_BUCKETS // 2
    ret = (rel > 0).astype(jnp.int32) * nb
    n = jnp.abs(rel)
    max_exact = nb // 2
    nf = jnp.maximum(n, 1).astype(F32)
    large = max_exact + (jnp.log(nf / max_exact) / math.log(REL_MAX_DIST / max_exact)
                         * (nb - max_exact)).astype(jnp.int32)
    large = jnp.minimum(large, nb - 1)
    return ret + jnp.where(n < max_exact, n, large)


def _rel_bias_kernel(table_ref, bucket_ref, out_ref, *, far_bucket):
    h = pl.program_id(0)
    bucket = bucket_ref[...]
    far = table_ref[far_bucket, h]
    acc = jnp.zeros(bucket.shape, F32)
    for b in range(REL_BUCKETS):
        acc = jnp.where(bucket == b, table_ref[b, h] - far, acc)
    out_ref[...] = acc * LOG2E


def _rel_bias_tiles(rel_bias):
    t = ATTN_TILE
    assert t > REL_MAX_DIST
    heads = rel_bias.shape[1]
    key = jnp.arange(t, dtype=jnp.int32)[:, None]
    query = jnp.arange(t, dtype=jnp.int32)[None, :]
    bucket = jnp.stack([_rel_bucket(key - t - query), _rel_bucket(key - query)])
    bucket = jnp.concatenate([bucket, bucket], axis=2)
    kern = functools.partial(_rel_bias_kernel, far_bucket=REL_BUCKETS // 2 - 1)
    return pl.pallas_call(
        kern,
        grid=(heads,),
        in_specs=[pl.BlockSpec(memory_space=pltpu.SMEM), pl.BlockSpec((2, t, 2 * t), lambda h: (0, 0, 0))],
        out_specs=pl.BlockSpec((None, 2, t, 2 * t), lambda h: (h, 0, 0, 0)),
        out_shape=jax.ShapeDtypeStruct((heads, 2, t, 2 * t), F32),
        name="rel_bias",
    )(rel_bias, bucket)


def _attn_schedule(nq):
    tiles = [(i, j) for i in range(nq) for j in range(i + 1)]
    events, live, n_slots = [], set(), 0
    pending = []
    pv_pos = 0
    for pos in range(2 * len(tiles)):
        if pos < len(tiles):
            events.append(("score",) + tiles[pos])
            live.add(tiles[pos][0])
            n_slots = max(n_slots, len(live))
        if pv_pos < len(tiles):
            i, j = tiles[pv_pos]
            scored = min(pos + 1, len(tiles))
            if scored >= min(tiles.index((i, i)) + 1 + ATTN_LEAD, len(tiles)):
                events.append(("pv", i, j))
                pv_pos += 1
                if j == i:
                    live.discard(i)
                    pending.append([i, FINISH_LAG + 1])
        for item in pending:
            item[1] -= 1
        while pending and pending[0][1] <= 0:
            events.append(("finish", pending.pop(0)[0], 0))
    events += [("finish", i, 0) for i, _ in pending]
    assert pv_pos == len(tiles)
    return events, n_slots


def _diff_attn_kernel(base_ref, lam_ref, q_ref, k_ref, v_ref, nb_ref, sg_ref, o_ref, s_ref, vt_ref, *, lambda_init):
    seq, hd = q_ref.shape
    t = ATTN_TILE
    nq = seq // t
    events, n_slots = _attn_schedule(nq)
    assert s_ref.shape[0] == n_slots
    base = base_ref[0]
    lane = lax.broadcasted_iota(jnp.int32, (t, hd), 1)
    key_chunk = lax.broadcasted_iota(jnp.int32, (t, 2 * t), 0) // CHUNK
    query_chunk = (lax.broadcasted_iota(jnp.int32, (t, 2 * t), 1) % t) // CHUNK
    allowed = key_chunk <= query_chunk
    lam = (jnp.exp(jnp.sum(lam_ref[0:1, :] * lam_ref[1:2, :], axis=-1, keepdims=True))
           - jnp.exp(jnp.sum(lam_ref[2:3, :] * lam_ref[3:4, :], axis=-1, keepdims=True)) + lambda_init)

    for j in range(nq):
        vt_ref[0:hd, j * t:(j + 1) * t] = v_ref[j * t:(j + 1) * t, :].astype(F32).T.astype(BF16)
    vt_ref[hd:, :] = jnp.ones((ONES_ROWS, seq), BF16)

    def stacked_q(i):
        q = q_ref[i * t:(i + 1) * t, :]
        zero = jnp.zeros_like(q)
        return jnp.concatenate([jnp.where(lane < hd // 2, q, zero), jnp.where(lane >= hd // 2, q, zero)], axis=0)

    def score_tile(i, j, qq):
        s = _dot_nt(k_ref[j * t:(j + 1) * t, :], qq)
        if j == i:
            s = jnp.where(allowed, s + nb_ref[1], NEG_INF)
        elif j == i - 1:
            s = s + nb_ref[0]
        s_ref[base + i % n_slots, j] = s
        return jnp.max(s, axis=0, keepdims=True)

    def finish(i, acc):
        o_maps = acc[:hd] / acc[hd:hd + 1]
        o = (o_maps[:, :t] - lam * o_maps[:, t:]).T
        o_ref[i * t:(i + 1) * t, :] = (_rms(o, sg_ref[...]) * (1.0 - lambda_init)).astype(o_ref.dtype)

    qq, mx, acc = {}, {}, {}
    for kind, i, j in events:
        if kind == "score":
            if j == 0:
                qq[i] = stacked_q(i)
            tile_max = score_tile(i, j, qq[i])
            mx[i] = tile_max if j == 0 else jnp.maximum(mx[i], tile_max)
        elif kind == "pv":
            p = jnp.exp2((s_ref[base + i % n_slots, j] - mx[i]).astype(BF16))
            pv = _dot(vt_ref[:, j * t:(j + 1) * t], p)
            acc[i] = pv if j == 0 else acc[i] + pv
        else:
            finish(i, acc.pop(i))


def _diff_attn(lam_vecs, q, k, v, near_bias, subln_g, *, batch, seq, lambda_init):
    n, width = q.shape
    t = ATTN_TILE
    hd = width // DIFF_HEADS
    nq = seq // t
    q3, k3, v3 = (a.reshape(batch, seq, width) for a in (q, k, v))
    head_block = pl.BlockSpec((None, seq, hd), lambda b, h: (b, 0, h))
    kern = functools.partial(_diff_attn_kernel, lambda_init=lambda_init)
    out = pl.pallas_call(
        kern,
        grid=(batch, DIFF_HEADS),
        in_specs=[
            pl.BlockSpec(memory_space=pltpu.SMEM),
            pl.BlockSpec(lam_vecs.shape, lambda b, h: (0, 0)),
            head_block, head_block, head_block,
            pl.BlockSpec((None, 2, t, 2 * t), lambda b, h: (h, 0, 0, 0)),
            pl.BlockSpec(subln_g.shape, lambda b, h: (0, 0)),
        ],
        out_specs=head_block,
        out_shape=jax.ShapeDtypeStruct((batch, seq, width), BF16),
        scratch_shapes=[pltpu.VMEM((_attn_schedule(nq)[1], nq, t, 2 * t), F32), pltpu.VMEM((hd + ONES_ROWS, seq), BF16)],
        compiler_params=pltpu.CompilerParams(dimension_semantics=("arbitrary",) * 2, vmem_limit_bytes=VMEM_LIMIT),
        name="diff_attn",
    )(jnp.zeros((1,), jnp.int32), lam_vecs, q3, k3, v3, near_bias, subln_g)
    return out.reshape(n, width)


def _mixer_merge_kernel(x_ref, o_ref, gaya_ref, gb_ref, wb_ref, wo_ref, h_ref):
    y_b = _dot(o_ref[...], wb_ref[...])
    merged = gaya_ref[...] + gb_ref[...] * y_b
    h_ref[...] = x_ref[...] + _dot(merged.astype(BF16), wo_ref[...])


def _mixer_merge(x2, o, gaya, gb, wb, wo):
    n, d = x2.shape
    tm = TOKEN_TILE
    row = lambda c: pl.BlockSpec((tm, c), lambda i: (i, 0))
    return pl.pallas_call(
        _mixer_merge_kernel,
        grid=(n // tm,),
        in_specs=[row(d), row(o.shape[1]), row(d), row(d), _resident(wb.shape), _resident(wo.shape)],
        out_specs=row(d),
        out_shape=jax.ShapeDtypeStruct((n, d), F32),
        compiler_params=pltpu.CompilerParams(dimension_semantics=("arbitrary",), vmem_limit_bytes=VMEM_LIMIT),
        name="mixer_merge",
    )(x2, o, gaya, gb, wb, wo)


def _mem_kv_kernel(mem_ref, g_ref, w_ref, kv_ref):
    mn = _rms(mem_ref[...], g_ref[...]).astype(BF16)
    kv_ref[...] = _dot(mn, w_ref[...]).astype(BF16)


def _mem_kv(mem2, g, w, *, mem_len):
    n, d = mem2.shape
    return pl.pallas_call(
        _mem_kv_kernel,
        grid=(n // mem_len,),
        in_specs=[pl.BlockSpec((mem_len, d), lambda i: (i, 0)), _resident(g.shape), _resident(w.shape)],
        out_specs=pl.BlockSpec((mem_len, w.shape[1]), lambda i: (i, 0)),
        out_shape=jax.ShapeDtypeStruct((n, w.shape[1]), BF16),
        compiler_params=pltpu.CompilerParams(dimension_semantics=("arbitrary",), vmem_limit_bytes=VMEM_LIMIT),
        name="mem_kv",
    )(mem2, g, w)


def _cross_attn_kernel(h_ref, g_ref, wq_ref, ck_ref, cv_ref, wo_ref, out_ref, *, q_scale):
    tm, d = h_ref.shape
    h = h_ref[...]
    hn = _rms(h, g_ref[...]).astype(BF16)
    cq = (_dot(hn, wq_ref[...]) * q_scale).astype(BF16)
    hd = d // CROSS_HEADS
    heads = []
    for a in range(CROSS_HEADS):
        cols = slice(a * hd, (a + 1) * hd)
        s = _dot_nt(cq[:, cols], ck_ref[:, cols])
        p = jnp.exp2(s - jnp.max(s, axis=-1, keepdims=True))
        l = jnp.sum(p, axis=-1, keepdims=True)
        heads.append(_dot(p.astype(BF16), cv_ref[:, cols]) / l)
    o = jnp.concatenate(heads, axis=-1).astype(BF16)
    out_ref[...] = h + _dot(o, wo_ref[...])


def _cross_attn(h, g, wq, ckv, wo, *, seq, mem_len):
    n, d = h.shape
    tm = TOKEN_TILE
    tiles_per_seq = seq // tm
    row = pl.BlockSpec((tm, d), lambda i: (i, 0))
    kern = functools.partial(_cross_attn_kernel, q_scale=(d // CROSS_HEADS) ** -0.5 * LOG2E)
    return pl.pallas_call(
        kern,
        grid=(n // tm,),
        in_specs=[row, _resident(g.shape), _resident(wq.shape),
                  pl.BlockSpec((mem_len, d), lambda i: (i // tiles_per_seq, 0)),
                  pl.BlockSpec((mem_len, d), lambda i: (i // tiles_per_seq, 1)),
                  _resident(wo.shape)],
        out_specs=row,
        out_shape=jax.ShapeDtypeStruct((n, d), F32),
        compiler_params=pltpu.CompilerParams(dimension_semantics=("arbitrary",), vmem_limit_bytes=VMEM_LIMIT),
        name="cross_attn",
    )(h, g, wq, ckv, ckv, wo)


def _mlp_kernel(h_ref, g_ref, w1_ref, w2_ref, gf_ref, out_ref, *, final_norm):
    h = h_ref[...]
    hn = _rms(h, g_ref[...]).astype(BF16)
    acc = h
    for c in range(w1_ref.shape[1] // FF_CHUNK):
        cols = slice(c * FF_CHUNK, (c + 1) * FF_CHUNK)
        a = jnp.maximum(_dot(hn, w1_ref[:, cols]), 0.0)
        acc = acc + _dot((a * a).astype(BF16), w2_ref[cols, :])
    out_ref[...] = _rms(acc, gf_ref[...]) if final_norm else acc


def _mlp(h, g, w1, w2, gf, *, final_norm):
    n, d = h.shape
    tm = TOKEN_TILE
    row = pl.BlockSpec((tm, d), lambda i: (i, 0))
    kern = functools.partial(_mlp_kernel, final_norm=final_norm)
    return pl.pallas_call(
        kern,
        grid=(n // tm,),
        in_specs=[row, _resident(g.shape), _resident(w1.shape), _resident(w2.shape), _resident(gf.shape)],
        out_specs=row,
        out_shape=jax.ShapeDtypeStruct((n, d), F32),
        compiler_params=pltpu.CompilerParams(dimension_semantics=("arbitrary",), vmem_limit_bytes=VMEM_LIMIT),
        name="mlp",
    )(h, g, w1, w2, gf)


def kernel(x, mem, norm_mix_g, w_in, w_pool_group, pool_scale, w_a_proj, lambda_q1, lambda_k1, lambda_q2, lambda_k2, subln_g, rel_bias, w_b_proj, w_gate, b_gate, w_out, norm_cross_g, norm_mem_g, w_cq, w_ckv, w_co, norm_mlp_g, w_ff1, w_ff2, final_norm_g):
    batch, seq, d = x.shape
    mem_len = mem.shape[1]
    depth = w_in.shape[0]
    pool_width = w_a_proj.shape[1]
    assert pool_width == d and seq % TOKEN_TILE == 0 and seq % ATTN_TILE == 0
    row = lambda a: a.reshape(1, -1).astype(F32)
    bf = lambda a: a.astype(BF16)

    near_bias = _rel_bias_tiles(rel_bias.astype(F32))
    h = x.reshape(batch * seq, d)
    mem2 = mem.reshape(batch * mem_len, d)
    for l in range(depth):
        lambda_init = LAMBDA_INIT_BASE - LAMBDA_INIT_AMP * math.exp(-LAMBDA_INIT_RATE * l)
        q, k, v, gaya, gb = _mixer_front(
            h, row(norm_mix_g[l]), bf(w_in[l, :, :pool_width]), bf(w_in[l, :, pool_width:]), bf(w_gate[l]),
            row(b_gate[l]), bf(w_pool_group[l]), row(pool_scale[l]), bf(w_a_proj[l]),
            seq=seq, q_scale=DIFF_HEAD_DIM ** -0.5 * LOG2E)
        lam_vecs = jnp.stack([lambda_q1[l], lambda_k1[l], lambda_q2[l], lambda_k2[l]]).astype(F32)
        o = _diff_attn(lam_vecs, q, k, v, near_bias, row(subln_g[l]), batch=batch, seq=seq, lambda_init=lambda_init)
        h = _mixer_merge(h, o, gaya, gb, bf(w_b_proj[l]), bf(w_out[l]))
        ckv = _mem_kv(mem2, row(norm_mem_g[l]), bf(w_ckv[l]), mem_len=mem_len)
        h = _cross_attn(h, row(norm_cross_g[l]), bf(w_cq[l]), ckv, bf(w_co[l]), seq=seq, mem_len=mem_len)
        h = _mlp(h, row(norm_mlp_g[l]), bf(w_ff1[l]), bf(w_ff2[l]), row(final_norm_g), final_norm=(l == depth - 1))
    return h.reshape(batch, seq, d)
```

```python
import functools
import math

import jax
import jax.numpy as jnp
from jax import lax
from jax.experimental import pallas as pl
from jax.experimental.pallas import tpu as pltpu

CHUNK = 64
POOL_WINDOWS = (2, 4, 8, 16)
DIFF_HEADS = 8
DIFF_HEAD_DIM = 64
LAMBDA_INIT_BASE = 0.8
LAMBDA_INIT_AMP = 0.6
LAMBDA_INIT_RATE = 0.3
REL_BUCKETS = 32
REL_MAX_DIST = 128
CROSS_HEADS = 4
EPS = 1e-6
NEG_INF = -1e30
LOG2E = math.log2(math.e)

POOL_HALO = 16
TOKEN_TILE = 512
ATTN_TILE = 256
ONES_ROWS = 16
ATTN_LEAD = 2
FF_CHUNK = 1024
VMEM_LIMIT = 56 * 1024 * 1024

BF16 = jnp.bfloat16
F32 = jnp.float32


def _resident(shape):
    n = len(shape)
    return pl.BlockSpec(shape, lambda *_: (0,) * n, pipeline_mode=pl.Buffered(1))


def _rms(x, g):
    return x * lax.rsqrt(jnp.mean(x * x, axis=-1, keepdims=True) + EPS) * g


def _dot(a, b):
    return jnp.dot(a, b, preferred_element_type=F32)


def _dot_nt(a, b):
    return lax.dot_general(a, b, (((1,), (1,)), ((), ())), preferred_element_type=F32)


def _mixer_front_kernel(x_ref, g_ref, wu_ref, wqkv_ref, wg_ref, bg_ref, wp_ref, ps_ref, wa_ref,
                        q_ref, k_ref, v_ref, gaya_ref, gb_ref, ubuf_ref, *, tiles_per_seq, q_scale):
    tm, d = x_ref.shape
    i = pl.program_id(0)
    tile_in_seq = i % tiles_per_seq
    xn = _rms(x_ref[...], g_ref[...]).astype(BF16)

    @pl.when(tile_in_seq == 0)
    def _():
        ubuf_ref[0:POOL_HALO, :] = jnp.zeros((POOL_HALO, d), F32)

    ubuf_ref[POOL_HALO:POOL_HALO + tm, :] = _dot(xn, wu_ref[...])

    width = q_ref.shape[1]
    q_ref[...] = (_dot(xn, wqkv_ref[:, 0:width]) * q_scale).astype(BF16)
    k_ref[...] = _dot(xn, wqkv_ref[:, width:2 * width]).astype(BF16)
    v_ref[...] = _dot(xn, wqkv_ref[:, 2 * width:3 * width]).astype(BF16)
    z = _dot(xn, wg_ref[...]) + bg_ref[...]
    gates = 1.0 / (1.0 + jnp.exp(-z))
    gaya_ref[...] = gates[:, :d]
    gb_ref[...] = gates[:, d:]

    pos = tile_in_seq * tm + lax.broadcasted_iota(jnp.int32, (tm, 1), 0)
    gd = d // len(POOL_WINDOWS)
    mapped = []
    for g, w in enumerate(POOL_WINDOWS):
        cols = slice(g * gd, (g + 1) * gd)
        u_g = ubuf_ref[POOL_HALO:POOL_HALO + tm, cols]
        acc = u_g
        for j in range(1, w):
            acc = acc + ubuf_ref[POOL_HALO - j:POOL_HALO - j + tm, cols]
        inv_count = 1.0 / jnp.minimum(pos + 1, w).astype(F32)
        pooled = acc * inv_count - u_g
        mapped.append(_dot(pooled.astype(BF16), wp_ref[g]))
    ubuf_ref[0:POOL_HALO, :] = ubuf_ref[tm:tm + POOL_HALO, :]
    y = jnp.concatenate(mapped, axis=-1) * ps_ref[...]
    gaya_ref[...] = gaya_ref[...] * _dot(y.astype(BF16), wa_ref[...])


def _mixer_front(x2, g, wu, wqkv, wg, bg, wp, ps, wa, *, seq, q_scale):
    n, d = x2.shape
    tm = TOKEN_TILE
    width = wqkv.shape[1] // 3
    row = lambda c: pl.BlockSpec((tm, c), lambda i: (i, 0))
    kern = functools.partial(_mixer_front_kernel, tiles_per_seq=seq // tm, q_scale=q_scale)
    return pl.pallas_call(
        kern,
        grid=(n // tm,),
        in_specs=[row(d), _resident(g.shape), _resident(wu.shape), _resident(wqkv.shape), _resident(wg.shape),
                  _resident(bg.shape), _resident(wp.shape), _resident(ps.shape), _resident(wa.shape)],
        out_specs=[row(width), row(width), row(width), row(d), row(d)],
        out_shape=[jax.ShapeDtypeStruct((n, width), BF16)] * 3 + [jax.ShapeDtypeStruct((n, d), F32)] * 2,
        scratch_shapes=[pltpu.VMEM((POOL_HALO + tm, d), F32)],
        compiler_params=pltpu.CompilerParams(dimension_semantics=("arbitrary",), vmem_limit_bytes=VMEM_LIMIT),
        name="mixer_front",
    )(x2, g, wu, wqkv, wg, bg, wp, ps, wa)


def _rel_bucket(rel):
    nb = REL_BUCKETS // 2
    ret = (rel > 0).astype(jnp.int32) * nb
    n = jnp.abs(rel)
    max_exact = nb // 2
    nf = jnp.maximum(n, 1).astype(F32)
    large = max_exact + (jnp.log(nf / max_exact) / math.log(REL_MAX_DIST / max_exact)
                         * (nb - max_exact)).astype(jnp.int32)
    large = jnp.minimum(large, nb - 1)
    return ret + jnp.where(n < max_exact, n, large)


def _rel_bias_kernel(table_ref, bucket_ref, out_ref, *, far_bucket):
    h = pl.program_id(0)
    bucket = bucket_ref[...]
    far = table_ref[far_bucket, h]
    acc = jnp.zeros(bucket.shape, F32)
    for b in range(REL_BUCKETS):
        acc = jnp.where(bucket == b, table_ref[b, h] - far, acc)
    out_ref[...] = acc * LOG2E


def _rel_bias_tiles(rel_bias):
    t = ATTN_TILE
    assert t > REL_MAX_DIST
    heads = rel_bias.shape[1]
    key = jnp.arange(t, dtype=jnp.int32)[:, None]
    query = jnp.arange(t, dtype=jnp.int32)[None, :]
    bucket = jnp.stack([_rel_bucket(key - t - query), _rel_bucket(key - query)])
    bucket = jnp.concatenate([bucket, bucket], axis=2)
    kern = functools.partial(_rel_bias_kernel, far_bucket=REL_BUCKETS // 2 - 1)
    return pl.pallas_call(
        kern,
        grid=(heads,),
        in_specs=[pl.BlockSpec(memory_space=pltpu.SMEM), pl.BlockSpec((2, t, 2 * t), lambda h: (0, 0, 0))],
        out_specs=pl.BlockSpec((None, 2, t, 2 * t), lambda h: (h, 0, 0, 0)),
        out_shape=jax.ShapeDtypeStruct((heads, 2, t, 2 * t), F32),
        name="rel_bias",
    )(rel_bias, bucket)


def _attn_schedule(nq):
    tiles = [(i, j) for i in reversed(range(nq)) for j in range(i + 1)]
    events, slot_of, free, n_slots = [], {}, [], 0
    pv_pos = 0
    for pos in range(2 * len(tiles)):
        if pos < len(tiles):
            i, j = tiles[pos]
            if i not in slot_of:
                if not free:
                    free.append(n_slots)
                    n_slots += 1
                slot_of[i] = free.pop(0)
            events.append(("score", i, j, slot_of[i]))
        if pv_pos < len(tiles):
            i, j = tiles[pv_pos]
            scored = min(pos + 1, len(tiles))
            if scored >= min(tiles.index((i, i)) + 1 + ATTN_LEAD, len(tiles)):
                events.append(("pv", i, j, slot_of[i]))
                pv_pos += 1
                if j == i:
                    free.append(slot_of[i])
    assert pv_pos == len(tiles)
    return events, n_slots


def _diff_attn_kernel(base_ref, lam_ref, q_ref, k_ref, v_ref, nb_ref, sg_ref, o_ref, s_ref, vt_ref, *, lambda_init):
    seq, hd = q_ref.shape
    t = ATTN_TILE
    nq = seq // t
    events, n_slots = _attn_schedule(nq)
    assert s_ref.shape[0] == n_slots
    base = base_ref[0]
    lane = lax.broadcasted_iota(jnp.int32, (t, hd), 1)
    key_chunk = lax.broadcasted_iota(jnp.int32, (t, 2 * t), 0) // CHUNK
    query_chunk = (lax.broadcasted_iota(jnp.int32, (t, 2 * t), 1) % t) // CHUNK
    allowed = key_chunk <= query_chunk
    lam = (jnp.exp(jnp.sum(lam_ref[0:1, :] * lam_ref[1:2, :], axis=-1, keepdims=True))
           - jnp.exp(jnp.sum(lam_ref[2:3, :] * lam_ref[3:4, :], axis=-1, keepdims=True)) + lambda_init)

    for j in range(nq):
        vt_ref[0:hd, j * t:(j + 1) * t] = v_ref[j * t:(j + 1) * t, :].astype(F32).T.astype(BF16)
    vt_ref[hd:, :] = jnp.ones((ONES_ROWS, seq), BF16)

    def stacked_q(i):
        q = q_ref[i * t:(i + 1) * t, :]
        zero = jnp.zeros_like(q)
        return jnp.concatenate([jnp.where(lane < hd // 2, q, zero), jnp.where(lane >= hd // 2, q, zero)], axis=0)

    def score_tile(i, j, slot, qq):
        s = _dot_nt(k_ref[j * t:(j + 1) * t, :], qq)
        if j == i:
            s = jnp.where(allowed, s + nb_ref[1], NEG_INF)
        elif j == i - 1:
            s = s + nb_ref[0]
        s_ref[base + slot, j] = s
        return jnp.max(s, axis=0, keepdims=True)

    def finish(i, acc):
        o_maps = acc[:hd] / acc[hd:hd + 1]
        o = (o_maps[:, :t] - lam * o_maps[:, t:]).T
        o_ref[i * t:(i + 1) * t, :] = (_rms(o, sg_ref[...]) * (1.0 - lambda_init)).astype(o_ref.dtype)

    qq, mx, acc = {}, {}, {}
    for kind, i, j, slot in events:
        if kind == "score":
            if j == 0:
                qq[i] = stacked_q(i)
            tile_max = score_tile(i, j, slot, qq[i])
            mx[i] = tile_max if j == 0 else jnp.maximum(mx[i], tile_max)
        else:
            p = jnp.exp2((s_ref[base + slot, j] - mx[i]).astype(BF16))
            pv = _dot(vt_ref[:, j * t:(j + 1) * t], p)
            acc[i] = pv if j == 0 else acc[i] + pv
            if j == i:
                finish(i, acc.pop(i))


def _diff_attn(lam_vecs, q, k, v, near_bias, subln_g, *, batch, seq, lambda_init):
    n, width = q.shape
    t = ATTN_TILE
    hd = width // DIFF_HEADS
    nq = seq // t
    q3, k3, v3 = (a.reshape(batch, seq, width) for a in (q, k, v))
    head_block = pl.BlockSpec((None, seq, hd), lambda b, h: (b, 0, h))
    kern = functools.partial(_diff_attn_kernel, lambda_init=lambda_init)
    out = pl.pallas_call(
        kern,
        grid=(batch, DIFF_HEADS),
        in_specs=[
            pl.BlockSpec(memory_space=pltpu.SMEM),
            pl.BlockSpec(lam_vecs.shape, lambda b, h: (0, 0)),
            head_block, head_block, head_block,
            pl.BlockSpec((None, 2, t, 2 * t), lambda b, h: (h, 0, 0, 0)),
            pl.BlockSpec(subln_g.shape, lambda b, h: (0, 0)),
        ],
        out_specs=head_block,
        out_shape=jax.ShapeDtypeStruct((batch, seq, width), BF16),
        scratch_shapes=[pltpu.VMEM((_attn_schedule(nq)[1], nq, t, 2 * t), F32), pltpu.VMEM((hd + ONES_ROWS, seq), BF16)],
        compiler_params=pltpu.CompilerParams(dimension_semantics=("arbitrary",) * 2, vmem_limit_bytes=VMEM_LIMIT),
        name="diff_attn",
    )(jnp.zeros((1,), jnp.int32), lam_vecs, q3, k3, v3, near_bias, subln_g)
    return out.reshape(n, width)


def _mixer_merge_kernel(x_ref, o_ref, gaya_ref, gb_ref, wb_ref, wo_ref, h_ref):
    y_b = _dot(o_ref[...], wb_ref[...])
    merged = gaya_ref[...] + gb_ref[...] * y_b
    h_ref[...] = x_ref[...] + _dot(merged.astype(BF16), wo_ref[...])


def _mixer_merge(x2, o, gaya, gb, wb, wo):
    n, d = x2.shape
    tm = TOKEN_TILE
    row = lambda c: pl.BlockSpec((tm, c), lambda i: (i, 0))
    return pl.pallas_call(
        _mixer_merge_kernel,
        grid=(n // tm,),
        in_specs=[row(d), row(o.shape[1]), row(d), row(d), _resident(wb.shape), _resident(wo.shape)],
        out_specs=row(d),
        out_shape=jax.ShapeDtypeStruct((n, d), F32),
        compiler_params=pltpu.CompilerParams(dimension_semantics=("arbitrary",), vmem_limit_bytes=VMEM_LIMIT),
        name="mixer_merge",
    )(x2, o, gaya, gb, wb, wo)


def _mem_kv_kernel(mem_ref, g_ref, w_ref, kv_ref):
    mn = _rms(mem_ref[...], g_ref[...]).astype(BF16)
    kv_ref[...] = _dot(mn, w_ref[...]).astype(BF16)


def _mem_kv(mem2, g, w, *, mem_len):
    n, d = mem2.shape
    return pl.pallas_call(
        _mem_kv_kernel,
        grid=(n // mem_len,),
        in_specs=[pl.BlockSpec((mem_len, d), lambda i: (i, 0)), _resident(g.shape), _resident(w.shape)],
        out_specs=pl.BlockSpec((mem_len, w.shape[1]), lambda i: (i, 0)),
        out_shape=jax.ShapeDtypeStruct((n, w.shape[1]), BF16),
        compiler_params=pltpu.CompilerParams(dimension_semantics=("arbitrary",), vmem_limit_bytes=VMEM_LIMIT),
        name="mem_kv",
    )(mem2, g, w)


def _cross_attn_kernel(h_ref, g_ref, wq_ref, ck_ref, cv_ref, wo_ref, out_ref, *, q_scale):
    tm, d = h_ref.shape
    h = h_ref[...]
    hn = _rms(h, g_ref[...]).astype(BF16)
    cq = (_dot(hn, wq_ref[...]) * q_scale).astype(BF16)
    hd = d // CROSS_HEADS
    cols = [slice(a * hd, (a + 1) * hd) for a in range(CROSS_HEADS)]
    scores = [_dot_nt(cq[:, c], ck_ref[:, c]) for c in cols]
    probs = [jnp.exp2(s - jnp.max(s, axis=-1, keepdims=True)) for s in scores]
    heads = [_dot(p.astype(BF16), cv_ref[:, c]) / jnp.sum(p, axis=-1, keepdims=True) for p, c in zip(probs, cols)]
    o = jnp.concatenate(heads, axis=-1).astype(BF16)
    out_ref[...] = h + _dot(o, wo_ref[...])


def _cross_attn(h, g, wq, ckv, wo, *, seq, mem_len):
    n, d = h.shape
    tm = TOKEN_TILE
    tiles_per_seq = seq // tm
    row = pl.BlockSpec((tm, d), lambda i: (i, 0))
    kern = functools.partial(_cross_attn_kernel, q_scale=(d // CROSS_HEADS) ** -0.5 * LOG2E)
    return pl.pallas_call(
        kern,
        grid=(n // tm,),
        in_specs=[row, _resident(g.shape), _resident(wq.shape),
                  pl.BlockSpec((mem_len, d), lambda i: (i // tiles_per_seq, 0)),
                  pl.BlockSpec((mem_len, d), lambda i: (i // tiles_per_seq, 1)),
                  _resident(wo.shape)],
        out_specs=row,
        out_shape=jax.ShapeDtypeStruct((n, d), F32),
        compiler_params=pltpu.CompilerParams(dimension_semantics=("arbitrary",), vmem_limit_bytes=VMEM_LIMIT),
        name="cross_attn",
    )(h, g, wq, ckv, ckv, wo)


def _mlp_kernel(h_ref, g_ref, w1_ref, w2_ref, gf_ref, out_ref, *, final_norm):
    h = h_ref[...]
    hn = _rms(h, g_ref[...]).astype(BF16)
    acc = h
    for c in range(w1_ref.shape[1] // FF_CHUNK):
        cols = slice(c * FF_CHUNK, (c + 1) * FF_CHUNK)
        a = jnp.maximum(_dot(hn, w1_ref[:, cols]), 0.0)
        acc = acc + _dot((a * a).astype(BF16), w2_ref[cols, :])
    out_ref[...] = _rms(acc, gf_ref[...]) if final_norm else acc


def _mlp(h, g, w1, w2, gf, *, final_norm):
    n, d = h.shape
    tm = TOKEN_TILE
    row = pl.BlockSpec((tm, d), lambda i: (i, 0))
    kern = functools.partial(_mlp_kernel, final_norm=final_norm)
    return pl.pallas_call(
        kern,
        grid=(n // tm,),
        in_specs=[row, _resident(g.shape), _resident(w1.shape), _resident(w2.shape), _resident(gf.shape)],
        out_specs=row,
        out_shape=jax.ShapeDtypeStruct((n, d), F32),
        compiler_params=pltpu.CompilerParams(dimension_semantics=("arbitrary",), vmem_limit_bytes=VMEM_LIMIT),
        name="mlp",
    )(h, g, w1, w2, gf)


def kernel(x, mem, norm_mix_g, w_in, w_pool_group, pool_scale, w_a_proj, lambda_q1, lambda_k1, lambda_q2, lambda_k2, subln_g, rel_bias, w_b_proj, w_gate, b_gate, w_out, norm_cross_g, norm_mem_g, w_cq, w_ckv, w_co, norm_mlp_g, w_ff1, w_ff2, final_norm_g):
    batch, seq, d = x.shape
    mem_len = mem.shape[1]
    depth = w_in.shape[0]
    pool_width = w_a_proj.shape[1]
    assert pool_width == d and seq % TOKEN_TILE == 0 and seq % ATTN_TILE == 0
    row = lambda a: a.reshape(1, -1).astype(F32)
    bf = lambda a: a.astype(BF16)

    near_bias = _rel_bias_tiles(rel_bias.astype(F32))
    h = x.reshape(batch * seq, d)
    mem2 = mem.reshape(batch * mem_len, d)
    for l in range(depth):
        lambda_init = LAMBDA_INIT_BASE - LAMBDA_INIT_AMP * math.exp(-LAMBDA_INIT_RATE * l)
        q, k, v, gaya, gb = _mixer_front(
            h, row(norm_mix_g[l]), bf(w_in[l, :, :pool_width]), bf(w_in[l, :, pool_width:]), bf(w_gate[l]),
            row(b_gate[l]), bf(w_pool_group[l]), row(pool_scale[l]), bf(w_a_proj[l]),
            seq=seq, q_scale=DIFF_HEAD_DIM ** -0.5 * LOG2E)
        lam_vecs = jnp.stack([lambda_q1[l], lambda_k1[l], lambda_q2[l], lambda_k2[l]]).astype(F32)
        o = _diff_attn(lam_vecs, q, k, v, near_bias, row(subln_g[l]), batch=batch, seq=seq, lambda_init=lambda_init)
        h = _mixer_merge(h, o, gaya, gb, bf(w_b_proj[l]), bf(w_out[l]))
        ckv = _mem_kv(mem2, row(norm_mem_g[l]), bf(w_ckv[l]), mem_len=mem_len)
        h = _cross_attn(h, row(norm_cross_g[l]), bf(w_cq[l]), ckv, bf(w_co[l]), seq=seq, mem_len=mem_len)
        h = _mlp(h, row(norm_mlp_g[l]), bf(w_ff1[l]), bf(w_ff2[l]), row(final_norm_g), final_norm=(l == depth - 1))
    return h.reshape(batch, seq, d)
```

```python
import functools
import math

import jax
import jax.numpy as jnp
from jax import lax
from jax.experimental import pallas as pl
from jax.experimental.pallas import tpu as pltpu

CHUNK = 64
POOL_WINDOWS = (2, 4, 8, 16)
DIFF_HEADS = 8
DIFF_HEAD_DIM = 64
LAMBDA_INIT_BASE = 0.8
LAMBDA_INIT_AMP = 0.6
LAMBDA_INIT_RATE = 0.3
REL_BUCKETS = 32
REL_MAX_DIST = 128
CROSS_HEADS = 4
EPS = 1e-6
NEG_INF = -1e30
LOG2E = math.log2(math.e)

POOL_HALO = 16
TOKEN_TILE = 512
MLP_TILE = 1024
ATTN_TILE = 256
ONES_ROWS = 16
ATTN_LEAD = 2
FF_CHUNK = 1024
VMEM_LIMIT = 56 * 1024 * 1024

BF16 = jnp.bfloat16
F32 = jnp.float32


def _resident(shape):
    n = len(shape)
    return pl.BlockSpec(shape, lambda *_: (0,) * n, pipeline_mode=pl.Buffered(1))


def _rms(x, g):
    return x * lax.rsqrt(jnp.mean(x * x, axis=-1, keepdims=True) + EPS) * g


def _dot(a, b):
    return jnp.dot(a, b, preferred_element_type=F32)


def _dot_nt(a, b):
    return lax.dot_general(a, b, (((1,), (1,)), ((), ())), preferred_element_type=F32)


def _mixer_front_kernel(x_ref, g_ref, wu_ref, wqkv_ref, wg_ref, bg_ref, wp_ref, ps_ref, wa_ref,
                        q_ref, k_ref, v_ref, gaya_ref, gb_ref, ubuf_ref, ga_ref, *, tiles_per_seq, q_scale):
    tm, d = x_ref.shape
    i = pl.program_id(0)
    tile_in_seq = i % tiles_per_seq
    xn = _rms(x_ref[...], g_ref[...]).astype(BF16)

    @pl.when(tile_in_seq == 0)
    def _():
        ubuf_ref[0:POOL_HALO, :] = jnp.zeros((POOL_HALO, d), F32)

    ubuf_ref[POOL_HALO:POOL_HALO + tm, :] = _dot(xn, wu_ref[...])

    width = q_ref.shape[1]
    q_ref[...] = (_dot(xn, wqkv_ref[:, 0:width]) * q_scale).astype(BF16)
    k_ref[...] = _dot(xn, wqkv_ref[:, width:2 * width]).astype(BF16)
    v_ref[...] = _dot(xn, wqkv_ref[:, 2 * width:3 * width]).astype(BF16)
    z = _dot(xn, wg_ref[...]) + bg_ref[...]
    gates = 1.0 / (1.0 + jnp.exp(-z))
    ga_ref[...] = gates[:, :d]
    gb_ref[...] = gates[:, d:].astype(gb_ref.dtype)

    pos = tile_in_seq * tm + lax.broadcasted_iota(jnp.int32, (tm, 1), 0)
    gd = d // len(POOL_WINDOWS)
    mapped = []
    for g, w in enumerate(POOL_WINDOWS):
        cols = slice(g * gd, (g + 1) * gd)
        u_g = ubuf_ref[POOL_HALO:POOL_HALO + tm, cols]
        acc = u_g
        for j in range(1, w):
            acc = acc + ubuf_ref[POOL_HALO - j:POOL_HALO - j + tm, cols]
        inv_count = 1.0 / jnp.minimum(pos + 1, w).astype(F32)
        pooled = acc * inv_count - u_g
        mapped.append(_dot(pooled.astype(BF16), wp_ref[g]))
    ubuf_ref[0:POOL_HALO, :] = ubuf_ref[tm:tm + POOL_HALO, :]
    y = jnp.concatenate(mapped, axis=-1) * ps_ref[...]
    gaya_ref[...] = (ga_ref[...] * _dot(y.astype(BF16), wa_ref[...])).astype(gaya_ref.dtype)


def _mixer_front(x2, g, wu, wqkv, wg, bg, wp, ps, wa, *, seq, q_scale):
    n, d = x2.shape
    tm = TOKEN_TILE
    width = wqkv.shape[1] // 3
    row = lambda c: pl.BlockSpec((tm, c), lambda i: (i, 0))
    kern = functools.partial(_mixer_front_kernel, tiles_per_seq=seq // tm, q_scale=q_scale)
    return pl.pallas_call(
        kern,
        grid=(n // tm,),
        in_specs=[row(d), _resident(g.shape), _resident(wu.shape), _resident(wqkv.shape), _resident(wg.shape),
                  _resident(bg.shape), _resident(wp.shape), _resident(ps.shape), _resident(wa.shape)],
        out_specs=[row(width), row(width), row(width), row(d), row(d)],
        out_shape=[jax.ShapeDtypeStruct((n, width), BF16)] * 3 + [jax.ShapeDtypeStruct((n, d), BF16)] * 2,
        scratch_shapes=[pltpu.VMEM((POOL_HALO + tm, d), F32), pltpu.VMEM((tm, d), F32)],
        compiler_params=pltpu.CompilerParams(dimension_semantics=("arbitrary",), vmem_limit_bytes=VMEM_LIMIT),
        name="mixer_front",
    )(x2, g, wu, wqkv, wg, bg, wp, ps, wa)


def _rel_bucket(rel):
    nb = REL_BUCKETS // 2
    ret = (rel > 0).astype(jnp.int32) * nb
    n = jnp.abs(rel)
    max_exact = nb // 2
    nf = jnp.maximum(n, 1).astype(F32)
    large = max_exact + (jnp.log(nf / max_exact) / math.log(REL_MAX_DIST / max_exact)
                         * (nb - max_exact)).astype(jnp.int32)
    large = jnp.minimum(large, nb - 1)
    return ret + jnp.where(n < max_exact, n, large)


def _rel_bias_kernel(table_ref, bucket_ref, out_ref, *, far_bucket):
    h = pl.program_id(0)
    bucket = bucket_ref[...]
    far = table_ref[far_bucket, h]
    acc = jnp.zeros(bucket.shape, F32)
    for b in range(REL_BUCKETS):
        acc = jnp.where(bucket == b, table_ref[b, h] - far, acc)
    t = bucket.shape[-1]
    out_ref[:, :, 0:t] = acc * LOG2E
    out_ref[:, :, t:2 * t] = acc * LOG2E


def _rel_bias_tiles(rel_bias):
    t = ATTN_TILE
    assert t > REL_MAX_DIST
    heads = rel_bias.shape[1]
    key = jnp.arange(t, dtype=jnp.int32)[:, None]
    query = jnp.arange(t, dtype=jnp.int32)[None, :]
    bucket = jnp.stack([_rel_bucket(key - t - query), _rel_bucket(key - query)])
    kern = functools.partial(_rel_bias_kernel, far_bucket=REL_BUCKETS // 2 - 1)
    return pl.pallas_call(
        kern,
        grid=(heads,),
        in_specs=[pl.BlockSpec(memory_space=pltpu.SMEM), pl.BlockSpec((2, t, t), lambda h: (0, 0, 0))],
        out_specs=pl.BlockSpec((None, 2, t, 2 * t), lambda h: (h, 0, 0, 0)),
        out_shape=jax.ShapeDtypeStruct((heads, 2, t, 2 * t), F32),
        name="rel_bias",
    )(rel_bias, bucket)


def _attn_schedule(nq):
    tiles = [(i, j) for i in reversed(range(nq)) for j in range(i + 1)]
    events, slot_of, free, n_slots = [], {}, [], 0
    pv_pos = 0
    for pos in range(2 * len(tiles)):
        if pos < len(tiles):
            i, j = tiles[pos]
            if i not in slot_of:
                if not free:
                    free.append(n_slots)
                    n_slots += 1
                slot_of[i] = free.pop(0)
            events.append(("score", i, j, slot_of[i]))
        if pv_pos < len(tiles):
            i, j = tiles[pv_pos]
            scored = min(pos + 1, len(tiles))
            if scored >= min(tiles.index((i, i)) + 1 + ATTN_LEAD, len(tiles)):
                events.append(("pv", i, j, slot_of[i]))
                pv_pos += 1
                if j == i:
                    free.append(slot_of[i])
    assert pv_pos == len(tiles)
    return events, n_slots


def _diff_attn_kernel(base_ref, lam_ref, q_ref, k_ref, v_ref, nb_ref, sg_ref, o_ref, s_ref, vt_ref, *, lambda_init):
    seq, hd = q_ref.shape
    t = ATTN_TILE
    nq = seq // t
    events, n_slots = _attn_schedule(nq)
    assert s_ref.shape[0] == n_slots
    base = base_ref[0]
    lane = lax.broadcasted_iota(jnp.int32, (t, hd), 1)
    key_chunk = lax.broadcasted_iota(jnp.int32, (t, 2 * t), 0) // CHUNK
    query_chunk = (lax.broadcasted_iota(jnp.int32, (t, 2 * t), 1) % t) // CHUNK
    allowed = key_chunk <= query_chunk
    lam = (jnp.exp(jnp.sum(lam_ref[0:1, :] * lam_ref[1:2, :], axis=-1, keepdims=True))
           - jnp.exp(jnp.sum(lam_ref[2:3, :] * lam_ref[3:4, :], axis=-1, keepdims=True)) + lambda_init)

    for j in range(nq):
        vt_ref[0:hd, j * t:(j + 1) * t] = v_ref[j * t:(j + 1) * t, :].astype(F32).T.astype(BF16)
    vt_ref[hd:, :] = jnp.ones((ONES_ROWS, seq), BF16)

    def stacked_q(i):
        q = q_ref[i * t:(i + 1) * t, :]
        zero = jnp.zeros_like(q)
        return jnp.concatenate([jnp.where(lane < hd // 2, q, zero), jnp.where(lane >= hd // 2, q, zero)], axis=0)

    def score_tile(i, j, slot, qq):
        s = _dot_nt(k_ref[j * t:(j + 1) * t, :], qq)
        if j == i:
            s = jnp.where(allowed, s + nb_ref[1], NEG_INF)
        elif j == i - 1:
            s = s + nb_ref[0]
        s_ref[base + slot, j] = s
        return jnp.max(s, axis=0, keepdims=True)

    def finish(i, acc):
        o_maps = acc[:hd] / acc[hd:hd + 1]
        o = (o_maps[:, :t] - lam * o_maps[:, t:]).T
        o_ref[i * t:(i + 1) * t, :] = (_rms(o, sg_ref[...]) * (1.0 - lambda_init)).astype(o_ref.dtype)

    qq, mx, acc = {}, {}, {}
    for kind, i, j, slot in events:
        if kind == "score":
            if j == 0:
                qq[i] = stacked_q(i)
            tile_max = score_tile(i, j, slot, qq[i])
            mx[i] = tile_max if j == 0 else jnp.maximum(mx[i], tile_max)
        else:
            p = jnp.exp2((s_ref[base + slot, j] - mx[i]).astype(BF16))
            pv = _dot(vt_ref[:, j * t:(j + 1) * t], p)
            acc[i] = pv if j == 0 else acc[i] + pv
            if j == i:
                finish(i, acc.pop(i))


def _diff_attn(lam_vecs, q, k, v, near_bias, subln_g, *, batch, seq, lambda_init):
    n, width = q.shape
    t = ATTN_TILE
    hd = width // DIFF_HEADS
    nq = seq // t
    q3, k3, v3 = (a.reshape(batch, seq, width) for a in (q, k, v))
    head_block = pl.BlockSpec((None, seq, hd), lambda b, h: (b, 0, h))
    kern = functools.partial(_diff_attn_kernel, lambda_init=lambda_init)
    out = pl.pallas_call(
        kern,
        grid=(batch, DIFF_HEADS),
        in_specs=[
            pl.BlockSpec(memory_space=pltpu.SMEM),
            pl.BlockSpec(lam_vecs.shape, lambda b, h: (0, 0)),
            head_block, head_block, head_block,
            pl.BlockSpec((None, 2, t, 2 * t), lambda b, h: (h, 0, 0, 0)),
            pl.BlockSpec(subln_g.shape, lambda b, h: (0, 0)),
        ],
        out_specs=head_block,
        out_shape=jax.ShapeDtypeStruct((batch, seq, width), BF16),
        scratch_shapes=[pltpu.VMEM((_attn_schedule(nq)[1], nq, t, 2 * t), F32), pltpu.VMEM((hd + ONES_ROWS, seq), BF16)],
        compiler_params=pltpu.CompilerParams(dimension_semantics=("arbitrary",) * 2, vmem_limit_bytes=VMEM_LIMIT),
        name="diff_attn",
    )(jnp.zeros((1,), jnp.int32), lam_vecs, q3, k3, v3, near_bias, subln_g)
    return out.reshape(n, width)


def _mem_kv_kernel(mem_ref, g_ref, w_ref, kv_ref):
    mn = _rms(mem_ref[...], g_ref[...]).astype(BF16)
    kv_ref[...] = _dot(mn, w_ref[...]).astype(BF16)


def _mem_kv(mem2, g, w, *, mem_len):
    n, d = mem2.shape
    return pl.pallas_call(
        _mem_kv_kernel,
        grid=(n // mem_len,),
        in_specs=[pl.BlockSpec((mem_len, d), lambda i: (i, 0)), _resident(g.shape), _resident(w.shape)],
        out_specs=pl.BlockSpec((mem_len, w.shape[1]), lambda i: (i, 0)),
        out_shape=jax.ShapeDtypeStruct((n, w.shape[1]), BF16),
        compiler_params=pltpu.CompilerParams(dimension_semantics=("arbitrary",), vmem_limit_bytes=VMEM_LIMIT),
        name="mem_kv",
    )(mem2, g, w)


def _merge_cross_kernel(x_ref, o_ref, gaya_ref, gb_ref, wb_ref, wmo_ref, g_ref, wq_ref, ck_ref, cv_ref, wo_ref,
                        out_ref, *, q_scale):
    tm, d = x_ref.shape
    y_b = _dot(o_ref[...], wb_ref[...])
    merged = gaya_ref[...].astype(F32) + gb_ref[...].astype(F32) * y_b
    h = x_ref[...] + _dot(merged.astype(BF16), wmo_ref[...])

    hn = _rms(h, g_ref[...]).astype(BF16)
    cq = (_dot(hn, wq_ref[...]) * q_scale).astype(BF16)
    hd = d // CROSS_HEADS
    cols = [slice(a * hd, (a + 1) * hd) for a in range(CROSS_HEADS)]
    scores = [_dot_nt(cq[:, c], ck_ref[:, c]) for c in cols]
    probs = [jnp.exp2(s - jnp.max(s, axis=-1, keepdims=True)) for s in scores]
    heads = [_dot(p.astype(BF16), cv_ref[:, c]) / jnp.sum(p, axis=-1, keepdims=True) for p, c in zip(probs, cols)]
    o = jnp.concatenate(heads, axis=-1).astype(BF16)
    out_ref[...] = h + _dot(o, wo_ref[...])


def _merge_cross(x2, o, gaya, gb, wb, wmo, g, wq, ckv, wo, *, seq, mem_len):
    n, d = x2.shape
    tm = TOKEN_TILE
    tiles_per_seq = seq // tm
    row = lambda c: pl.BlockSpec((tm, c), lambda i: (i, 0))
    kern = functools.partial(_merge_cross_kernel, q_scale=(d // CROSS_HEADS) ** -0.5 * LOG2E)
    return pl.pallas_call(
        kern,
        grid=(n // tm,),
        in_specs=[row(d), row(o.shape[1]), row(d), row(d), _resident(wb.shape), _resident(wmo.shape),
                  _resident(g.shape), _resident(wq.shape),
                  pl.BlockSpec((mem_len, d), lambda i: (i // tiles_per_seq, 0)),
                  pl.BlockSpec((mem_len, d), lambda i: (i // tiles_per_seq, 1)),
                  _resident(wo.shape)],
        out_specs=row(d),
        out_shape=jax.ShapeDtypeStruct((n, d), F32),
        compiler_params=pltpu.CompilerParams(dimension_semantics=("arbitrary",), vmem_limit_bytes=VMEM_LIMIT),
        name="merge_cross",
    )(x2, o, gaya, gb, wb, wmo, g, wq, ckv, ckv, wo)


def _mlp_kernel(h_ref, g_ref, w1_ref, w2_ref, gf_ref, out_ref, *, final_norm):
    h = h_ref[...]
    hn = _rms(h, g_ref[...]).astype(BF16)
    acc = h
    for c in range(w1_ref.shape[1] // FF_CHUNK):
        cols = slice(c * FF_CHUNK, (c + 1) * FF_CHUNK)
        a = jnp.maximum(_dot(hn, w1_ref[:, cols]), 0.0)
        acc = acc + _dot((a * a).astype(BF16), w2_ref[cols, :])
    out_ref[...] = _rms(acc, gf_ref[...]) if final_norm else acc


def _mlp(h, g, w1, w2, gf, *, final_norm):
    n, d = h.shape
    tm = MLP_TILE
    row = pl.BlockSpec((tm, d), lambda i: (i, 0))
    kern = functools.partial(_mlp_kernel, final_norm=final_norm)
    return pl.pallas_call(
        kern,
        grid=(n // tm,),
        in_specs=[row, _resident(g.shape), _resident(w1.shape), _resident(w2.shape), _resident(gf.shape)],
        out_specs=row,
        out_shape=jax.ShapeDtypeStruct((n, d), F32),
        compiler_params=pltpu.CompilerParams(dimension_semantics=("arbitrary",), vmem_limit_bytes=VMEM_LIMIT),
        name="mlp",
    )(h, g, w1, w2, gf)


def kernel(x, mem, norm_mix_g, w_in, w_pool_group, pool_scale, w_a_proj, lambda_q1, lambda_k1, lambda_q2, lambda_k2, subln_g, rel_bias, w_b_proj, w_gate, b_gate, w_out, norm_cross_g, norm_mem_g, w_cq, w_ckv, w_co, norm_mlp_g, w_ff1, w_ff2, final_norm_g):
    batch, seq, d = x.shape
    mem_len = mem.shape[1]
    depth = w_in.shape[0]
    pool_width = w_a_proj.shape[1]
    assert pool_width == d and seq % TOKEN_TILE == 0 and seq % ATTN_TILE == 0
    row = lambda a: a.reshape(1, -1).astype(F32)
    bf = lambda a: a.astype(BF16)

    near_bias = _rel_bias_tiles(rel_bias.astype(F32))
    h = x.reshape(batch * seq, d)
    mem2 = mem.reshape(batch * mem_len, d)
    for l in range(depth):
        lambda_init = LAMBDA_INIT_BASE - LAMBDA_INIT_AMP * math.exp(-LAMBDA_INIT_RATE * l)
        q, k, v, gaya, gb = _mixer_front(
            h, row(norm_mix_g[l]), bf(w_in[l, :, :pool_width]), bf(w_in[l, :, pool_width:]), bf(w_gate[l]),
            row(b_gate[l]), bf(w_pool_group[l]), row(pool_scale[l]), bf(w_a_proj[l]),
            seq=seq, q_scale=DIFF_HEAD_DIM ** -0.5 * LOG2E)
        lam_vecs = jnp.stack([lambda_q1[l], lambda_k1[l], lambda_q2[l], lambda_k2[l]]).astype(F32)
        o = _diff_attn(lam_vecs, q, k, v, near_bias, row(subln_g[l]), batch=batch, seq=seq, lambda_init=lambda_init)
        ckv = _mem_kv(mem2, row(norm_mem_g[l]), bf(w_ckv[l]), mem_len=mem_len)
        h = _merge_cross(h, o, gaya, gb, bf(w_b_proj[l]), bf(w_out[l]), row(norm_cross_g[l]), bf(w_cq[l]), ckv,
                         bf(w_co[l]), seq=seq, mem_len=mem_len)
        h = _mlp(h, row(norm_mlp_g[l]), bf(w_ff1[l]), bf(w_ff2[l]), row(final_norm_g), final_norm=(l == depth - 1))
    return h.reshape(batch, seq, d)
```

```python
import functools
import math

import jax
import jax.numpy as jnp
from jax import lax
from jax.experimental import pallas as pl
from jax.experimental.pallas import tpu as pltpu

CHUNK = 64
POOL_WINDOWS = (2, 4, 8, 16)
DIFF_HEADS = 8
DIFF_HEAD_DIM = 64
LAMBDA_INIT_BASE = 0.8
LAMBDA_INIT_AMP = 0.6
LAMBDA_INIT_RATE = 0.3
REL_BUCKETS = 32
REL_MAX_DIST = 128
CROSS_HEADS = 4
EPS = 1e-6
NEG_INF = -1e30
LOG2E = math.log2(math.e)

POOL_HALO = 16
TOKEN_TILE = 512
MLP_TILE = 1024
ATTN_TILE = 256
ATTN_HEADS = 2
ONES_ROWS = 16
ATTN_LEAD = 2
FF_CHUNK = 1024
VMEM_LIMIT = 56 * 1024 * 1024

BF16 = jnp.bfloat16
F32 = jnp.float32


def _resident(shape):
    n = len(shape)
    return pl.BlockSpec(shape, lambda *_: (0,) * n, pipeline_mode=pl.Buffered(1))


def _rms(x, g):
    return x * lax.rsqrt(jnp.mean(x * x, axis=-1, keepdims=True) + EPS) * g


def _dot(a, b):
    return jnp.dot(a, b, preferred_element_type=F32)


def _dot_nt(a, b):
    return lax.dot_general(a, b, (((1,), (1,)), ((), ())), preferred_element_type=F32)


def _mixer_front_kernel(x_ref, g_ref, wu_ref, wqkv_ref, wg_ref, bg_ref, wp_ref, ps_ref, wa_ref,
                        q_ref, k_ref, v_ref, gaya_ref, gb_ref, ubuf_ref, ga_ref, *, tiles_per_seq, q_scale):
    tm, d = x_ref.shape
    i = pl.program_id(0)
    tile_in_seq = i % tiles_per_seq
    xn = _rms(x_ref[...], g_ref[...]).astype(BF16)

    @pl.when(tile_in_seq == 0)
    def _():
        ubuf_ref[0:POOL_HALO, :] = jnp.zeros((POOL_HALO, d), F32)

    ubuf_ref[POOL_HALO:POOL_HALO + tm, :] = _dot(xn, wu_ref[...])

    width = q_ref.shape[1]
    q_ref[...] = (_dot(xn, wqkv_ref[:, 0:width]) * q_scale).astype(BF16)
    k_ref[...] = _dot(xn, wqkv_ref[:, width:2 * width]).astype(BF16)
    v_ref[...] = _dot(xn, wqkv_ref[:, 2 * width:3 * width]).astype(BF16)
    z = _dot(xn, wg_ref[...]) + bg_ref[...]
    gates = 1.0 / (1.0 + jnp.exp(-z))
    ga_ref[...] = gates[:, :d]
    gb_ref[...] = gates[:, d:].astype(gb_ref.dtype)

    pos = tile_in_seq * tm + lax.broadcasted_iota(jnp.int32, (tm, 1), 0)
    gd = d // len(POOL_WINDOWS)
    mapped = []
    for g, w in enumerate(POOL_WINDOWS):
        cols = slice(g * gd, (g + 1) * gd)
        u_g = ubuf_ref[POOL_HALO:POOL_HALO + tm, cols]
        acc = u_g
        for j in range(1, w):
            acc = acc + ubuf_ref[POOL_HALO - j:POOL_HALO - j + tm, cols]
        inv_count = 1.0 / jnp.minimum(pos + 1, w).astype(F32)
        pooled = acc * inv_count - u_g
        mapped.append(_dot(pooled.astype(BF16), wp_ref[g]))
    ubuf_ref[0:POOL_HALO, :] = ubuf_ref[tm:tm + POOL_HALO, :]
    y = jnp.concatenate(mapped, axis=-1) * ps_ref[...]
    gaya_ref[...] = (ga_ref[...] * _dot(y.astype(BF16), wa_ref[...])).astype(gaya_ref.dtype)


def _mixer_front(x2, g, wu, wqkv, wg, bg, wp, ps, wa, *, seq, q_scale):
    n, d = x2.shape
    tm = TOKEN_TILE
    width = wqkv.shape[1] // 3
    row = lambda c: pl.BlockSpec((tm, c), lambda i: (i, 0))
    kern = functools.partial(_mixer_front_kernel, tiles_per_seq=seq // tm, q_scale=q_scale)
    return pl.pallas_call(
        kern,
        grid=(n // tm,),
        in_specs=[row(d), _resident(g.shape), _resident(wu.shape), _resident(wqkv.shape), _resident(wg.shape),
                  _resident(bg.shape), _resident(wp.shape), _resident(ps.shape), _resident(wa.shape)],
        out_specs=[row(width), row(width), row(width), row(d), row(d)],
        out_shape=[jax.ShapeDtypeStruct((n, width), BF16)] * 3 + [jax.ShapeDtypeStruct((n, d), BF16)] * 2,
        scratch_shapes=[pltpu.VMEM((POOL_HALO + tm, d), F32), pltpu.VMEM((tm, d), F32)],
        compiler_params=pltpu.CompilerParams(dimension_semantics=("arbitrary",), vmem_limit_bytes=VMEM_LIMIT),
        name="mixer_front",
    )(x2, g, wu, wqkv, wg, bg, wp, ps, wa)


def _rel_bucket(rel):
    nb = REL_BUCKETS // 2
    ret = (rel > 0).astype(jnp.int32) * nb
    n = jnp.abs(rel)
    max_exact = nb // 2
    nf = jnp.maximum(n, 1).astype(F32)
    large = max_exact + (jnp.log(nf / max_exact) / math.log(REL_MAX_DIST / max_exact)
                         * (nb - max_exact)).astype(jnp.int32)
    large = jnp.minimum(large, nb - 1)
    return ret + jnp.where(n < max_exact, n, large)


def _rel_bias_kernel(table_ref, bucket_ref, out_ref, *, far_bucket):
    h = pl.program_id(0)
    bucket = bucket_ref[...]
    far = table_ref[far_bucket, h]
    acc = jnp.zeros(bucket.shape, F32)
    for b in range(REL_BUCKETS):
        acc = jnp.where(bucket == b, table_ref[b, h] - far, acc)
    t = bucket.shape[-1]
    out_ref[:, :, 0:t] = acc * LOG2E
    out_ref[:, :, t:2 * t] = acc * LOG2E


def _rel_bias_tiles(rel_bias):
    t = ATTN_TILE
    assert t > REL_MAX_DIST
    heads = rel_bias.shape[1]
    key = jnp.arange(t, dtype=jnp.int32)[:, None]
    query = jnp.arange(t, dtype=jnp.int32)[None, :]
    bucket = jnp.stack([_rel_bucket(key - t - query), _rel_bucket(key - query)])
    kern = functools.partial(_rel_bias_kernel, far_bucket=REL_BUCKETS // 2 - 1)
    return pl.pallas_call(
        kern,
        grid=(heads,),
        in_specs=[pl.BlockSpec(memory_space=pltpu.SMEM), pl.BlockSpec((2, t, t), lambda h: (0, 0, 0))],
        out_specs=pl.BlockSpec((None, 2, t, 2 * t), lambda h: (h, 0, 0, 0)),
        out_shape=jax.ShapeDtypeStruct((heads, 2, t, 2 * t), F32),
        name="rel_bias",
    )(rel_bias, bucket)


def _attn_schedule(nq, heads):
    tiles = [(h, i, j) for h in range(heads) for i in reversed(range(nq)) for j in range(i + 1)]
    events, slot_of, free, n_slots = [], {}, [], 0
    pv_pos = 0
    for pos in range(2 * len(tiles)):
        if pos < len(tiles):
            h, i, j = tiles[pos]
            if (h, i) not in slot_of:
                if not free:
                    free.append(n_slots)
                    n_slots += 1
                slot_of[h, i] = free.pop(0)
            events.append(("score", h, i, j, slot_of[h, i]))
        if pv_pos < len(tiles):
            h, i, j = tiles[pv_pos]
            scored = min(pos + 1, len(tiles))
            if scored >= min(tiles.index((h, i, i)) + 1 + ATTN_LEAD, len(tiles)):
                events.append(("pv", h, i, j, slot_of[h, i]))
                pv_pos += 1
                if j == i:
                    free.append(slot_of[h, i])
    assert pv_pos == len(tiles)
    return events, n_slots


def _diff_attn_kernel(base_ref, lam_ref, q_ref, k_ref, v_ref, nb_ref, sg_ref, o_ref, s_ref, vt_ref, *, lambda_init):
    seq = q_ref.shape[0]
    hd = q_ref.shape[1] // ATTN_HEADS
    t = ATTN_TILE
    nq = seq // t
    events, n_slots = _attn_schedule(nq, ATTN_HEADS)
    assert s_ref.shape[0] == n_slots
    base = base_ref[0]
    lane = lax.broadcasted_iota(jnp.int32, (t, hd), 1)
    key_chunk = lax.broadcasted_iota(jnp.int32, (t, 2 * t), 0) // CHUNK
    query_chunk = (lax.broadcasted_iota(jnp.int32, (t, 2 * t), 1) % t) // CHUNK
    allowed = key_chunk <= query_chunk
    lam = (jnp.exp(jnp.sum(lam_ref[0:1, :] * lam_ref[1:2, :], axis=-1, keepdims=True))
           - jnp.exp(jnp.sum(lam_ref[2:3, :] * lam_ref[3:4, :], axis=-1, keepdims=True)) + lambda_init)

    for h in range(ATTN_HEADS):
        for j in range(nq):
            v = v_ref[j * t:(j + 1) * t, h * hd:(h + 1) * hd]
            vt_ref[h, 0:hd, j * t:(j + 1) * t] = v.astype(F32).T.astype(BF16)
        vt_ref[h, hd:, :] = jnp.ones((ONES_ROWS, seq), BF16)

    def stacked_q(h, i):
        q = q_ref[i * t:(i + 1) * t, h * hd:(h + 1) * hd]
        zero = jnp.zeros_like(q)
        return jnp.concatenate([jnp.where(lane < hd // 2, q, zero), jnp.where(lane >= hd // 2, q, zero)], axis=0)

    def score_tile(h, i, j, slot, qq):
        s = _dot_nt(k_ref[j * t:(j + 1) * t, h * hd:(h + 1) * hd], qq)
        if j == i:
            s = jnp.where(allowed, s + nb_ref[h, 1], NEG_INF)
        elif j == i - 1:
            s = s + nb_ref[h, 0]
        s_ref[base + slot, j] = s
        return jnp.max(s, axis=0, keepdims=True)

    def finish(h, i, acc):
        o_maps = acc[:hd] / acc[hd:hd + 1]
        o = (o_maps[:, :t] - lam * o_maps[:, t:]).T
        o = _rms(o, sg_ref[...]) * (1.0 - lambda_init)
        o_ref[i * t:(i + 1) * t, h * hd:(h + 1) * hd] = o.astype(o_ref.dtype)

    qq, mx, acc = {}, {}, {}
    for kind, h, i, j, slot in events:
        if kind == "score":
            if j == 0:
                qq[h, i] = stacked_q(h, i)
            tile_max = score_tile(h, i, j, slot, qq[h, i])
            mx[h, i] = tile_max if j == 0 else jnp.maximum(mx[h, i], tile_max)
        else:
            p = jnp.exp2((s_ref[base + slot, j] - mx[h, i]).astype(BF16))
            pv = _dot(vt_ref[h, :, j * t:(j + 1) * t], p)
            acc[h, i] = pv if j == 0 else acc[h, i] + pv
            if j == i:
                finish(h, i, acc.pop((h, i)))


def _diff_attn(lam_vecs, q, k, v, near_bias, subln_g, *, batch, seq, lambda_init):
    n, width = q.shape
    t = ATTN_TILE
    hd = width // DIFF_HEADS
    nq = seq // t
    q3, k3, v3 = (a.reshape(batch, seq, width) for a in (q, k, v))
    head_block = pl.BlockSpec((None, seq, ATTN_HEADS * hd), lambda g, b: (b, 0, g))
    n_slots = _attn_schedule(nq, ATTN_HEADS)[1]
    kern = functools.partial(_diff_attn_kernel, lambda_init=lambda_init)
    out = pl.pallas_call(
        kern,
        grid=(DIFF_HEADS // ATTN_HEADS, batch),
        in_specs=[
            pl.BlockSpec(memory_space=pltpu.SMEM),
            pl.BlockSpec(lam_vecs.shape, lambda g, b: (0, 0)),
            head_block, head_block, head_block,
            pl.BlockSpec((ATTN_HEADS, 2, t, 2 * t), lambda g, b: (g, 0, 0, 0)),
            pl.BlockSpec(subln_g.shape, lambda g, b: (0, 0)),
        ],
        out_specs=head_block,
        out_shape=jax.ShapeDtypeStruct((batch, seq, width), BF16),
        scratch_shapes=[pltpu.VMEM((n_slots, nq, t, 2 * t), F32),
                        pltpu.VMEM((ATTN_HEADS, hd + ONES_ROWS, seq), BF16)],
        compiler_params=pltpu.CompilerParams(dimension_semantics=("arbitrary",) * 2, vmem_limit_bytes=VMEM_LIMIT),
        name="diff_attn",
    )(jnp.zeros((1,), jnp.int32), lam_vecs, q3, k3, v3, near_bias, subln_g)
    return out.reshape(n, width)


def _mem_kv_kernel(mem_ref, g_ref, w_ref, kv_ref):
    mn = _rms(mem_ref[...], g_ref[...]).astype(BF16)
    kv_ref[...] = _dot(mn, w_ref[...]).astype(BF16)


def _mem_kv(mem2, g, w, *, mem_len):
    n, d = mem2.shape
    return pl.pallas_call(
        _mem_kv_kernel,
        grid=(n // mem_len,),
        in_specs=[pl.BlockSpec((mem_len, d), lambda i: (i, 0)), _resident(g.shape), _resident(w.shape)],
        out_specs=pl.BlockSpec((mem_len, w.shape[1]), lambda i: (i, 0)),
        out_shape=jax.ShapeDtypeStruct((n, w.shape[1]), BF16),
        compiler_params=pltpu.CompilerParams(dimension_semantics=("arbitrary",), vmem_limit_bytes=VMEM_LIMIT),
        name="mem_kv",
    )(mem2, g, w)


def _merge_cross_kernel(x_ref, o_ref, gaya_ref, gb_ref, wb_ref, wmo_ref, g_ref, wq_ref, ck_ref, cv_ref, wo_ref,
                        out_ref, *, q_scale):
    tm, d = x_ref.shape
    y_b = _dot(o_ref[...], wb_ref[...])
    merged = gaya_ref[...].astype(F32) + gb_ref[...].astype(F32) * y_b
    h = x_ref[...] + _dot(merged.astype(BF16), wmo_ref[...])

    hn = _rms(h, g_ref[...]).astype(BF16)
    cq = (_dot(hn, wq_ref[...]) * q_scale).astype(BF16)
    hd = d // CROSS_HEADS
    cols = [slice(a * hd, (a + 1) * hd) for a in range(CROSS_HEADS)]
    scores = [_dot_nt(cq[:, c], ck_ref[:, c]) for c in cols]
    probs = [jnp.exp2(s - jnp.max(s, axis=-1, keepdims=True)) for s in scores]
    heads = [_dot(p.astype(BF16), cv_ref[:, c]) / jnp.sum(p, axis=-1, keepdims=True) for p, c in zip(probs, cols)]
    o = jnp.concatenate(heads, axis=-1).astype(BF16)
    out_ref[...] = h + _dot(o, wo_ref[...])


def _merge_cross(x2, o, gaya, gb, wb, wmo, g, wq, ckv, wo, *, seq, mem_len):
    n, d = x2.shape
    tm = TOKEN_TILE
    tiles_per_seq = seq // tm
    row = lambda c: pl.BlockSpec((tm, c), lambda i: (i, 0))
    kern = functools.partial(_merge_cross_kernel, q_scale=(d // CROSS_HEADS) ** -0.5 * LOG2E)
    return pl.pallas_call(
        kern,
        grid=(n // tm,),
        in_specs=[row(d), row(o.shape[1]), row(d), row(d), _resident(wb.shape), _resident(wmo.shape),
                  _resident(g.shape), _resident(wq.shape),
                  pl.BlockSpec((mem_len, d), lambda i: (i // tiles_per_seq, 0)),
                  pl.BlockSpec((mem_len, d), lambda i: (i // tiles_per_seq, 1)),
                  _resident(wo.shape)],
        out_specs=row(d),
        out_shape=jax.ShapeDtypeStruct((n, d), F32),
        compiler_params=pltpu.CompilerParams(dimension_semantics=("arbitrary",), vmem_limit_bytes=VMEM_LIMIT),
        name="merge_cross",
    )(x2, o, gaya, gb, wb, wmo, g, wq, ckv, ckv, wo)


def _mlp_kernel(h_ref, g_ref, w1_ref, w2_ref, gf_ref, out_ref, *, final_norm):
    h = h_ref[...]
    hn = _rms(h, g_ref[...]).astype(BF16)
    acc = h
    for c in range(w1_ref.shape[1] // FF_CHUNK):
        cols = slice(c * FF_CHUNK, (c + 1) * FF_CHUNK)
        a = jnp.maximum(_dot(hn, w1_ref[:, cols]), 0.0)
        acc = acc + _dot((a * a).astype(BF16), w2_ref[cols, :])
    out_ref[...] = _rms(acc, gf_ref[...]) if final_norm else acc


def _mlp(h, g, w1, w2, gf, *, final_norm):
    n, d = h.shape
    tm = MLP_TILE
    row = pl.BlockSpec((tm, d), lambda i: (i, 0))
    kern = functools.partial(_mlp_kernel, final_norm=final_norm)
    return pl.pallas_call(
        kern,
        grid=(n // tm,),
        in_specs=[row, _resident(g.shape), _resident(w1.shape), _resident(w2.shape), _resident(gf.shape)],
        out_specs=row,
        out_shape=jax.ShapeDtypeStruct((n, d), F32),
        compiler_params=pltpu.CompilerParams(dimension_semantics=("arbitrary",), vmem_limit_bytes=VMEM_LIMIT),
        name="mlp",
    )(h, g, w1, w2, gf)


def kernel(x, mem, norm_mix_g, w_in, w_pool_group, pool_scale, w_a_proj, lambda_q1, lambda_k1, lambda_q2, lambda_k2, subln_g, rel_bias, w_b_proj, w_gate, b_gate, w_out, norm_cross_g, norm_mem_g, w_cq, w_ckv, w_co, norm_mlp_g, w_ff1, w_ff2, final_norm_g):
    batch, seq, d = x.shape
    mem_len = mem.shape[1]
    depth = w_in.shape[0]
    pool_width = w_a_proj.shape[1]
    assert pool_width == d and seq % TOKEN_TILE == 0 and seq % ATTN_TILE == 0
    row = lambda a: a.reshape(1, -1).astype(F32)
    bf = lambda a: a.astype(BF16)

    near_bias = _rel_bias_tiles(rel_bias.astype(F32))
    h = x.reshape(batch * seq, d)
    mem2 = mem.reshape(batch * mem_len, d)
    for l in range(depth):
        lambda_init = LAMBDA_INIT_BASE - LAMBDA_INIT_AMP * math.exp(-LAMBDA_INIT_RATE * l)
        q, k, v, gaya, gb = _mixer_front(
            h, row(norm_mix_g[l]), bf(w_in[l, :, :pool_width]), bf(w_in[l, :, pool_width:]), bf(w_gate[l]),
            row(b_gate[l]), bf(w_pool_group[l]), row(pool_scale[l]), bf(w_a_proj[l]),
            seq=seq, q_scale=DIFF_HEAD_DIM ** -0.5 * LOG2E)
        lam_vecs = jnp.stack([lambda_q1[l], lambda_k1[l], lambda_q2[l], lambda_k2[l]]).astype(F32)
        o = _diff_attn(lam_vecs, q, k, v, near_bias, row(subln_g[l]), batch=batch, seq=seq, lambda_init=lambda_init)
        ckv = _mem_kv(mem2, row(norm_mem_g[l]), bf(w_ckv[l]), mem_len=mem_len)
        h = _merge_cross(h, o, gaya, gb, bf(w_b_proj[l]), bf(w_out[l]), row(norm_cross_g[l]), bf(w_cq[l]), ckv,
                         bf(w_co[l]), seq=seq, mem_len=mem_len)
        h = _mlp(h, row(norm_mlp_g[l]), bf(w_ff1[l]), bf(w_ff2[l]), row(final_norm_g), final_norm=(l == depth - 1))
    return h.reshape(batch, seq, d)
```

```python
import functools
import math

import jax
import jax.numpy as jnp
from jax import lax
from jax.experimental import pallas as pl
from jax.experimental.pallas import tpu as pltpu

CHUNK = 64
POOL_WINDOWS = (2, 4, 8, 16)
DIFF_HEADS = 8
DIFF_HEAD_DIM = 64
LAMBDA_INIT_BASE = 0.8
LAMBDA_INIT_AMP = 0.6
LAMBDA_INIT_RATE = 0.3
REL_BUCKETS = 32
REL_MAX_DIST = 128
CROSS_HEADS = 4
EPS = 1e-6
NEG_INF = -1e30
LOG2E = math.log2(math.e)

POOL_HALO = 16
TOKEN_TILE = 512
MLP_TILE = 1024
ATTN_TILE = 256
ATTN_HEADS = 2
ONES_ROWS = 16
ATTN_LEAD = 2
FF_CHUNK = 1024
VMEM_LIMIT = 56 * 1024 * 1024

BF16 = jnp.bfloat16
F32 = jnp.float32


def _resident(shape):
    n = len(shape)
    return pl.BlockSpec(shape, lambda *_: (0,) * n, pipeline_mode=pl.Buffered(1))


def _rms(x, g):
    return x * lax.rsqrt(jnp.mean(x * x, axis=-1, keepdims=True) + EPS) * g


def _dot(a, b):
    return jnp.dot(a, b, preferred_element_type=F32)


def _dot_nt(a, b):
    return lax.dot_general(a, b, (((1,), (1,)), ((), ())), preferred_element_type=F32)


def _mixer_front_kernel(x_ref, g_ref, wu_ref, wqkv_ref, wg_ref, bg_ref, wp_ref, ps_ref, wa_ref,
                        q_ref, k_ref, v_ref, gaya_ref, gb_ref, ubuf_ref, ga_ref, *, tiles_per_seq, q_scale):
    tm, d = x_ref.shape
    i = pl.program_id(0)
    tile_in_seq = i % tiles_per_seq
    xn = _rms(x_ref[...], g_ref[...]).astype(BF16)

    @pl.when(tile_in_seq == 0)
    def _():
        ubuf_ref[0:POOL_HALO, :] = jnp.zeros((POOL_HALO, d), F32)

    ubuf_ref[POOL_HALO:POOL_HALO + tm, :] = _dot(xn, wu_ref[...])

    width = q_ref.shape[1]
    q_ref[...] = (_dot(xn, wqkv_ref[:, 0:width]) * q_scale).astype(BF16)
    k_ref[...] = _dot(xn, wqkv_ref[:, width:2 * width]).astype(BF16)
    v_ref[...] = _dot(xn, wqkv_ref[:, 2 * width:3 * width]).astype(BF16)
    z = _dot(xn, wg_ref[...]) + bg_ref[...]
    gates = 1.0 / (1.0 + jnp.exp(-z))
    ga_ref[...] = gates[:, :d]
    gb_ref[...] = gates[:, d:].astype(gb_ref.dtype)

    pos = tile_in_seq * tm + lax.broadcasted_iota(jnp.int32, (tm, 1), 0)
    gd = d // len(POOL_WINDOWS)
    mapped = []
    for g, w in enumerate(POOL_WINDOWS):
        cols = slice(g * gd, (g + 1) * gd)
        u_g = ubuf_ref[POOL_HALO:POOL_HALO + tm, cols]
        acc = u_g
        for j in range(1, w):
            acc = acc + ubuf_ref[POOL_HALO - j:POOL_HALO - j + tm, cols]
        inv_count = 1.0 / jnp.minimum(pos + 1, w).astype(F32)
        pooled = acc * inv_count - u_g
        mapped.append(_dot(pooled.astype(BF16), wp_ref[g]))
    ubuf_ref[0:POOL_HALO, :] = ubuf_ref[tm:tm + POOL_HALO, :]
    y = jnp.concatenate(mapped, axis=-1) * ps_ref[...]
    gaya_ref[...] = (ga_ref[...] * _dot(y.astype(BF16), wa_ref[...])).astype(gaya_ref.dtype)


def _mixer_front(x2, g, wu, wqkv, wg, bg, wp, ps, wa, *, seq, q_scale):
    n, d = x2.shape
    tm = TOKEN_TILE
    width = wqkv.shape[1] // 3
    row = lambda c: pl.BlockSpec((tm, c), lambda i: (i, 0))
    kern = functools.partial(_mixer_front_kernel, tiles_per_seq=seq // tm, q_scale=q_scale)
    return pl.pallas_call(
        kern,
        grid=(n // tm,),
        in_specs=[row(d), _resident(g.shape), _resident(wu.shape), _resident(wqkv.shape), _resident(wg.shape),
                  _resident(bg.shape), _resident(wp.shape), _resident(ps.shape), _resident(wa.shape)],
        out_specs=[row(width), row(width), row(width), row(d), row(d)],
        out_shape=[jax.ShapeDtypeStruct((n, width), BF16)] * 3 + [jax.ShapeDtypeStruct((n, d), BF16)] * 2,
        scratch_shapes=[pltpu.VMEM((POOL_HALO + tm, d), F32), pltpu.VMEM((tm, d), F32)],
        compiler_params=pltpu.CompilerParams(dimension_semantics=("arbitrary",), vmem_limit_bytes=VMEM_LIMIT),
        name="mixer_front",
    )(x2, g, wu, wqkv, wg, bg, wp, ps, wa)


def _rel_bucket(rel):
    nb = REL_BUCKETS // 2
    ret = (rel > 0).astype(jnp.int32) * nb
    n = jnp.abs(rel)
    max_exact = nb // 2
    nf = jnp.maximum(n, 1).astype(F32)
    large = max_exact + (jnp.log(nf / max_exact) / math.log(REL_MAX_DIST / max_exact)
                         * (nb - max_exact)).astype(jnp.int32)
    large = jnp.minimum(large, nb - 1)
    return ret + jnp.where(n < max_exact, n, large)


def _rel_bias_kernel(table_ref, bucket_ref, out_ref, *, far_bucket):
    h = pl.program_id(0)
    bucket = bucket_ref[...]
    far = table_ref[far_bucket, h]
    acc = jnp.zeros(bucket.shape, F32)
    for b in range(REL_BUCKETS):
        acc = jnp.where(bucket == b, table_ref[b, h] - far, acc)
    t = bucket.shape[-1]
    out_ref[:, :, 0:t] = acc * LOG2E
    out_ref[:, :, t:2 * t] = acc * LOG2E


def _rel_bias_tiles(rel_bias):
    t = ATTN_TILE
    assert t > REL_MAX_DIST
    heads = rel_bias.shape[1]
    key = jnp.arange(t, dtype=jnp.int32)[:, None]
    query = jnp.arange(t, dtype=jnp.int32)[None, :]
    bucket = jnp.stack([_rel_bucket(key - t - query), _rel_bucket(key - query)])
    kern = functools.partial(_rel_bias_kernel, far_bucket=REL_BUCKETS // 2 - 1)
    return pl.pallas_call(
        kern,
        grid=(heads,),
        in_specs=[pl.BlockSpec(memory_space=pltpu.SMEM), pl.BlockSpec((2, t, t), lambda h: (0, 0, 0))],
        out_specs=pl.BlockSpec((None, 2, t, 2 * t), lambda h: (h, 0, 0, 0)),
        out_shape=jax.ShapeDtypeStruct((heads, 2, t, 2 * t), F32),
        name="rel_bias",
    )(rel_bias, bucket)


def _attn_schedule(nq, heads):
    tiles = [(h, i, j) for h in range(heads) for i in reversed(range(nq)) for j in range(i + 1)]
    events, slot_of, free, n_slots = [], {}, [], 0
    pv_pos = 0
    for pos in range(2 * len(tiles)):
        if pos < len(tiles):
            h, i, j = tiles[pos]
            if (h, i) not in slot_of:
                if not free:
                    free.append(n_slots)
                    n_slots += 1
                slot_of[h, i] = free.pop(0)
            events.append(("score", h, i, j, slot_of[h, i]))
        if pv_pos < len(tiles):
            h, i, j = tiles[pv_pos]
            scored = min(pos + 1, len(tiles))
            if scored >= min(tiles.index((h, i, i)) + 1 + ATTN_LEAD, len(tiles)):
                events.append(("pv", h, i, j, slot_of[h, i]))
                pv_pos += 1
                if j == i:
                    free.append(slot_of[h, i])
    assert pv_pos == len(tiles)
    return events, n_slots


def _diff_attn_kernel(base_ref, lam_ref, q_ref, k_ref, v_ref, nb_ref, sg_ref, o_ref, s_ref, vt_ref, *, lambda_init):
    seq = q_ref.shape[0]
    hd = q_ref.shape[1] // ATTN_HEADS
    t = ATTN_TILE
    nq = seq // t
    events, n_slots = _attn_schedule(nq, ATTN_HEADS)
    assert s_ref.shape[0] == n_slots
    base = base_ref[0]
    lane = lax.broadcasted_iota(jnp.int32, (t, hd), 1)
    key_chunk = lax.broadcasted_iota(jnp.int32, (t, 2 * t), 0) // CHUNK
    query_chunk = (lax.broadcasted_iota(jnp.int32, (t, 2 * t), 1) % t) // CHUNK
    allowed = key_chunk <= query_chunk
    lam = (jnp.exp(jnp.sum(lam_ref[0:1, :] * lam_ref[1:2, :], axis=-1, keepdims=True))
           - jnp.exp(jnp.sum(lam_ref[2:3, :] * lam_ref[3:4, :], axis=-1, keepdims=True)) + lambda_init)

    for h in range(ATTN_HEADS):
        for j in range(nq):
            v = v_ref[j * t:(j + 1) * t, h * hd:(h + 1) * hd]
            vt_ref[h, 0:hd, j * t:(j + 1) * t] = v.astype(F32).T.astype(BF16)
        vt_ref[h, hd:, :] = jnp.ones((ONES_ROWS, seq), BF16)

    def stacked_q(h, i):
        q = q_ref[i * t:(i + 1) * t, h * hd:(h + 1) * hd]
        zero = jnp.zeros_like(q)
        return jnp.concatenate([jnp.where(lane < hd // 2, q, zero), jnp.where(lane >= hd // 2, q, zero)], axis=0)

    def score_tile(h, i, j, slot, qq):
        s = _dot_nt(k_ref[j * t:(j + 1) * t, h * hd:(h + 1) * hd], qq)
        if j == i:
            s = jnp.where(allowed, s + nb_ref[h, 1], NEG_INF)
        elif j == i - 1:
            s = s + nb_ref[h, 0]
        s_ref[base + slot, j] = s
        return jnp.max(s, axis=0, keepdims=True)

    def finish(h, i, acc):
        o_maps = acc[:hd] / acc[hd:hd + 1]
        o = (o_maps[:, :t] - lam * o_maps[:, t:]).T
        o = _rms(o, sg_ref[...]) * (1.0 - lambda_init)
        o_ref[i * t:(i + 1) * t, h * hd:(h + 1) * hd] = o.astype(o_ref.dtype)

    qq, mx, acc = {}, {}, {}
    for kind, h, i, j, slot in events:
        if kind == "score":
            if j == 0:
                qq[h, i] = stacked_q(h, i)
            tile_max = score_tile(h, i, j, slot, qq[h, i])
            mx[h, i] = tile_max if j == 0 else jnp.maximum(mx[h, i], tile_max)
        else:
            p = jnp.exp2(s_ref[base + slot, j] - mx[h, i]).astype(BF16)
            pv = _dot(vt_ref[h, :, j * t:(j + 1) * t], p)
            acc[h, i] = pv if j == 0 else acc[h, i] + pv
            if j == i:
                finish(h, i, acc.pop((h, i)))


def _diff_attn(lam_vecs, q, k, v, near_bias, subln_g, *, batch, seq, lambda_init):
    n, width = q.shape
    t = ATTN_TILE
    hd = width // DIFF_HEADS
    nq = seq // t
    q3, k3, v3 = (a.reshape(batch, seq, width) for a in (q, k, v))
    head_block = pl.BlockSpec((None, seq, ATTN_HEADS * hd), lambda g, b: (b, 0, g))
    n_slots = _attn_schedule(nq, ATTN_HEADS)[1]
    kern = functools.partial(_diff_attn_kernel, lambda_init=lambda_init)
    out = pl.pallas_call(
        kern,
        grid=(DIFF_HEADS // ATTN_HEADS, batch),
        in_specs=[
            pl.BlockSpec(memory_space=pltpu.SMEM),
            pl.BlockSpec(lam_vecs.shape, lambda g, b: (0, 0)),
            head_block, head_block, head_block,
            pl.BlockSpec((ATTN_HEADS, 2, t, 2 * t), lambda g, b: (g, 0, 0, 0)),
            pl.BlockSpec(subln_g.shape, lambda g, b: (0, 0)),
        ],
        out_specs=head_block,
        out_shape=jax.ShapeDtypeStruct((batch, seq, width), BF16),
        scratch_shapes=[pltpu.VMEM((n_slots, nq, t, 2 * t), F32),
                        pltpu.VMEM((ATTN_HEADS, hd + ONES_ROWS, seq), BF16)],
        compiler_params=pltpu.CompilerParams(dimension_semantics=("arbitrary",) * 2, vmem_limit_bytes=VMEM_LIMIT),
        name="diff_attn",
    )(jnp.zeros((1,), jnp.int32), lam_vecs, q3, k3, v3, near_bias, subln_g)
    return out.reshape(n, width)


def _mem_kv_kernel(mem_ref, g_ref, w_ref, kv_ref):
    mn = _rms(mem_ref[...], g_ref[...]).astype(BF16)
    kv_ref[...] = _dot(mn, w_ref[...]).astype(BF16)


def _mem_kv(mem2, g, w, *, mem_len):
    n, d = mem2.shape
    return pl.pallas_call(
        _mem_kv_kernel,
        grid=(n // mem_len,),
        in_specs=[pl.BlockSpec((mem_len, d), lambda i: (i, 0)), _resident(g.shape), _resident(w.shape)],
        out_specs=pl.BlockSpec((mem_len, w.shape[1]), lambda i: (i, 0)),
        out_shape=jax.ShapeDtypeStruct((n, w.shape[1]), BF16),
        compiler_params=pltpu.CompilerParams(dimension_semantics=("arbitrary",), vmem_limit_bytes=VMEM_LIMIT),
        name="mem_kv",
    )(mem2, g, w)


def _merge_cross_kernel(x_ref, o_ref, gaya_ref, gb_ref, wb_ref, wmo_ref, g_ref, wq_ref, ck_ref, cv_ref, wo_ref,
                        out_ref, *, q_scale):
    tm, d = x_ref.shape
    y_b = _dot(o_ref[...], wb_ref[...])
    merged = gaya_ref[...].astype(F32) + gb_ref[...].astype(F32) * y_b
    h = x_ref[...] + _dot(merged.astype(BF16), wmo_ref[...])

    hn = _rms(h, g_ref[...]).astype(BF16)
    cq = (_dot(hn, wq_ref[...]) * q_scale).astype(BF16)
    hd = d // CROSS_HEADS
    cols = [slice(a * hd, (a + 1) * hd) for a in range(CROSS_HEADS)]
    scores = [_dot_nt(cq[:, c], ck_ref[:, c]) for c in cols]
    probs = [jnp.exp2(s - jnp.max(s, axis=-1, keepdims=True)) for s in scores]
    heads = [_dot(p.astype(BF16), cv_ref[:, c]) / jnp.sum(p, axis=-1, keepdims=True) for p, c in zip(probs, cols)]
    o = jnp.concatenate(heads, axis=-1).astype(BF16)
    out_ref[...] = h + _dot(o, wo_ref[...])


def _merge_cross(x2, o, gaya, gb, wb, wmo, g, wq, ckv, wo, *, seq, mem_len):
    n, d = x2.shape
    tm = TOKEN_TILE
    tiles_per_seq = seq // tm
    row = lambda c: pl.BlockSpec((tm, c), lambda i: (i, 0))
    kern = functools.partial(_merge_cross_kernel, q_scale=(d // CROSS_HEADS) ** -0.5 * LOG2E)
    return pl.pallas_call(
        kern,
        grid=(n // tm,),
        in_specs=[row(d), row(o.shape[1]), row(d), row(d), _resident(wb.shape), _resident(wmo.shape),
                  _resident(g.shape), _resident(wq.shape),
                  pl.BlockSpec((mem_len, d), lambda i: (i // tiles_per_seq, 0)),
                  pl.BlockSpec((mem_len, d), lambda i: (i // tiles_per_seq, 1)),
                  _resident(wo.shape)],
        out_specs=row(d),
        out_shape=jax.ShapeDtypeStruct((n, d), F32),
        compiler_params=pltpu.CompilerParams(dimension_semantics=("arbitrary",), vmem_limit_bytes=VMEM_LIMIT),
        name="merge_cross",
    )(x2, o, gaya, gb, wb, wmo, g, wq, ckv, ckv, wo)


def _mlp_kernel(h_ref, g_ref, w1_ref, w2_ref, gf_ref, out_ref, *, final_norm):
    h = h_ref[...]
    hn = _rms(h, g_ref[...]).astype(BF16)
    acc = h
    for c in range(w1_ref.shape[1] // FF_CHUNK):
        cols = slice(c * FF_CHUNK, (c + 1) * FF_CHUNK)
        a = jnp.maximum(_dot(hn, w1_ref[:, cols]), 0.0)
        acc = acc + _dot((a * a).astype(BF16), w2_ref[cols, :])
    out_ref[...] = _rms(acc, gf_ref[...]) if final_norm else acc


def _mlp(h, g, w1, w2, gf, *, final_norm):
    n, d = h.shape
    tm = MLP_TILE
    row = pl.BlockSpec((tm, d), lambda i: (i, 0))
    kern = functools.partial(_mlp_kernel, final_norm=final_norm)
    return pl.pallas_call(
        kern,
        grid=(n // tm,),
        in_specs=[row, _resident(g.shape), _resident(w1.shape), _resident(w2.shape), _resident(gf.shape)],
        out_specs=row,
        out_shape=jax.ShapeDtypeStruct((n, d), F32),
        compiler_params=pltpu.CompilerParams(dimension_semantics=("arbitrary",), vmem_limit_bytes=VMEM_LIMIT),
        name="mlp",
    )(h, g, w1, w2, gf)


def kernel(x, mem, norm_mix_g, w_in, w_pool_group, pool_scale, w_a_proj, lambda_q1, lambda_k1, lambda_q2, lambda_k2, subln_g, rel_bias, w_b_proj, w_gate, b_gate, w_out, norm_cross_g, norm_mem_g, w_cq, w_ckv, w_co, norm_mlp_g, w_ff1, w_ff2, final_norm_g):
    batch, seq, d = x.shape
    mem_len = mem.shape[1]
    depth = w_in.shape[0]
    pool_width = w_a_proj.shape[1]
    assert pool_width == d and seq % TOKEN_TILE == 0 and seq % ATTN_TILE == 0
    row = lambda a: a.reshape(1, -1).astype(F32)
    bf = lambda a: a.astype(BF16)

    near_bias = _rel_bias_tiles(rel_bias.astype(F32))
    h = x.reshape(batch * seq, d)
    mem2 = mem.reshape(batch * mem_len, d)
    for l in range(depth):
        lambda_init = LAMBDA_INIT_BASE - LAMBDA_INIT_AMP * math.exp(-LAMBDA_INIT_RATE * l)
        q, k, v, gaya, gb = _mixer_front(
            h, row(norm_mix_g[l]), bf(w_in[l, :, :pool_width]), bf(w_in[l, :, pool_width:]), bf(w_gate[l]),
            row(b_gate[l]), bf(w_pool_group[l]), row(pool_scale[l]), bf(w_a_proj[l]),
            seq=seq, q_scale=DIFF_HEAD_DIM ** -0.5 * LOG2E)
        lam_vecs = jnp.stack([lambda_q1[l], lambda_k1[l], lambda_q2[l], lambda_k2[l]]).astype(F32)
        o = _diff_attn(lam_vecs, q, k, v, near_bias, row(subln_g[l]), batch=batch, seq=seq, lambda_init=lambda_init)
        ckv = _mem_kv(mem2, row(norm_mem_g[l]), bf(w_ckv[l]), mem_len=mem_len)
        h = _merge_cross(h, o, gaya, gb, bf(w_b_proj[l]), bf(w_out[l]), row(norm_cross_g[l]), bf(w_cq[l]), ckv,
                         bf(w_co[l]), seq=seq, mem_len=mem_len)
        h = _mlp(h, row(norm_mlp_g[l]), bf(w_ff1[l]), bf(w_ff2[l]), row(final_norm_g), final_norm=(l == depth - 1))
    return h.reshape(batch, seq, d)
```

```python
import functools
import math

import jax
import jax.numpy as jnp
import numpy as np
from jax import lax
from jax.experimental import pallas as pl
from jax.experimental.pallas import tpu as pltpu

CHUNK = 64
POOL_WINDOWS = (2, 4, 8, 16)
DIFF_HEADS = 8
DIFF_HEAD_DIM = 64
LAMBDA_INIT_BASE = 0.8
LAMBDA_INIT_AMP = 0.6
LAMBDA_INIT_RATE = 0.3
REL_BUCKETS = 32
REL_MAX_DIST = 128
CROSS_HEADS = 4
EPS = 1e-6
NEG_INF = -1e30
LOG2E = math.log2(math.e)

POOL_HALO = 16
TOKEN_TILE = 512
MLP_TILE = 512
ATTN_TILE = 256
ATTN_HEADS = 2
ONES_ROWS = 16
ATTN_LEAD = 2
FF_CHUNK = 1024
VMEM_LIMIT = 56 * 1024 * 1024

BF16 = jnp.bfloat16
F32 = jnp.float32


def _resident(shape):
    n = len(shape)
    return pl.BlockSpec(shape, lambda *_: (0,) * n, pipeline_mode=pl.Buffered(1))


def _rms(x, g):
    return x * lax.rsqrt(jnp.mean(x * x, axis=-1, keepdims=True) + EPS) * g


def _dot(a, b):
    return jnp.dot(a, b, preferred_element_type=F32)


def _dot_nt(a, b):
    return lax.dot_general(a, b, (((1,), (1,)), ((), ())), preferred_element_type=F32)


def _mixer_front_kernel(x_ref, g_ref, wu_ref, wqkv_ref, wg_ref, bg_ref, wp_ref, ps_ref, wa_ref,
                        q_ref, k_ref, v_ref, gaya_ref, gb_ref, ubuf_ref, ga_ref, *, tiles_per_seq, q_scale):
    tm, d = x_ref.shape
    i = pl.program_id(0)
    tile_in_seq = i % tiles_per_seq
    xn = _rms(x_ref[...], g_ref[...]).astype(BF16)

    @pl.when(tile_in_seq == 0)
    def _():
        ubuf_ref[0:POOL_HALO, :] = jnp.zeros((POOL_HALO, d), F32)

    ubuf_ref[POOL_HALO:POOL_HALO + tm, :] = _dot(xn, wu_ref[...])

    width = q_ref.shape[1]
    q_ref[...] = (_dot(xn, wqkv_ref[:, 0:width]) * q_scale).astype(BF16)
    k_ref[...] = _dot(xn, wqkv_ref[:, width:2 * width]).astype(BF16)
    v_ref[...] = _dot(xn, wqkv_ref[:, 2 * width:3 * width]).astype(BF16)
    z = _dot(xn, wg_ref[...]) + bg_ref[...]
    gates = 1.0 / (1.0 + jnp.exp(-z))
    ga_ref[...] = gates[:, :d]
    gb_ref[...] = gates[:, d:].astype(gb_ref.dtype)

    pos = tile_in_seq * tm + lax.broadcasted_iota(jnp.int32, (tm, 1), 0)
    gd = d // len(POOL_WINDOWS)
    mapped = []
    for g, w in enumerate(POOL_WINDOWS):
        cols = slice(g * gd, (g + 1) * gd)
        u_g = ubuf_ref[POOL_HALO:POOL_HALO + tm, cols]
        acc = u_g
        for j in range(1, w):
            acc = acc + ubuf_ref[POOL_HALO - j:POOL_HALO - j + tm, cols]
        inv_count = 1.0 / jnp.minimum(pos + 1, w).astype(F32)
        pooled = acc * inv_count - u_g
        mapped.append(_dot(pooled.astype(BF16), wp_ref[g]))
    ubuf_ref[0:POOL_HALO, :] = ubuf_ref[tm:tm + POOL_HALO, :]
    y = jnp.concatenate(mapped, axis=-1) * ps_ref[...]
    gaya_ref[...] = (ga_ref[...] * _dot(y.astype(BF16), wa_ref[...])).astype(gaya_ref.dtype)


def _mixer_front(x2, g, wu, wqkv, wg, bg, wp, ps, wa, *, seq, q_scale):
    n, d = x2.shape
    tm = TOKEN_TILE
    width = wqkv.shape[1] // 3
    row = lambda c: pl.BlockSpec((tm, c), lambda i: (i, 0))
    kern = functools.partial(_mixer_front_kernel, tiles_per_seq=seq // tm, q_scale=q_scale)
    return pl.pallas_call(
        kern,
        grid=(n // tm,),
        in_specs=[row(d), _resident(g.shape), _resident(wu.shape), _resident(wqkv.shape), _resident(wg.shape),
                  _resident(bg.shape), _resident(wp.shape), _resident(ps.shape), _resident(wa.shape)],
        out_specs=[row(width), row(width), row(width), row(d), row(d)],
        out_shape=[jax.ShapeDtypeStruct((n, width), BF16)] * 3 + [jax.ShapeDtypeStruct((n, d), BF16)] * 2,
        scratch_shapes=[pltpu.VMEM((POOL_HALO + tm, d), F32), pltpu.VMEM((tm, d), F32)],
        compiler_params=pltpu.CompilerParams(dimension_semantics=("arbitrary",), vmem_limit_bytes=VMEM_LIMIT),
        name="mixer_front",
    )(x2, g, wu, wqkv, wg, bg, wp, ps, wa)


def _rel_bucket(rel):
    nb = REL_BUCKETS // 2
    max_exact = nb // 2
    n = np.abs(rel)
    scaled = np.log(np.maximum(n, 1) / max_exact) / math.log(REL_MAX_DIST / max_exact) * (nb - max_exact)
    nearest = np.rint(scaled)
    scaled = np.where(np.abs(scaled - nearest) < 1e-9, nearest, scaled)
    large = np.minimum(max_exact + np.floor(scaled).astype(np.int64), nb - 1)
    return ((rel > 0) * nb + np.where(n < max_exact, n, large)).astype(np.int32)


def _rel_bias_kernel(table_ref, bucket_ref, out_ref, *, far_bucket):
    h = pl.program_id(0)
    bucket = bucket_ref[...]
    far = table_ref[far_bucket, h]
    acc = jnp.zeros(bucket.shape, F32)
    for b in range(REL_BUCKETS):
        acc = jnp.where(bucket == b, table_ref[b, h] - far, acc)
    t = bucket.shape[-1]
    out_ref[:, :, 0:t] = acc * LOG2E
    out_ref[:, :, t:2 * t] = acc * LOG2E


def _rel_bias_tiles(rel_bias):
    t = ATTN_TILE
    assert t > REL_MAX_DIST
    heads = rel_bias.shape[1]
    key = np.arange(t, dtype=np.int64)[:, None]
    query = np.arange(t, dtype=np.int64)[None, :]
    bucket = jnp.asarray(np.stack([_rel_bucket(key - t - query), _rel_bucket(key - query)]))
    kern = functools.partial(_rel_bias_kernel, far_bucket=REL_BUCKETS // 2 - 1)
    return pl.pallas_call(
        kern,
        grid=(heads,),
        in_specs=[pl.BlockSpec(memory_space=pltpu.SMEM), pl.BlockSpec((2, t, t), lambda h: (0, 0, 0))],
        out_specs=pl.BlockSpec((None, 2, t, 2 * t), lambda h: (h, 0, 0, 0)),
        out_shape=jax.ShapeDtypeStruct((heads, 2, t, 2 * t), F32),
        name="rel_bias",
    )(rel_bias, bucket)


def _attn_schedule(nq, heads):
    tiles = [(h, i, j) for h in range(heads) for i in reversed(range(nq)) for j in range(i + 1)]
    events, slot_of, free, n_slots = [], {}, [], 0
    pv_pos = 0
    for pos in range(2 * len(tiles)):
        if pos < len(tiles):
            h, i, j = tiles[pos]
            if (h, i) not in slot_of:
                if not free:
                    free.append(n_slots)
                    n_slots += 1
                slot_of[h, i] = free.pop(0)
            events.append(("score", h, i, j, slot_of[h, i]))
        if pv_pos < len(tiles):
            h, i, j = tiles[pv_pos]
            scored = min(pos + 1, len(tiles))
            if scored >= min(tiles.index((h, i, i)) + 1 + ATTN_LEAD, len(tiles)):
                events.append(("pv", h, i, j, slot_of[h, i]))
                pv_pos += 1
                if j == i:
                    free.append(slot_of[h, i])
    assert pv_pos == len(tiles)
    return events, n_slots


def _diff_attn_kernel(base_ref, lam_ref, q_ref, k_ref, v_ref, nb_ref, sg_ref, o_ref, s_ref, vt_ref, *, lambda_init):
    seq = q_ref.shape[0]
    hd = q_ref.shape[1] // ATTN_HEADS
    t = ATTN_TILE
    nq = seq // t
    events, n_slots = _attn_schedule(nq, ATTN_HEADS)
    assert s_ref.shape[0] == n_slots
    base = base_ref[0]
    lane = lax.broadcasted_iota(jnp.int32, (t, hd), 1)
    key_chunk = lax.broadcasted_iota(jnp.int32, (t, 2 * t), 0) // CHUNK
    query_chunk = (lax.broadcasted_iota(jnp.int32, (t, 2 * t), 1) % t) // CHUNK
    allowed = key_chunk <= query_chunk
    lam = (jnp.exp(jnp.sum(lam_ref[0:1, :] * lam_ref[1:2, :], axis=-1, keepdims=True))
           - jnp.exp(jnp.sum(lam_ref[2:3, :] * lam_ref[3:4, :], axis=-1, keepdims=True)) + lambda_init)

    for h in range(ATTN_HEADS):
        for j in range(nq):
            v = v_ref[j * t:(j + 1) * t, h * hd:(h + 1) * hd]
            vt_ref[h, 0:hd, j * t:(j + 1) * t] = v.astype(F32).T.astype(BF16)
        vt_ref[h, hd:, :] = jnp.ones((ONES_ROWS, seq), BF16)

    def stacked_q(h, i):
        q = q_ref[i * t:(i + 1) * t, h * hd:(h + 1) * hd]
        zero = jnp.zeros_like(q)
        return jnp.concatenate([jnp.where(lane < hd // 2, q, zero), jnp.where(lane >= hd // 2, q, zero)], axis=0)

    def score_tile(h, i, j, slot, qq):
        s = _dot_nt(k_ref[j * t:(j + 1) * t, h * hd:(h + 1) * hd], qq)
        if j == i:
            s = jnp.where(allowed, s + nb_ref[h, 1], NEG_INF)
        elif j == i - 1:
            s = s + nb_ref[h, 0]
        s_ref[base + slot, j] = s
        return jnp.max(s, axis=0, keepdims=True)

    def finish(h, i, acc):
        o_maps = acc[:hd] / acc[hd:hd + 1]
        o = (o_maps[:, :t] - lam * o_maps[:, t:]).T
        o = _rms(o, sg_ref[...]) * (1.0 - lambda_init)
        o_ref[i * t:(i + 1) * t, h * hd:(h + 1) * hd] = o.astype(o_ref.dtype)

    qq, mx, acc = {}, {}, {}
    for kind, h, i, j, slot in events:
        if kind == "score":
            if j == 0:
                qq[h, i] = stacked_q(h, i)
            tile_max = score_tile(h, i, j, slot, qq[h, i])
            mx[h, i] = tile_max if j == 0 else jnp.maximum(mx[h, i], tile_max)
        else:
            p = jnp.exp2(s_ref[base + slot, j] - mx[h, i]).astype(BF16)
            pv = _dot(vt_ref[h, :, j * t:(j + 1) * t], p)
            acc[h, i] = pv if j == 0 else acc[h, i] + pv
            if j == i:
                finish(h, i, acc.pop((h, i)))


def _diff_attn(lam_vecs, q, k, v, near_bias, subln_g, *, batch, seq, lambda_init):
    n, width = q.shape
    t = ATTN_TILE
    hd = width // DIFF_HEADS
    nq = seq // t
    q3, k3, v3 = (a.reshape(batch, seq, width) for a in (q, k, v))
    head_block = pl.BlockSpec((None, seq, ATTN_HEADS * hd), lambda g, b: (b, 0, g))
    n_slots = _attn_schedule(nq, ATTN_HEADS)[1]
    kern = functools.partial(_diff_attn_kernel, lambda_init=lambda_init)
    out = pl.pallas_call(
        kern,
        grid=(DIFF_HEADS // ATTN_HEADS, batch),
        in_specs=[
            pl.BlockSpec(memory_space=pltpu.SMEM),
            pl.BlockSpec(lam_vecs.shape, lambda g, b: (0, 0)),
            head_block, head_block, head_block,
            pl.BlockSpec((ATTN_HEADS, 2, t, 2 * t), lambda g, b: (g, 0, 0, 0)),
            pl.BlockSpec(subln_g.shape, lambda g, b: (0, 0)),
        ],
        out_specs=head_block,
        out_shape=jax.ShapeDtypeStruct((batch, seq, width), BF16),
        scratch_shapes=[pltpu.VMEM((n_slots, nq, t, 2 * t), F32),
                        pltpu.VMEM((ATTN_HEADS, hd + ONES_ROWS, seq), BF16)],
        compiler_params=pltpu.CompilerParams(dimension_semantics=("arbitrary",) * 2, vmem_limit_bytes=VMEM_LIMIT),
        name="diff_attn",
    )(jnp.zeros((1,), jnp.int32), lam_vecs, q3, k3, v3, near_bias, subln_g)
    return out.reshape(n, width)


def _mem_kv_kernel(mem_ref, g_ref, w_ref, kv_ref):
    mn = _rms(mem_ref[...], g_ref[...]).astype(BF16)
    kv_ref[...] = _dot(mn, w_ref[...]).astype(BF16)


def _mem_kv(mem2, g, w, *, mem_len):
    n, d = mem2.shape
    return pl.pallas_call(
        _mem_kv_kernel,
        grid=(n // mem_len,),
        in_specs=[pl.BlockSpec((mem_len, d), lambda i: (i, 0)), _resident(g.shape), _resident(w.shape)],
        out_specs=pl.BlockSpec((mem_len, w.shape[1]), lambda i: (i, 0)),
        out_shape=jax.ShapeDtypeStruct((n, w.shape[1]), BF16),
        compiler_params=pltpu.CompilerParams(dimension_semantics=("arbitrary",), vmem_limit_bytes=VMEM_LIMIT),
        name="mem_kv",
    )(mem2, g, w)


def _merge_cross_kernel(x_ref, o_ref, gaya_ref, gb_ref, wb_ref, wmo_ref, g_ref, wq_ref, ck_ref, cv_ref, wo_ref,
                        out_ref, *, q_scale):
    tm, d = x_ref.shape
    y_b = _dot(o_ref[...], wb_ref[...].astype(BF16))
    merged = gaya_ref[...].astype(F32) + gb_ref[...].astype(F32) * y_b
    h = x_ref[...] + _dot(merged.astype(BF16), wmo_ref[...].astype(BF16))

    hn = _rms(h, g_ref[...]).astype(BF16)
    cq = (_dot(hn, wq_ref[...].astype(BF16)) * q_scale).astype(BF16)
    hd = d // CROSS_HEADS
    cols = [slice(a * hd, (a + 1) * hd) for a in range(CROSS_HEADS)]
    scores = [_dot_nt(cq[:, c], ck_ref[:, c]) for c in cols]
    probs = [jnp.exp2(s - jnp.max(s, axis=-1, keepdims=True)) for s in scores]
    heads = [_dot(p.astype(BF16), cv_ref[:, c]) / jnp.sum(p, axis=-1, keepdims=True) for p, c in zip(probs, cols)]
    o = jnp.concatenate(heads, axis=-1).astype(BF16)
    out_ref[...] = h + _dot(o, wo_ref[...].astype(BF16))


def _merge_cross(x2, o, gaya, gb, wb, wmo, g, wq, ckv, wo, *, seq, mem_len):
    n, d = x2.shape
    tm = TOKEN_TILE
    tiles_per_seq = seq // tm
    row = lambda c: pl.BlockSpec((tm, c), lambda i: (i, 0))
    kern = functools.partial(_merge_cross_kernel, q_scale=(d // CROSS_HEADS) ** -0.5 * LOG2E)
    return pl.pallas_call(
        kern,
        grid=(n // tm,),
        in_specs=[row(d), row(o.shape[1]), row(d), row(d), _resident(wb.shape), _resident(wmo.shape),
                  _resident(g.shape), _resident(wq.shape),
                  pl.BlockSpec((mem_len, d), lambda i: (i // tiles_per_seq, 0)),
                  pl.BlockSpec((mem_len, d), lambda i: (i // tiles_per_seq, 1)),
                  _resident(wo.shape)],
        out_specs=row(d),
        out_shape=jax.ShapeDtypeStruct((n, d), F32),
        compiler_params=pltpu.CompilerParams(dimension_semantics=("arbitrary",), vmem_limit_bytes=VMEM_LIMIT),
        name="merge_cross",
    )(x2, o, gaya, gb, wb, wmo, g, wq, ckv, ckv, wo)


def _mlp_kernel(h_ref, g_ref, w1_ref, w2_ref, gf_ref, out_ref, *, final_norm):
    h = h_ref[...]
    hn = _rms(h, g_ref[...]).astype(BF16)
    acc = h
    for c in range(w1_ref.shape[1] // FF_CHUNK):
        cols = slice(c * FF_CHUNK, (c + 1) * FF_CHUNK)
        a = jnp.maximum(_dot(hn, w1_ref[:, cols].astype(BF16)), 0.0)
        acc = acc + _dot((a * a).astype(BF16), w2_ref[cols, :].astype(BF16))
    out_ref[...] = _rms(acc, gf_ref[...]) if final_norm else acc


def _mlp(h, g, w1, w2, gf, *, final_norm):
    n, d = h.shape
    tm = MLP_TILE
    row = pl.BlockSpec((tm, d), lambda i: (i, 0))
    kern = functools.partial(_mlp_kernel, final_norm=final_norm)
    return pl.pallas_call(
        kern,
        grid=(n // tm,),
        in_specs=[row, _resident(g.shape), _resident(w1.shape), _resident(w2.shape), _resident(gf.shape)],
        out_specs=row,
        out_shape=jax.ShapeDtypeStruct((n, d), F32),
        compiler_params=pltpu.CompilerParams(dimension_semantics=("arbitrary",), vmem_limit_bytes=VMEM_LIMIT),
        name="mlp",
    )(h, g, w1, w2, gf)


def kernel(x, mem, norm_mix_g, w_in, w_pool_group, pool_scale, w_a_proj, lambda_q1, lambda_k1, lambda_q2, lambda_k2, subln_g, rel_bias, w_b_proj, w_gate, b_gate, w_out, norm_cross_g, norm_mem_g, w_cq, w_ckv, w_co, norm_mlp_g, w_ff1, w_ff2, final_norm_g):
    batch, seq, d = x.shape
    mem_len = mem.shape[1]
    depth = w_in.shape[0]
    pool_width = w_a_proj.shape[1]
    assert pool_width == d and seq % TOKEN_TILE == 0 and seq % ATTN_TILE == 0
    row = lambda a: a.reshape(1, -1).astype(F32)
    bf = lambda a: a.astype(BF16)

    near_bias = _rel_bias_tiles(rel_bias.astype(F32))
    h = x.reshape(batch * seq, d)
    mem2 = mem.reshape(batch * mem_len, d)
    for l in range(depth):
        lambda_init = LAMBDA_INIT_BASE - LAMBDA_INIT_AMP * math.exp(-LAMBDA_INIT_RATE * l)
        q, k, v, gaya, gb = _mixer_front(
            h, row(norm_mix_g[l]), bf(w_in[l, :, :pool_width]), bf(w_in[l, :, pool_width:]), bf(w_gate[l]),
            row(b_gate[l]), bf(w_pool_group[l]), row(pool_scale[l]), bf(w_a_proj[l]),
            seq=seq, q_scale=DIFF_HEAD_DIM ** -0.5 * LOG2E)
        lam_vecs = jnp.stack([lambda_q1[l], lambda_k1[l], lambda_q2[l], lambda_k2[l]]).astype(F32)
        o = _diff_attn(lam_vecs, q, k, v, near_bias, row(subln_g[l]), batch=batch, seq=seq, lambda_init=lambda_init)
        ckv = _mem_kv(mem2, row(norm_mem_g[l]), bf(w_ckv[l]), mem_len=mem_len)
        h = _merge_cross(h, o, gaya, gb, w_b_proj[l], w_out[l], row(norm_cross_g[l]), w_cq[l], ckv, w_co[l],
                         seq=seq, mem_len=mem_len)
        h = _mlp(h, row(norm_mlp_g[l]), w_ff1[l], w_ff2[l], row(final_norm_g), final_norm=(l == depth - 1))
    return h.reshape(batch, seq, d)
```

```python
import functools
import math

import jax
import jax.numpy as jnp
import numpy as np
from jax import lax
from jax.experimental import pallas as pl
from jax.experimental.pallas import tpu as pltpu

CHUNK = 64
POOL_WINDOWS = (2, 4, 8, 16)
DIFF_HEADS = 8
DIFF_HEAD_DIM = 64
LAMBDA_INIT_BASE = 0.8
LAMBDA_INIT_AMP = 0.6
LAMBDA_INIT_RATE = 0.3
REL_BUCKETS = 32
REL_MAX_DIST = 128
CROSS_HEADS = 4
EPS = 1e-6
NEG_INF = -1e30
LOG2E = math.log2(math.e)

POOL_HALO = 16
TOKEN_TILE = 512
MLP_TILE = 512
ATTN_TILE = 256
ATTN_HEADS = 2
ONES_ROWS = 16
ATTN_LEAD = 2
FF_CHUNK = 1024
VMEM_LIMIT = 56 * 1024 * 1024

BF16 = jnp.bfloat16
F32 = jnp.float32


def _resident(shape):
    n = len(shape)
    return pl.BlockSpec(shape, lambda *_: (0,) * n, pipeline_mode=pl.Buffered(1))


def _rms(x, g):
    return x * lax.rsqrt(jnp.mean(x * x, axis=-1, keepdims=True) + EPS) * g


def _dot(a, b):
    return jnp.dot(a, b, preferred_element_type=F32)


def _dot_nt(a, b):
    return lax.dot_general(a, b, (((1,), (1,)), ((), ())), preferred_element_type=F32)


def _mixer_front_kernel(x_ref, g_ref, win_ref, wg_ref, bg_ref, wp_ref, ps_ref, wa_ref,
                        q_ref, k_ref, v_ref, gaya_ref, gb_ref, ubuf_ref, ga_ref, *, tiles_per_seq, q_scale):
    tm, d = x_ref.shape
    i = pl.program_id(0)
    tile_in_seq = i % tiles_per_seq
    xn = _rms(x_ref[...], g_ref[...]).astype(BF16)

    @pl.when(tile_in_seq == 0)
    def _():
        ubuf_ref[0:POOL_HALO, :] = jnp.zeros((POOL_HALO, d), F32)

    def w_in(lo, hi):
        return win_ref[:, lo:hi].astype(BF16)

    ubuf_ref[POOL_HALO:POOL_HALO + tm, :] = _dot(xn, w_in(0, d))

    width = q_ref.shape[1]
    q_ref[...] = (_dot(xn, w_in(d, d + width)) * q_scale).astype(BF16)
    k_ref[...] = _dot(xn, w_in(d + width, d + 2 * width)).astype(BF16)
    v_ref[...] = _dot(xn, w_in(d + 2 * width, d + 3 * width)).astype(BF16)
    z = _dot(xn, wg_ref[...].astype(BF16)) + bg_ref[...]
    gates = 1.0 / (1.0 + jnp.exp(-z))
    ga_ref[...] = gates[:, :d]
    gb_ref[...] = gates[:, d:].astype(gb_ref.dtype)

    pos = tile_in_seq * tm + lax.broadcasted_iota(jnp.int32, (tm, 1), 0)
    gd = d // len(POOL_WINDOWS)
    mapped = []
    for g, w in enumerate(POOL_WINDOWS):
        cols = slice(g * gd, (g + 1) * gd)
        u_g = ubuf_ref[POOL_HALO:POOL_HALO + tm, cols]
        acc = u_g
        for j in range(1, w):
            acc = acc + ubuf_ref[POOL_HALO - j:POOL_HALO - j + tm, cols]
        inv_count = 1.0 / jnp.minimum(pos + 1, w).astype(F32)
        pooled = acc * inv_count - u_g
        mapped.append(_dot(pooled.astype(BF16), wp_ref[g].astype(BF16)))
    ubuf_ref[0:POOL_HALO, :] = ubuf_ref[tm:tm + POOL_HALO, :]
    y = jnp.concatenate(mapped, axis=-1) * ps_ref[...]
    gaya_ref[...] = (ga_ref[...] * _dot(y.astype(BF16), wa_ref[...].astype(BF16))).astype(gaya_ref.dtype)


def _mixer_front(x2, g, win, wg, bg, wp, ps, wa, *, seq, q_scale):
    n, d = x2.shape
    tm = TOKEN_TILE
    width = (win.shape[1] - d) // 3
    row = lambda c: pl.BlockSpec((tm, c), lambda i: (i, 0))
    kern = functools.partial(_mixer_front_kernel, tiles_per_seq=seq // tm, q_scale=q_scale)
    return pl.pallas_call(
        kern,
        grid=(n // tm,),
        in_specs=[row(d), _resident(g.shape), _resident(win.shape), _resident(wg.shape), _resident(bg.shape),
                  _resident(wp.shape), _resident(ps.shape), _resident(wa.shape)],
        out_specs=[row(width), row(width), row(width), row(d), row(d)],
        out_shape=[jax.ShapeDtypeStruct((n, width), BF16)] * 3 + [jax.ShapeDtypeStruct((n, d), BF16)] * 2,
        scratch_shapes=[pltpu.VMEM((POOL_HALO + tm, d), F32), pltpu.VMEM((tm, d), F32)],
        compiler_params=pltpu.CompilerParams(dimension_semantics=("arbitrary",), vmem_limit_bytes=VMEM_LIMIT),
        name="mixer_front",
    )(x2, g, win, wg, bg, wp, ps, wa)


def _rel_bucket(rel):
    nb = REL_BUCKETS // 2
    max_exact = nb // 2
    n = np.abs(rel)
    scaled = np.log(np.maximum(n, 1) / max_exact) / math.log(REL_MAX_DIST / max_exact) * (nb - max_exact)
    nearest = np.rint(scaled)
    scaled = np.where(np.abs(scaled - nearest) < 1e-9, nearest, scaled)
    large = np.minimum(max_exact + np.floor(scaled).astype(np.int64), nb - 1)
    return ((rel > 0) * nb + np.where(n < max_exact, n, large)).astype(np.int32)


def _rel_bias_kernel(table_ref, bucket_ref, out_ref, *, far_bucket):
    h = pl.program_id(0)
    bucket = bucket_ref[...]
    far = table_ref[far_bucket, h]
    acc = jnp.zeros(bucket.shape, F32)
    for b in range(REL_BUCKETS):
        acc = jnp.where(bucket == b, table_ref[b, h] - far, acc)
    t = bucket.shape[-1]
    out_ref[:, :, 0:t] = acc * LOG2E
    out_ref[:, :, t:2 * t] = acc * LOG2E


def _rel_bias_tiles(rel_bias):
    t = ATTN_TILE
    assert t > REL_MAX_DIST
    heads = rel_bias.shape[1]
    key = np.arange(t, dtype=np.int64)[:, None]
    query = np.arange(t, dtype=np.int64)[None, :]
    bucket = jnp.asarray(np.stack([_rel_bucket(key - t - query), _rel_bucket(key - query)]))
    kern = functools.partial(_rel_bias_kernel, far_bucket=REL_BUCKETS // 2 - 1)
    return pl.pallas_call(
        kern,
        grid=(heads,),
        in_specs=[pl.BlockSpec(memory_space=pltpu.SMEM), pl.BlockSpec((2, t, t), lambda h: (0, 0, 0))],
        out_specs=pl.BlockSpec((None, 2, t, 2 * t), lambda h: (h, 0, 0, 0)),
        out_shape=jax.ShapeDtypeStruct((heads, 2, t, 2 * t), F32),
        name="rel_bias",
    )(rel_bias, bucket)


def _attn_schedule(nq, heads):
    tiles = [(h, i, j) for h in range(heads) for i in reversed(range(nq)) for j in range(i + 1)]
    events, slot_of, free, n_slots = [], {}, [], 0
    pv_pos = 0
    for pos in range(2 * len(tiles)):
        if pos < len(tiles):
            h, i, j = tiles[pos]
            if (h, i) not in slot_of:
                if not free:
                    free.append(n_slots)
                    n_slots += 1
                slot_of[h, i] = free.pop(0)
            events.append(("score", h, i, j, slot_of[h, i]))
        if pv_pos < len(tiles):
            h, i, j = tiles[pv_pos]
            scored = min(pos + 1, len(tiles))
            if scored >= min(tiles.index((h, i, i)) + 1 + ATTN_LEAD, len(tiles)):
                events.append(("pv", h, i, j, slot_of[h, i]))
                pv_pos += 1
                if j == i:
                    free.append(slot_of[h, i])
    assert pv_pos == len(tiles)
    return events, n_slots


def _diff_attn_kernel(base_ref, lam_ref, q_ref, k_ref, v_ref, nb_ref, sg_ref, o_ref, s_ref, vt_ref, *, lambda_init):
    seq = q_ref.shape[0]
    hd = q_ref.shape[1] // ATTN_HEADS
    t = ATTN_TILE
    nq = seq // t
    events, n_slots = _attn_schedule(nq, ATTN_HEADS)
    assert s_ref.shape[0] == n_slots
    base = base_ref[0]
    lane = lax.broadcasted_iota(jnp.int32, (t, hd), 1)
    key_chunk = lax.broadcasted_iota(jnp.int32, (t, 2 * t), 0) // CHUNK
    query_chunk = (lax.broadcasted_iota(jnp.int32, (t, 2 * t), 1) % t) // CHUNK
    allowed = key_chunk <= query_chunk
    lam = (jnp.exp(jnp.sum(lam_ref[0:1, :] * lam_ref[1:2, :], axis=-1, keepdims=True))
           - jnp.exp(jnp.sum(lam_ref[2:3, :] * lam_ref[3:4, :], axis=-1, keepdims=True)) + lambda_init)

    for h in range(ATTN_HEADS):
        for j in range(nq):
            v = v_ref[j * t:(j + 1) * t, h * hd:(h + 1) * hd]
            vt_ref[h, 0:hd, j * t:(j + 1) * t] = v.astype(F32).T.astype(BF16)
        vt_ref[h, hd:, :] = jnp.ones((ONES_ROWS, seq), BF16)

    def stacked_q(h, i):
        q = q_ref[i * t:(i + 1) * t, h * hd:(h + 1) * hd]
        zero = jnp.zeros_like(q)
        return jnp.concatenate([jnp.where(lane < hd // 2, q, zero), jnp.where(lane >= hd // 2, q, zero)], axis=0)

    def score_tile(h, i, j, slot, qq):
        s = _dot_nt(k_ref[j * t:(j + 1) * t, h * hd:(h + 1) * hd], qq)
        if j == i:
            s = jnp.where(allowed, s + nb_ref[h, 1], NEG_INF)
        elif j == i - 1:
            s = s + nb_ref[h, 0]
        s_ref[base + slot, j] = s
        return jnp.max(s, axis=0, keepdims=True)

    def finish(h, i, acc):
        o_maps = acc[:hd] / acc[hd:hd + 1]
        o = (o_maps[:, :t] - lam * o_maps[:, t:]).T
        o = _rms(o, sg_ref[...]) * (1.0 - lambda_init)
        o_ref[i * t:(i + 1) * t, h * hd:(h + 1) * hd] = o.astype(o_ref.dtype)

    qq, mx, acc = {}, {}, {}
    for kind, h, i, j, slot in events:
        if kind == "score":
            if j == 0:
                qq[h, i] = stacked_q(h, i)
            tile_max = score_tile(h, i, j, slot, qq[h, i])
            mx[h, i] = tile_max if j == 0 else jnp.maximum(mx[h, i], tile_max)
        else:
            p = jnp.exp2(s_ref[base + slot, j] - mx[h, i]).astype(BF16)
            pv = _dot(vt_ref[h, :, j * t:(j + 1) * t], p)
            acc[h, i] = pv if j == 0 else acc[h, i] + pv
            if j == i:
                finish(h, i, acc.pop((h, i)))


def _diff_attn(lam_vecs, q, k, v, near_bias, subln_g, *, batch, seq, lambda_init):
    n, width = q.shape
    t = ATTN_TILE
    hd = width // DIFF_HEADS
    nq = seq // t
    q3, k3, v3 = (a.reshape(batch, seq, width) for a in (q, k, v))
    head_block = pl.BlockSpec((None, seq, ATTN_HEADS * hd), lambda g, b: (b, 0, g))
    n_slots = _attn_schedule(nq, ATTN_HEADS)[1]
    kern = functools.partial(_diff_attn_kernel, lambda_init=lambda_init)
    out = pl.pallas_call(
        kern,
        grid=(DIFF_HEADS // ATTN_HEADS, batch),
        in_specs=[
            pl.BlockSpec(memory_space=pltpu.SMEM),
            pl.BlockSpec(lam_vecs.shape, lambda g, b: (0, 0)),
            head_block, head_block, head_block,
            pl.BlockSpec((ATTN_HEADS, 2, t, 2 * t), lambda g, b: (g, 0, 0, 0)),
            pl.BlockSpec(subln_g.shape, lambda g, b: (0, 0)),
        ],
        out_specs=head_block,
        out_shape=jax.ShapeDtypeStruct((batch, seq, width), BF16),
        scratch_shapes=[pltpu.VMEM((n_slots, nq, t, 2 * t), F32),
                        pltpu.VMEM((ATTN_HEADS, hd + ONES_ROWS, seq), BF16)],
        compiler_params=pltpu.CompilerParams(dimension_semantics=("arbitrary",) * 2, vmem_limit_bytes=VMEM_LIMIT),
        name="diff_attn",
    )(jnp.zeros((1,), jnp.int32), lam_vecs, q3, k3, v3, near_bias, subln_g)
    return out.reshape(n, width)


def _mem_kv_kernel(mem_ref, g_ref, w_ref, kv_ref):
    mn = _rms(mem_ref[...], g_ref[...]).astype(BF16)
    kv_ref[...] = _dot(mn, w_ref[...]).astype(BF16)


def _mem_kv(mem2, g, w, *, mem_len):
    n, d = mem2.shape
    return pl.pallas_call(
        _mem_kv_kernel,
        grid=(n // mem_len,),
        in_specs=[pl.BlockSpec((mem_len, d), lambda i: (i, 0)), _resident(g.shape), _resident(w.shape)],
        out_specs=pl.BlockSpec((mem_len, w.shape[1]), lambda i: (i, 0)),
        out_shape=jax.ShapeDtypeStruct((n, w.shape[1]), BF16),
        compiler_params=pltpu.CompilerParams(dimension_semantics=("arbitrary",), vmem_limit_bytes=VMEM_LIMIT),
        name="mem_kv",
    )(mem2, g, w)


def _merge_cross_kernel(x_ref, o_ref, gaya_ref, gb_ref, wb_ref, wmo_ref, g_ref, wq_ref, ck_ref, cv_ref, wo_ref,
                        out_ref, *, q_scale):
    tm, d = x_ref.shape
    y_b = _dot(o_ref[...], wb_ref[...].astype(BF16))
    merged = gaya_ref[...].astype(F32) + gb_ref[...].astype(F32) * y_b
    h = x_ref[...] + _dot(merged.astype(BF16), wmo_ref[...].astype(BF16))

    hn = _rms(h, g_ref[...]).astype(BF16)
    cq = (_dot(hn, wq_ref[...].astype(BF16)) * q_scale).astype(BF16)
    hd = d // CROSS_HEADS
    cols = [slice(a * hd, (a + 1) * hd) for a in range(CROSS_HEADS)]
    scores = [_dot_nt(cq[:, c], ck_ref[:, c]) for c in cols]
    probs = [jnp.exp2(s - jnp.max(s, axis=-1, keepdims=True)) for s in scores]
    heads = [_dot(p.astype(BF16), cv_ref[:, c]) / jnp.sum(p, axis=-1, keepdims=True) for p, c in zip(probs, cols)]
    o = jnp.concatenate(heads, axis=-1).astype(BF16)
    out_ref[...] = h + _dot(o, wo_ref[...].astype(BF16))


def _merge_cross(x2, o, gaya, gb, wb, wmo, g, wq, ckv, wo, *, seq, mem_len):
    n, d = x2.shape
    tm = TOKEN_TILE
    tiles_per_seq = seq // tm
    row = lambda c: pl.BlockSpec((tm, c), lambda i: (i, 0))
    kern = functools.partial(_merge_cross_kernel, q_scale=(d // CROSS_HEADS) ** -0.5 * LOG2E)
    return pl.pallas_call(
        kern,
        grid=(n // tm,),
        in_specs=[row(d), row(o.shape[1]), row(d), row(d), _resident(wb.shape), _resident(wmo.shape),
                  _resident(g.shape), _resident(wq.shape),
                  pl.BlockSpec((mem_len, d), lambda i: (i // tiles_per_seq, 0)),
                  pl.BlockSpec((mem_len, d), lambda i: (i // tiles_per_seq, 1)),
                  _resident(wo.shape)],
        out_specs=row(d),
        out_shape=jax.ShapeDtypeStruct((n, d), F32),
        compiler_params=pltpu.CompilerParams(dimension_semantics=("arbitrary",), vmem_limit_bytes=VMEM_LIMIT),
        name="merge_cross",
    )(x2, o, gaya, gb, wb, wmo, g, wq, ckv, ckv, wo)


def _mlp_kernel(h_ref, g_ref, w1_ref, w2_ref, gf_ref, out_ref, *, final_norm):
    h = h_ref[...]
    hn = _rms(h, g_ref[...]).astype(BF16)
    acc = h
    for c in range(w1_ref.shape[1] // FF_CHUNK):
        cols = slice(c * FF_CHUNK, (c + 1) * FF_CHUNK)
        a = jnp.maximum(_dot(hn, w1_ref[:, cols].astype(BF16)), 0.0)
        acc = acc + _dot((a * a).astype(BF16), w2_ref[cols, :].astype(BF16))
    out_ref[...] = _rms(acc, gf_ref[...]) if final_norm else acc


def _mlp(h, g, w1, w2, gf, *, final_norm):
    n, d = h.shape
    tm = MLP_TILE
    row = pl.BlockSpec((tm, d), lambda i: (i, 0))
    kern = functools.partial(_mlp_kernel, final_norm=final_norm)
    return pl.pallas_call(
        kern,
        grid=(n // tm,),
        in_specs=[row, _resident(g.shape), _resident(w1.shape), _resident(w2.shape), _resident(gf.shape)],
        out_specs=row,
        out_shape=jax.ShapeDtypeStruct((n, d), F32),
        compiler_params=pltpu.CompilerParams(dimension_semantics=("arbitrary",), vmem_limit_bytes=VMEM_LIMIT),
        name="mlp",
    )(h, g, w1, w2, gf)


def kernel(x, mem, norm_mix_g, w_in, w_pool_group, pool_scale, w_a_proj, lambda_q1, lambda_k1, lambda_q2, lambda_k2, subln_g, rel_bias, w_b_proj, w_gate, b_gate, w_out, norm_cross_g, norm_mem_g, w_cq, w_ckv, w_co, norm_mlp_g, w_ff1, w_ff2, final_norm_g):
    batch, seq, d = x.shape
    mem_len = mem.shape[1]
    depth = w_in.shape[0]
    pool_width = w_a_proj.shape[1]
    assert pool_width == d and seq % TOKEN_TILE == 0 and seq % ATTN_TILE == 0
    row = lambda a: a.reshape(1, -1).astype(F32)
    bf = lambda a: a.astype(BF16)

    near_bias = _rel_bias_tiles(rel_bias.astype(F32))
    h = x.reshape(batch * seq, d)
    mem2 = mem.reshape(batch * mem_len, d)
    for l in range(depth):
        lambda_init = LAMBDA_INIT_BASE - LAMBDA_INIT_AMP * math.exp(-LAMBDA_INIT_RATE * l)
        q, k, v, gaya, gb = _mixer_front(
            h, row(norm_mix_g[l]), w_in[l], w_gate[l], row(b_gate[l]), w_pool_group[l], row(pool_scale[l]), w_a_proj[l],
            seq=seq, q_scale=DIFF_HEAD_DIM ** -0.5 * LOG2E)
        lam_vecs = jnp.stack([lambda_q1[l], lambda_k1[l], lambda_q2[l], lambda_k2[l]]).astype(F32)
        o = _diff_attn(lam_vecs, q, k, v, near_bias, row(subln_g[l]), batch=batch, seq=seq, lambda_init=lambda_init)
        ckv = _mem_kv(mem2, row(norm_mem_g[l]), bf(w_ckv[l]), mem_len=mem_len)
        h = _merge_cross(h, o, gaya, gb, w_b_proj[l], w_out[l], row(norm_cross_g[l]), w_cq[l], ckv, w_co[l],
                         seq=seq, mem_len=mem_len)
        h = _mlp(h, row(norm_mlp_g[l]), w_ff1[l], w_ff2[l], row(final_norm_g), final_norm=(l == depth - 1))
    return h.reshape(batch, seq, d)
```

```python
import functools
import math

import jax
import jax.numpy as jnp
import numpy as np
from jax import lax
from jax.experimental import pallas as pl
from jax.experimental.pallas import tpu as pltpu

CHUNK = 64
POOL_WINDOWS = (2, 4, 8, 16)
DIFF_HEADS = 8
DIFF_HEAD_DIM = 64
LAMBDA_INIT_BASE = 0.8
LAMBDA_INIT_AMP = 0.6
LAMBDA_INIT_RATE = 0.3
REL_BUCKETS = 32
REL_MAX_DIST = 128
CROSS_HEADS = 4
EPS = 1e-6
NEG_INF = -1e30
LOG2E = math.log2(math.e)

POOL_HALO = 16
TOKEN_TILE = 512
MLP_TILE = 512
ATTN_TILE = 256
ATTN_HEADS = 2
ONES_ROWS = 16
ATTN_LEAD = 2
FF_CHUNK = 1024
VMEM_LIMIT = 56 * 1024 * 1024

BF16 = jnp.bfloat16
F32 = jnp.float32


def _resident(shape):
    n = len(shape)
    return pl.BlockSpec(shape, lambda *_: (0,) * n, pipeline_mode=pl.Buffered(1))


def _rms(x, g):
    return x * lax.rsqrt(jnp.mean(x * x, axis=-1, keepdims=True) + EPS) * g


def _dot(a, b):
    return jnp.dot(a, b, preferred_element_type=F32)


def _dot_nt(a, b):
    return lax.dot_general(a, b, (((1,), (1,)), ((), ())), preferred_element_type=F32)


def _mixer_front_kernel(x_ref, g_ref, win_ref, wg_ref, bg_ref, wp_ref, ps_ref, wa_ref,
                        q_ref, k_ref, v_ref, gaya_ref, gb_ref, ubuf_ref, ga_ref, *, tiles_per_seq, q_scale):
    tm, d = x_ref.shape
    i = pl.program_id(0)
    tile_in_seq = i % tiles_per_seq
    xn = _rms(x_ref[...], g_ref[...]).astype(BF16)

    @pl.when(tile_in_seq == 0)
    def _():
        ubuf_ref[0:POOL_HALO, :] = jnp.zeros((POOL_HALO, d), F32)

    def w_in(lo, hi):
        return win_ref[:, lo:hi].astype(BF16)

    ubuf_ref[POOL_HALO:POOL_HALO + tm, :] = _dot(xn, w_in(0, d))

    width = q_ref.shape[1]
    q_ref[...] = (_dot(xn, w_in(d, d + width)) * q_scale).astype(BF16)
    k_ref[...] = _dot(xn, w_in(d + width, d + 2 * width)).astype(BF16)
    v_ref[...] = _dot(xn, w_in(d + 2 * width, d + 3 * width)).astype(BF16)
    z = _dot(xn, wg_ref[...].astype(BF16)) + bg_ref[...]
    gates = 1.0 / (1.0 + jnp.exp(-z))
    ga_ref[...] = gates[:, :d]
    gb_ref[...] = gates[:, d:].astype(gb_ref.dtype)

    pos = tile_in_seq * tm + lax.broadcasted_iota(jnp.int32, (tm, 1), 0)
    gd = d // len(POOL_WINDOWS)
    mapped = []
    for g, w in enumerate(POOL_WINDOWS):
        cols = slice(g * gd, (g + 1) * gd)
        u_g = ubuf_ref[POOL_HALO:POOL_HALO + tm, cols]
        acc = u_g
        for j in range(1, w):
            acc = acc + ubuf_ref[POOL_HALO - j:POOL_HALO - j + tm, cols]
        inv_count = 1.0 / jnp.minimum(pos + 1, w).astype(F32)
        pooled = acc * inv_count - u_g
        mapped.append(_dot(pooled.astype(BF16), wp_ref[g].astype(BF16)))
    ubuf_ref[0:POOL_HALO, :] = ubuf_ref[tm:tm + POOL_HALO, :]
    y = jnp.concatenate(mapped, axis=-1) * ps_ref[...]
    gaya_ref[...] = (ga_ref[...] * _dot(y.astype(BF16), wa_ref[...].astype(BF16))).astype(gaya_ref.dtype)


def _mixer_front(x2, g, win, wg, bg, wp, ps, wa, *, seq, q_scale):
    n, d = x2.shape
    tm = TOKEN_TILE
    width = (win.shape[1] - d) // 3
    row = lambda c: pl.BlockSpec((tm, c), lambda i: (i, 0))
    kern = functools.partial(_mixer_front_kernel, tiles_per_seq=seq // tm, q_scale=q_scale)
    return pl.pallas_call(
        kern,
        grid=(n // tm,),
        in_specs=[row(d), _resident(g.shape), _resident(win.shape), _resident(wg.shape), _resident(bg.shape),
                  _resident(wp.shape), _resident(ps.shape), _resident(wa.shape)],
        out_specs=[row(width), row(width), row(width), row(d), row(d)],
        out_shape=[jax.ShapeDtypeStruct((n, width), BF16)] * 3 + [jax.ShapeDtypeStruct((n, d), BF16)] * 2,
        scratch_shapes=[pltpu.VMEM((POOL_HALO + tm, d), F32), pltpu.VMEM((tm, d), F32)],
        compiler_params=pltpu.CompilerParams(dimension_semantics=("arbitrary",), vmem_limit_bytes=VMEM_LIMIT),
        name="mixer_front",
    )(x2, g, win, wg, bg, wp, ps, wa)


def _rel_bucket(rel):
    nb = REL_BUCKETS // 2
    max_exact = nb // 2
    n = np.abs(rel)
    scaled = np.log(np.maximum(n, 1) / max_exact) / math.log(REL_MAX_DIST / max_exact) * (nb - max_exact)
    nearest = np.rint(scaled)
    scaled = np.where(np.abs(scaled - nearest) < 1e-9, nearest, scaled)
    large = np.minimum(max_exact + np.floor(scaled).astype(np.int64), nb - 1)
    return ((rel > 0) * nb + np.where(n < max_exact, n, large)).astype(np.int32)


def _rel_bias_kernel(table_ref, bucket_ref, out_ref, *, far_bucket):
    h = pl.program_id(0)
    bucket = bucket_ref[...]
    far = table_ref[far_bucket, h]
    acc = jnp.zeros(bucket.shape, F32)
    for b in range(REL_BUCKETS):
        acc = jnp.where(bucket == b, table_ref[b, h] - far, acc)
    t = bucket.shape[-1]
    half = t // 2
    bias = acc * LOG2E
    for part in range(4):
        src = (part // 2) * half
        out_ref[:, :, part * half:(part + 1) * half] = bias[:, :, src:src + half]


def _rel_bias_tiles(rel_bias):
    t = ATTN_TILE
    assert t > REL_MAX_DIST
    heads = rel_bias.shape[1]
    key = np.arange(t, dtype=np.int64)[:, None]
    query = np.arange(t, dtype=np.int64)[None, :]
    bucket = jnp.asarray(np.stack([_rel_bucket(key - t - query), _rel_bucket(key - query)]))
    kern = functools.partial(_rel_bias_kernel, far_bucket=REL_BUCKETS // 2 - 1)
    return pl.pallas_call(
        kern,
        grid=(heads,),
        in_specs=[pl.BlockSpec(memory_space=pltpu.SMEM), pl.BlockSpec((2, t, t), lambda h: (0, 0, 0))],
        out_specs=pl.BlockSpec((None, 2, t, 2 * t), lambda h: (h, 0, 0, 0)),
        out_shape=jax.ShapeDtypeStruct((heads, 2, t, 2 * t), F32),
        name="rel_bias",
    )(rel_bias, bucket)


def _attn_schedule(nq, heads):
    tiles = [(h, i, j) for h in range(heads) for i in reversed(range(nq)) for j in range(i + 1)]
    events, slot_of, free, n_slots = [], {}, [], 0
    pv_pos = 0
    for pos in range(2 * len(tiles)):
        if pos < len(tiles):
            h, i, j = tiles[pos]
            if (h, i) not in slot_of:
                if not free:
                    free.append(n_slots)
                    n_slots += 1
                slot_of[h, i] = free.pop(0)
            events.append(("score", h, i, j, slot_of[h, i]))
        if pv_pos < len(tiles):
            h, i, j = tiles[pv_pos]
            scored = min(pos + 1, len(tiles))
            if scored >= min(tiles.index((h, i, i)) + 1 + ATTN_LEAD, len(tiles)):
                events.append(("pv", h, i, j, slot_of[h, i]))
                pv_pos += 1
                if j == i:
                    free.append(slot_of[h, i])
    assert pv_pos == len(tiles)
    return events, n_slots


def _diff_attn_kernel(base_ref, lam_ref, q_ref, k_ref, v_ref, nb_ref, sg_ref, o_ref, s_ref, vt_ref, *, lambda_init):
    seq = q_ref.shape[0]
    hd = q_ref.shape[1] // ATTN_HEADS
    t = ATTN_TILE
    nq = seq // t
    events, n_slots = _attn_schedule(nq, ATTN_HEADS)
    assert s_ref.shape[0] == n_slots
    base = base_ref[0]
    half = t // 2
    lane = lax.broadcasted_iota(jnp.int32, (half, hd), 1)
    key_pos = lax.broadcasted_iota(jnp.int32, (t, 2 * t), 0)
    col = lax.broadcasted_iota(jnp.int32, (t, 2 * t), 1)
    query_pos = (col // t) * half + col % half
    allowed = key_pos // CHUNK <= query_pos // CHUNK
    lam = (jnp.exp(jnp.sum(lam_ref[0:1, :] * lam_ref[1:2, :], axis=-1, keepdims=True))
           - jnp.exp(jnp.sum(lam_ref[2:3, :] * lam_ref[3:4, :], axis=-1, keepdims=True)) + lambda_init)

    for h in range(ATTN_HEADS):
        for j in range(nq):
            v = v_ref[j * t:(j + 1) * t, h * hd:(h + 1) * hd]
            vt_ref[h, 0:hd, j * t:(j + 1) * t] = v.astype(F32).T.astype(BF16)
        vt_ref[h, hd:, :] = jnp.ones((ONES_ROWS, seq), BF16)

    def stacked_q(h, i):
        parts = []
        for part in range(2):
            q = q_ref[i * t + part * half:i * t + (part + 1) * half, h * hd:(h + 1) * hd]
            zero = jnp.zeros_like(q)
            parts += [jnp.where(lane < hd // 2, q, zero), jnp.where(lane >= hd // 2, q, zero)]
        return jnp.concatenate(parts, axis=0)

    def score_tile(h, i, j, slot, qq):
        heads = slice(h * hd, (h + 1) * hd)
        if j < i:
            s = _dot_nt(k_ref[j * t:(j + 1) * t, heads], qq)
            if j == i - 1:
                s = s + nb_ref[h, 0]
            s_ref[base + slot, j] = s
            return jnp.max(s, axis=0, keepdims=True)
        s_top = _dot_nt(k_ref[j * t:j * t + half, heads], qq)
        s_top = jnp.where(allowed[:half], s_top + nb_ref[h, 1, 0:half, :], NEG_INF)
        s_bot = _dot_nt(k_ref[j * t + half:(j + 1) * t, heads], qq[t:])
        s_bot = jnp.where(allowed[half:, t:], s_bot + nb_ref[h, 1, half:t, t:2 * t], NEG_INF)
        s_ref[base + slot, j, 0:half, :] = s_top
        s_ref[base + slot, j, half:t, t:2 * t] = s_bot
        m_top = jnp.max(s_top, axis=0, keepdims=True)
        m_bot = jnp.max(s_bot, axis=0, keepdims=True)
        return jnp.concatenate([m_top[:, :t], jnp.maximum(m_top[:, t:], m_bot)], axis=1)

    def prob_tile(h, i, j, slot):
        if j < i:
            return jnp.exp2(s_ref[base + slot, j] - mx[h, i]).astype(BF16)
        p_top = jnp.exp2(s_ref[base + slot, j, 0:half, :] - mx[h, i]).astype(BF16)
        p_bot = jnp.exp2(s_ref[base + slot, j, half:t, t:2 * t] - mx[h, i][:, t:]).astype(BF16)
        p_bot = jnp.concatenate([jnp.zeros((half, t), BF16), p_bot], axis=1)
        return jnp.concatenate([p_top, p_bot], axis=0)

    def finish(h, i, acc):
        o_maps = acc[:hd] / acc[hd:hd + 1]
        o = jnp.concatenate([o_maps[:, part * t:part * t + half] - lam * o_maps[:, part * t + half:(part + 1) * t]
                             for part in range(2)], axis=1).T
        o = _rms(o, sg_ref[...]) * (1.0 - lambda_init)
        o_ref[i * t:(i + 1) * t, h * hd:(h + 1) * hd] = o.astype(o_ref.dtype)

    qq, mx, acc = {}, {}, {}
    for kind, h, i, j, slot in events:
        if kind == "score":
            if j == 0:
                qq[h, i] = stacked_q(h, i)
            tile_max = score_tile(h, i, j, slot, qq[h, i])
            mx[h, i] = tile_max if j == 0 else jnp.maximum(mx[h, i], tile_max)
        else:
            pv = _dot(vt_ref[h, :, j * t:(j + 1) * t], prob_tile(h, i, j, slot))
            acc[h, i] = pv if j == 0 else acc[h, i] + pv
            if j == i:
                finish(h, i, acc.pop((h, i)))


def _diff_attn(lam_vecs, q, k, v, near_bias, subln_g, *, batch, seq, lambda_init):
    n, width = q.shape
    t = ATTN_TILE
    hd = width // DIFF_HEADS
    nq = seq // t
    q3, k3, v3 = (a.reshape(batch, seq, width) for a in (q, k, v))
    head_block = pl.BlockSpec((None, seq, ATTN_HEADS * hd), lambda g, b: (b, 0, g))
    n_slots = _attn_schedule(nq, ATTN_HEADS)[1]
    kern = functools.partial(_diff_attn_kernel, lambda_init=lambda_init)
    out = pl.pallas_call(
        kern,
        grid=(DIFF_HEADS // ATTN_HEADS, batch),
        in_specs=[
            pl.BlockSpec(memory_space=pltpu.SMEM),
            pl.BlockSpec(lam_vecs.shape, lambda g, b: (0, 0)),
            head_block, head_block, head_block,
            pl.BlockSpec((ATTN_HEADS, 2, t, 2 * t), lambda g, b: (g, 0, 0, 0)),
            pl.BlockSpec(subln_g.shape, lambda g, b: (0, 0)),
        ],
        out_specs=head_block,
        out_shape=jax.ShapeDtypeStruct((batch, seq, width), BF16),
        scratch_shapes=[pltpu.VMEM((n_slots, nq, t, 2 * t), F32),
                        pltpu.VMEM((ATTN_HEADS, hd + ONES_ROWS, seq), BF16)],
        compiler_params=pltpu.CompilerParams(dimension_semantics=("arbitrary",) * 2, vmem_limit_bytes=VMEM_LIMIT),
        name="diff_attn",
    )(jnp.zeros((1,), jnp.int32), lam_vecs, q3, k3, v3, near_bias, subln_g)
    return out.reshape(n, width)


def _mem_kv_kernel(mem_ref, g_ref, w_ref, kv_ref):
    mn = _rms(mem_ref[...], g_ref[...]).astype(BF16)
    kv_ref[...] = _dot(mn, w_ref[...]).astype(BF16)


def _mem_kv(mem2, g, w, *, mem_len):
    n, d = mem2.shape
    return pl.pallas_call(
        _mem_kv_kernel,
        grid=(n // mem_len,),
        in_specs=[pl.BlockSpec((mem_len, d), lambda i: (i, 0)), _resident(g.shape), _resident(w.shape)],
        out_specs=pl.BlockSpec((mem_len, w.shape[1]), lambda i: (i, 0)),
        out_shape=jax.ShapeDtypeStruct((n, w.shape[1]), BF16),
        compiler_params=pltpu.CompilerParams(dimension_semantics=("arbitrary",), vmem_limit_bytes=VMEM_LIMIT),
        name="mem_kv",
    )(mem2, g, w)


def _merge_cross_kernel(x_ref, o_ref, gaya_ref, gb_ref, wb_ref, wmo_ref, g_ref, wq_ref, ck_ref, cv_ref, wo_ref,
                        out_ref, *, q_scale):
    tm, d = x_ref.shape
    y_b = _dot(o_ref[...], wb_ref[...].astype(BF16))
    merged = gaya_ref[...].astype(F32) + gb_ref[...].astype(F32) * y_b
    h = x_ref[...] + _dot(merged.astype(BF16), wmo_ref[...].astype(BF16))

    hn = _rms(h, g_ref[...]).astype(BF16)
    cq = (_dot(hn, wq_ref[...].astype(BF16)) * q_scale).astype(BF16)
    hd = d // CROSS_HEADS
    cols = [slice(a * hd, (a + 1) * hd) for a in range(CROSS_HEADS)]
    scores = [_dot_nt(cq[:, c], ck_ref[:, c]) for c in cols]
    probs = [jnp.exp2(s - jnp.max(s, axis=-1, keepdims=True)) for s in scores]
    heads = [_dot(p.astype(BF16), cv_ref[:, c]) / jnp.sum(p, axis=-1, keepdims=True) for p, c in zip(probs, cols)]
    o = jnp.concatenate(heads, axis=-1).astype(BF16)
    out_ref[...] = h + _dot(o, wo_ref[...].astype(BF16))


def _merge_cross(x2, o, gaya, gb, wb, wmo, g, wq, ckv, wo, *, seq, mem_len):
    n, d = x2.shape
    tm = TOKEN_TILE
    tiles_per_seq = seq // tm
    row = lambda c: pl.BlockSpec((tm, c), lambda i: (i, 0))
    kern = functools.partial(_merge_cross_kernel, q_scale=(d // CROSS_HEADS) ** -0.5 * LOG2E)
    return pl.pallas_call(
        kern,
        grid=(n // tm,),
        in_specs=[row(d), row(o.shape[1]), row(d), row(d), _resident(wb.shape), _resident(wmo.shape),
                  _resident(g.shape), _resident(wq.shape),
                  pl.BlockSpec((mem_len, d), lambda i: (i // tiles_per_seq, 0)),
                  pl.BlockSpec((mem_len, d), lambda i: (i // tiles_per_seq, 1)),
                  _resident(wo.shape)],
        out_specs=row(d),
        out_shape=jax.ShapeDtypeStruct((n, d), F32),
        compiler_params=pltpu.CompilerParams(dimension_semantics=("arbitrary",), vmem_limit_bytes=VMEM_LIMIT),
        name="merge_cross",
    )(x2, o, gaya, gb, wb, wmo, g, wq, ckv, ckv, wo)


def _mlp_kernel(h_ref, g_ref, w1_ref, w2_ref, gf_ref, out_ref, *, final_norm):
    h = h_ref[...]
    hn = _rms(h, g_ref[...]).astype(BF16)
    acc = h
    for c in range(w1_ref.shape[1] // FF_CHUNK):
        cols = slice(c * FF_CHUNK, (c + 1) * FF_CHUNK)
        a = jnp.maximum(_dot(hn, w1_ref[:, cols].astype(BF16)), 0.0)
        acc = acc + _dot((a * a).astype(BF16), w2_ref[cols, :].astype(BF16))
    out_ref[...] = _rms(acc, gf_ref[...]) if final_norm else acc


def _mlp(h, g, w1, w2, gf, *, final_norm):
    n, d = h.shape
    tm = MLP_TILE
    row = pl.BlockSpec((tm, d), lambda i: (i, 0))
    kern = functools.partial(_mlp_kernel, final_norm=final_norm)
    return pl.pallas_call(
        kern,
        grid=(n // tm,),
        in_specs=[row, _resident(g.shape), _resident(w1.shape), _resident(w2.shape), _resident(gf.shape)],
        out_specs=row,
        out_shape=jax.ShapeDtypeStruct((n, d), F32),
        compiler_params=pltpu.CompilerParams(dimension_semantics=("arbitrary",), vmem_limit_bytes=VMEM_LIMIT),
        name="mlp",
    )(h, g, w1, w2, gf)


def kernel(x, mem, norm_mix_g, w_in, w_pool_group, pool_scale, w_a_proj, lambda_q1, lambda_k1, lambda_q2, lambda_k2, subln_g, rel_bias, w_b_proj, w_gate, b_gate, w_out, norm_cross_g, norm_mem_g, w_cq, w_ckv, w_co, norm_mlp_g, w_ff1, w_ff2, final_norm_g):
    batch, seq, d = x.shape
    mem_len = mem.shape[1]
    depth = w_in.shape[0]
    pool_width = w_a_proj.shape[1]
    assert pool_width == d and seq % TOKEN_TILE == 0 and seq % ATTN_TILE == 0
    row = lambda a: a.reshape(1, -1).astype(F32)
    bf = lambda a: a.astype(BF16)

    near_bias = _rel_bias_tiles(rel_bias.astype(F32))
    h = x.reshape(batch * seq, d)
    mem2 = mem.reshape(batch * mem_len, d)
    for l in range(depth):
        lambda_init = LAMBDA_INIT_BASE - LAMBDA_INIT_AMP * math.exp(-LAMBDA_INIT_RATE * l)
        q, k, v, gaya, gb = _mixer_front(
            h, row(norm_mix_g[l]), w_in[l], w_gate[l], row(b_gate[l]), w_pool_group[l], row(pool_scale[l]), w_a_proj[l],
            seq=seq, q_scale=DIFF_HEAD_DIM ** -0.5 * LOG2E)
        lam_vecs = jnp.stack([lambda_q1[l], lambda_k1[l], lambda_q2[l], lambda_k2[l]]).astype(F32)
        o = _diff_attn(lam_vecs, q, k, v, near_bias, row(subln_g[l]), batch=batch, seq=seq, lambda_init=lambda_init)
        ckv = _mem_kv(mem2, row(norm_mem_g[l]), bf(w_ckv[l]), mem_len=mem_len)
        h = _merge_cross(h, o, gaya, gb, w_b_proj[l], w_out[l], row(norm_cross_g[l]), w_cq[l], ckv, w_co[l],
                         seq=seq, mem_len=mem_len)
        h = _mlp(h, row(norm_mlp_g[l]), w_ff1[l], w_ff2[l], row(final_norm_g), final_norm=(l == depth - 1))
    return h.reshape(batch, seq, d)
```

```python
import functools
import math

import jax
import jax.numpy as jnp
import numpy as np
from jax import lax
from jax.experimental import pallas as pl
from jax.experimental.pallas import tpu as pltpu

CHUNK = 64
POOL_WINDOWS = (2, 4, 8, 16)
DIFF_HEADS = 8
DIFF_HEAD_DIM = 64
LAMBDA_INIT_BASE = 0.8
LAMBDA_INIT_AMP = 0.6
LAMBDA_INIT_RATE = 0.3
REL_BUCKETS = 32
REL_MAX_DIST = 128
CROSS_HEADS = 4
EPS = 1e-6
NEG_INF = -1e30
LOG2E = math.log2(math.e)

POOL_HALO = 16
TOKEN_TILE = 512
MLP_TILE = 512
ATTN_TILE = 256
ATTN_HEADS = 2
ONES_ROWS = 16
ATTN_LEAD = 2
FF_CHUNK = 1024
VMEM_LIMIT = 56 * 1024 * 1024

BF16 = jnp.bfloat16
F32 = jnp.float32


def _resident(shape):
    n = len(shape)
    return pl.BlockSpec(shape, lambda *_: (0,) * n, pipeline_mode=pl.Buffered(1))


def _rms(x, g):
    return x * lax.rsqrt(jnp.mean(x * x, axis=-1, keepdims=True) + EPS) * g


def _dot(a, b):
    return jnp.dot(a, b, preferred_element_type=F32)


def _dot_nt(a, b):
    return lax.dot_general(a, b, (((1,), (1,)), ((), ())), preferred_element_type=F32)


def _mixer_front_kernel(x_ref, g_ref, win_ref, wg_ref, bg_ref, wp_ref, ps_ref, wa_ref,
                        q_ref, k_ref, v_ref, gaya_ref, gb_ref, ubuf_ref, ga_ref, *, tiles_per_seq, q_scale):
    tm, d = x_ref.shape
    i = pl.program_id(0)
    tile_in_seq = i % tiles_per_seq
    xn = _rms(x_ref[...], g_ref[...]).astype(BF16)

    @pl.when(tile_in_seq == 0)
    def _():
        ubuf_ref[0:POOL_HALO, :] = jnp.zeros((POOL_HALO, d), F32)

    def w_in(lo, hi):
        return win_ref[:, lo:hi].astype(BF16)

    ubuf_ref[POOL_HALO:POOL_HALO + tm, :] = _dot(xn, w_in(0, d))

    width = q_ref.shape[1]
    q_ref[...] = (_dot(xn, w_in(d, d + width)) * q_scale).astype(BF16)
    k_ref[...] = _dot(xn, w_in(d + width, d + 2 * width)).astype(BF16)
    v_ref[...] = _dot(xn, w_in(d + 2 * width, d + 3 * width)).astype(BF16)
    z = _dot(xn, wg_ref[...].astype(BF16)) + bg_ref[...]
    gates = 1.0 / (1.0 + jnp.exp(-z))
    ga_ref[...] = gates[:, :d]
    gb_ref[...] = gates[:, d:].astype(gb_ref.dtype)

    pos = tile_in_seq * tm + lax.broadcasted_iota(jnp.int32, (tm, 1), 0)
    gd = d // len(POOL_WINDOWS)
    mapped = []
    for g, w in enumerate(POOL_WINDOWS):
        cols = slice(g * gd, (g + 1) * gd)
        u_g = ubuf_ref[POOL_HALO:POOL_HALO + tm, cols]
        acc = u_g
        for j in range(1, w):
            acc = acc + ubuf_ref[POOL_HALO - j:POOL_HALO - j + tm, cols]
        inv_count = 1.0 / jnp.minimum(pos + 1, w).astype(F32)
        pooled = acc * inv_count - u_g
        mapped.append(_dot(pooled.astype(BF16), wp_ref[g].astype(BF16)))
    ubuf_ref[0:POOL_HALO, :] = ubuf_ref[tm:tm + POOL_HALO, :]
    y = jnp.concatenate(mapped, axis=-1) * ps_ref[...]
    gaya_ref[...] = (ga_ref[...] * _dot(y.astype(BF16), wa_ref[...].astype(BF16))).astype(gaya_ref.dtype)


def _mixer_front(x2, g, win, wg, bg, wp, ps, wa, *, seq, q_scale):
    n, d = x2.shape
    tm = TOKEN_TILE
    width = (win.shape[1] - d) // 3
    row = lambda c: pl.BlockSpec((tm, c), lambda i: (i, 0))
    kern = functools.partial(_mixer_front_kernel, tiles_per_seq=seq // tm, q_scale=q_scale)
    return pl.pallas_call(
        kern,
        grid=(n // tm,),
        in_specs=[row(d), _resident(g.shape), _resident(win.shape), _resident(wg.shape), _resident(bg.shape),
                  _resident(wp.shape), _resident(ps.shape), _resident(wa.shape)],
        out_specs=[row(width), row(width), row(width), row(d), row(d)],
        out_shape=[jax.ShapeDtypeStruct((n, width), BF16)] * 3 + [jax.ShapeDtypeStruct((n, d), BF16)] * 2,
        scratch_shapes=[pltpu.VMEM((POOL_HALO + tm, d), F32), pltpu.VMEM((tm, d), F32)],
        compiler_params=pltpu.CompilerParams(dimension_semantics=("arbitrary",), vmem_limit_bytes=VMEM_LIMIT),
        name="mixer_front",
    )(x2, g, win, wg, bg, wp, ps, wa)


def _rel_bucket(rel):
    nb = REL_BUCKETS // 2
    max_exact = nb // 2
    n = np.abs(rel)
    scaled = np.log(np.maximum(n, 1) / max_exact) / math.log(REL_MAX_DIST / max_exact) * (nb - max_exact)
    nearest = np.rint(scaled)
    scaled = np.where(np.abs(scaled - nearest) < 1e-9, nearest, scaled)
    large = np.minimum(max_exact + np.floor(scaled).astype(np.int64), nb - 1)
    return ((rel > 0) * nb + np.where(n < max_exact, n, large)).astype(np.int32)


def _rel_bias_kernel(table_ref, bucket_ref, out_ref, *, far_bucket):
    h = pl.program_id(0)
    bucket = bucket_ref[...]
    far = table_ref[far_bucket, h]
    acc = jnp.zeros(bucket.shape, F32)
    for b in range(REL_BUCKETS):
        acc = jnp.where(bucket == b, table_ref[b, h] - far, acc)
    t = bucket.shape[-1]
    half = t // 2
    bias = acc * LOG2E
    for part in range(4):
        src = (part // 2) * half
        out_ref[:, :, part * half:(part + 1) * half] = bias[:, :, src:src + half]


def _prev_bias_region():
    t = ATTN_TILE
    key = np.arange(t, dtype=np.int64)[:, None]
    query = np.arange(t, dtype=np.int64)[None, :]
    near = _rel_bucket(key - t - query) != REL_BUCKETS // 2 - 1
    first_row = int(np.nonzero(near.any(axis=1))[0].min()) // 8 * 8
    last_query = int(np.nonzero(near.any(axis=0))[0].max())
    n_cols = t if last_query < t // 2 else 2 * t
    return first_row, n_cols


def _rel_bias_tiles(rel_bias):
    t = ATTN_TILE
    assert t > REL_MAX_DIST
    heads = rel_bias.shape[1]
    key = np.arange(t, dtype=np.int64)[:, None]
    query = np.arange(t, dtype=np.int64)[None, :]
    bucket = jnp.asarray(np.stack([_rel_bucket(key - t - query), _rel_bucket(key - query)]))
    kern = functools.partial(_rel_bias_kernel, far_bucket=REL_BUCKETS // 2 - 1)
    return pl.pallas_call(
        kern,
        grid=(heads,),
        in_specs=[pl.BlockSpec(memory_space=pltpu.SMEM), pl.BlockSpec((2, t, t), lambda h: (0, 0, 0))],
        out_specs=pl.BlockSpec((None, 2, t, 2 * t), lambda h: (h, 0, 0, 0)),
        out_shape=jax.ShapeDtypeStruct((heads, 2, t, 2 * t), F32),
        name="rel_bias",
    )(rel_bias, bucket)


def _attn_schedule(nq, heads):
    tiles = [(h, i, j) for h in range(heads) for i in reversed(range(nq)) for j in range(i + 1)]
    events, slot_of, free, n_slots = [], {}, [], 0
    pv_pos = 0
    for pos in range(2 * len(tiles)):
        if pos < len(tiles):
            h, i, j = tiles[pos]
            if (h, i) not in slot_of:
                if not free:
                    free.append(n_slots)
                    n_slots += 1
                slot_of[h, i] = free.pop(0)
            events.append(("score", h, i, j, slot_of[h, i]))
        if pv_pos < len(tiles):
            h, i, j = tiles[pv_pos]
            scored = min(pos + 1, len(tiles))
            if scored >= min(tiles.index((h, i, i)) + 1 + ATTN_LEAD, len(tiles)):
                events.append(("pv", h, i, j, slot_of[h, i]))
                pv_pos += 1
                if j == i:
                    free.append(slot_of[h, i])
    assert pv_pos == len(tiles)
    return events, n_slots


def _diff_attn_kernel(base_ref, lam_ref, q_ref, k_ref, v_ref, nb_ref, sg_ref, o_ref, s_ref, vt_ref, *, lambda_init):
    seq = q_ref.shape[0]
    hd = q_ref.shape[1] // ATTN_HEADS
    t = ATTN_TILE
    nq = seq // t
    events, n_slots = _attn_schedule(nq, ATTN_HEADS)
    assert s_ref.shape[0] == n_slots
    base = base_ref[0]
    half = t // 2
    lane = lax.broadcasted_iota(jnp.int32, (half, hd), 1)
    key_pos = lax.broadcasted_iota(jnp.int32, (t, 2 * t), 0)
    col = lax.broadcasted_iota(jnp.int32, (t, 2 * t), 1)
    query_pos = (col // t) * half + col % half
    allowed = key_pos // CHUNK <= query_pos // CHUNK
    lam = (jnp.exp(jnp.sum(lam_ref[0:1, :] * lam_ref[1:2, :], axis=-1, keepdims=True))
           - jnp.exp(jnp.sum(lam_ref[2:3, :] * lam_ref[3:4, :], axis=-1, keepdims=True)) + lambda_init)

    for h in range(ATTN_HEADS):
        for j in range(nq):
            v = v_ref[j * t:(j + 1) * t, h * hd:(h + 1) * hd]
            vt_ref[h, 0:hd, j * t:(j + 1) * t] = v.astype(F32).T.astype(BF16)
        vt_ref[h, hd:, :] = jnp.ones((ONES_ROWS, seq), BF16)

    def stacked_q(h, i):
        parts = []
        for part in range(2):
            q = q_ref[i * t + part * half:i * t + (part + 1) * half, h * hd:(h + 1) * hd]
            zero = jnp.zeros_like(q)
            parts += [jnp.where(lane < hd // 2, q, zero), jnp.where(lane >= hd // 2, q, zero)]
        return jnp.concatenate(parts, axis=0)

    def sublane_max(s):
        return jnp.max(s.reshape(s.shape[0] // 8, 8, s.shape[1]), axis=0)

    def score_tile(h, i, j, slot, qq):
        heads = slice(h * hd, (h + 1) * hd)
        if j < i:
            s = _dot_nt(k_ref[j * t:(j + 1) * t, heads], qq)
            if j == i - 1:
                r0, nc = _prev_bias_region()
                corner = s[r0:, :nc] + nb_ref[h, 0, r0:t, 0:nc]
                s = jnp.concatenate([s[:r0], jnp.concatenate([corner, s[r0:, nc:]], axis=1)], axis=0)
            s_ref[base + slot, j] = s
            return sublane_max(s)
        s_top = _dot_nt(k_ref[j * t:j * t + half, heads], qq)
        s_top = jnp.where(allowed[:half], s_top + nb_ref[h, 1, 0:half, :], NEG_INF)
        s_bot = _dot_nt(k_ref[j * t + half:(j + 1) * t, heads], qq[t:])
        s_bot = jnp.where(allowed[half:, t:], s_bot + nb_ref[h, 1, half:t, t:2 * t], NEG_INF)
        s_ref[base + slot, j, 0:half, :] = s_top
        s_ref[base + slot, j, half:t, t:2 * t] = s_bot
        m_top = sublane_max(s_top)
        m_bot = sublane_max(s_bot)
        return jnp.concatenate([m_top[:, :t], jnp.maximum(m_top[:, t:], m_bot)], axis=1)

    def prob_tile(h, i, j, slot):
        if j < i:
            return jnp.exp2(s_ref[base + slot, j] - mx[h, i]).astype(BF16)
        p_top = jnp.exp2(s_ref[base + slot, j, 0:half, :] - mx[h, i]).astype(BF16)
        p_bot = jnp.exp2(s_ref[base + slot, j, half:t, t:2 * t] - mx[h, i][:, t:]).astype(BF16)
        p_bot = jnp.concatenate([jnp.zeros((half, t), BF16), p_bot], axis=1)
        return jnp.concatenate([p_top, p_bot], axis=0)

    def finish(h, i, acc):
        o_maps = acc[:hd] / acc[hd:hd + 1]
        o = jnp.concatenate([o_maps[:, part * t:part * t + half] - lam * o_maps[:, part * t + half:(part + 1) * t]
                             for part in range(2)], axis=1).T
        o = _rms(o, sg_ref[...]) * (1.0 - lambda_init)
        o_ref[i * t:(i + 1) * t, h * hd:(h + 1) * hd] = o.astype(o_ref.dtype)

    qq, mx, acc = {}, {}, {}
    for kind, h, i, j, slot in events:
        if kind == "score":
            if j == 0:
                qq[h, i] = stacked_q(h, i)
            tile_max = score_tile(h, i, j, slot, qq[h, i])
            mx[h, i] = tile_max if j == 0 else jnp.maximum(mx[h, i], tile_max)
        else:
            if j == 0:
                mx[h, i] = jnp.max(mx[h, i], axis=0, keepdims=True)
            pv = _dot(vt_ref[h, :, j * t:(j + 1) * t], prob_tile(h, i, j, slot))
            acc[h, i] = pv if j == 0 else acc[h, i] + pv
            if j == i:
                finish(h, i, acc.pop((h, i)))


def _diff_attn(lam_vecs, q, k, v, near_bias, subln_g, *, batch, seq, lambda_init):
    n, width = q.shape
    t = ATTN_TILE
    hd = width // DIFF_HEADS
    nq = seq // t
    q3, k3, v3 = (a.reshape(batch, seq, width) for a in (q, k, v))
    head_block = pl.BlockSpec((None, seq, ATTN_HEADS * hd), lambda g, b: (b, 0, g))
    n_slots = _attn_schedule(nq, ATTN_HEADS)[1]
    kern = functools.partial(_diff_attn_kernel, lambda_init=lambda_init)
    out = pl.pallas_call(
        kern,
        grid=(DIFF_HEADS // ATTN_HEADS, batch),
        in_specs=[
            pl.BlockSpec(memory_space=pltpu.SMEM),
            pl.BlockSpec(lam_vecs.shape, lambda g, b: (0, 0)),
            head_block, head_block, head_block,
            pl.BlockSpec((ATTN_HEADS, 2, t, 2 * t), lambda g, b: (g, 0, 0, 0)),
            pl.BlockSpec(subln_g.shape, lambda g, b: (0, 0)),
        ],
        out_specs=head_block,
        out_shape=jax.ShapeDtypeStruct((batch, seq, width), BF16),
        scratch_shapes=[pltpu.VMEM((n_slots, nq, t, 2 * t), F32),
                        pltpu.VMEM((ATTN_HEADS, hd + ONES_ROWS, seq), BF16)],
        compiler_params=pltpu.CompilerParams(dimension_semantics=("arbitrary",) * 2, vmem_limit_bytes=VMEM_LIMIT),
        name="diff_attn",
    )(jnp.zeros((1,), jnp.int32), lam_vecs, q3, k3, v3, near_bias, subln_g)
    return out.reshape(n, width)


def _mem_kv_kernel(mem_ref, g_ref, w_ref, kv_ref):
    mn = _rms(mem_ref[...], g_ref[...]).astype(BF16)
    kv_ref[...] = _dot(mn, w_ref[...]).astype(BF16)


def _mem_kv(mem2, g, w, *, mem_len):
    n, d = mem2.shape
    return pl.pallas_call(
        _mem_kv_kernel,
        grid=(n // mem_len,),
        in_specs=[pl.BlockSpec((mem_len, d), lambda i: (i, 0)), _resident(g.shape), _resident(w.shape)],
        out_specs=pl.BlockSpec((mem_len, w.shape[1]), lambda i: (i, 0)),
        out_shape=jax.ShapeDtypeStruct((n, w.shape[1]), BF16),
        compiler_params=pltpu.CompilerParams(dimension_semantics=("arbitrary",), vmem_limit_bytes=VMEM_LIMIT),
        name="mem_kv",
    )(mem2, g, w)


def _merge_cross_kernel(x_ref, o_ref, gaya_ref, gb_ref, wb_ref, wmo_ref, g_ref, wq_ref, ck_ref, cv_ref, wo_ref,
                        out_ref, *, q_scale):
    tm, d = x_ref.shape
    y_b = _dot(o_ref[...], wb_ref[...].astype(BF16))
    merged = gaya_ref[...].astype(F32) + gb_ref[...].astype(F32) * y_b
    h = x_ref[...] + _dot(merged.astype(BF16), wmo_ref[...].astype(BF16))

    hn = _rms(h, g_ref[...]).astype(BF16)
    cq = (_dot(hn, wq_ref[...].astype(BF16)) * q_scale).astype(BF16)
    hd = d // CROSS_HEADS
    cols = [slice(a * hd, (a + 1) * hd) for a in range(CROSS_HEADS)]
    scores = [_dot_nt(cq[:, c], ck_ref[:, c]) for c in cols]
    probs = [jnp.exp2(s - jnp.max(s, axis=-1, keepdims=True)) for s in scores]
    heads = [_dot(p.astype(BF16), cv_ref[:, c]) / jnp.sum(p, axis=-1, keepdims=True) for p, c in zip(probs, cols)]
    o = jnp.concatenate(heads, axis=-1).astype(BF16)
    out_ref[...] = h + _dot(o, wo_ref[...].astype(BF16))


def _merge_cross(x2, o, gaya, gb, wb, wmo, g, wq, ckv, wo, *, seq, mem_len):
    n, d = x2.shape
    tm = TOKEN_TILE
    tiles_per_seq = seq // tm
    row = lambda c: pl.BlockSpec((tm, c), lambda i: (i, 0))
    kern = functools.partial(_merge_cross_kernel, q_scale=(d // CROSS_HEADS) ** -0.5 * LOG2E)
    return pl.pallas_call(
        kern,
        grid=(n // tm,),
        in_specs=[row(d), row(o.shape[1]), row(d), row(d), _resident(wb.shape), _resident(wmo.shape),
                  _resident(g.shape), _resident(wq.shape),
                  pl.BlockSpec((mem_len, d), lambda i: (i // tiles_per_seq, 0)),
                  pl.BlockSpec((mem_len, d), lambda i: (i // tiles_per_seq, 1)),
                  _resident(wo.shape)],
        out_specs=row(d),
        out_shape=jax.ShapeDtypeStruct((n, d), F32),
        compiler_params=pltpu.CompilerParams(dimension_semantics=("arbitrary",), vmem_limit_bytes=VMEM_LIMIT),
        name="merge_cross",
    )(x2, o, gaya, gb, wb, wmo, g, wq, ckv, ckv, wo)


def _mlp_kernel(h_ref, g_ref, w1_ref, w2_ref, gf_ref, out_ref, *, final_norm):
    h = h_ref[...]
    hn = _rms(h, g_ref[...]).astype(BF16)
    acc = h
    for c in range(w1_ref.shape[1] // FF_CHUNK):
        cols = slice(c * FF_CHUNK, (c + 1) * FF_CHUNK)
        a = jnp.maximum(_dot(hn, w1_ref[:, cols].astype(BF16)), 0.0)
        acc = acc + _dot((a * a).astype(BF16), w2_ref[cols, :].astype(BF16))
    out_ref[...] = _rms(acc, gf_ref[...]) if final_norm else acc


def _mlp(h, g, w1, w2, gf, *, final_norm):
    n, d = h.shape
    tm = MLP_TILE
    row = pl.BlockSpec((tm, d), lambda i: (i, 0))
    kern = functools.partial(_mlp_kernel, final_norm=final_norm)
    return pl.pallas_call(
        kern,
        grid=(n // tm,),
        in_specs=[row, _resident(g.shape), _resident(w1.shape), _resident(w2.shape), _resident(gf.shape)],
        out_specs=row,
        out_shape=jax.ShapeDtypeStruct((n, d), F32),
        compiler_params=pltpu.CompilerParams(dimension_semantics=("arbitrary",), vmem_limit_bytes=VMEM_LIMIT),
        name="mlp",
    )(h, g, w1, w2, gf)


def kernel(x, mem, norm_mix_g, w_in, w_pool_group, pool_scale, w_a_proj, lambda_q1, lambda_k1, lambda_q2, lambda_k2, subln_g, rel_bias, w_b_proj, w_gate, b_gate, w_out, norm_cross_g, norm_mem_g, w_cq, w_ckv, w_co, norm_mlp_g, w_ff1, w_ff2, final_norm_g):
    batch, seq, d = x.shape
    mem_len = mem.shape[1]
    depth = w_in.shape[0]
    pool_width = w_a_proj.shape[1]
    assert pool_width == d and seq % TOKEN_TILE == 0 and seq % ATTN_TILE == 0
    row = lambda a: a.reshape(1, -1).astype(F32)
    bf = lambda a: a.astype(BF16)

    near_bias = _rel_bias_tiles(rel_bias.astype(F32))
    h = x.reshape(batch * seq, d)
    mem2 = mem.reshape(batch * mem_len, d)
    for l in range(depth):
        lambda_init = LAMBDA_INIT_BASE - LAMBDA_INIT_AMP * math.exp(-LAMBDA_INIT_RATE * l)
        q, k, v, gaya, gb = _mixer_front(
            h, row(norm_mix_g[l]), w_in[l], w_gate[l], row(b_gate[l]), w_pool_group[l], row(pool_scale[l]), w_a_proj[l],
            seq=seq, q_scale=DIFF_HEAD_DIM ** -0.5 * LOG2E)
        lam_vecs = jnp.stack([lambda_q1[l], lambda_k1[l], lambda_q2[l], lambda_k2[l]]).astype(F32)
        o = _diff_attn(lam_vecs, q, k, v, near_bias, row(subln_g[l]), batch=batch, seq=seq, lambda_init=lambda_init)
        ckv = _mem_kv(mem2, row(norm_mem_g[l]), bf(w_ckv[l]), mem_len=mem_len)
        h = _merge_cross(h, o, gaya, gb, w_b_proj[l], w_out[l], row(norm_cross_g[l]), w_cq[l], ckv, w_co[l],
                         seq=seq, mem_len=mem_len)
        h = _mlp(h, row(norm_mlp_g[l]), w_ff1[l], w_ff2[l], row(final_norm_g), final_norm=(l == depth - 1))
    return h.reshape(batch, seq, d)
```

```python
import functools
import math

import jax
import jax.numpy as jnp
import numpy as np
from jax import lax
from jax.experimental import pallas as pl
from jax.experimental.pallas import tpu as pltpu

CHUNK = 64
POOL_WINDOWS = (2, 4, 8, 16)
DIFF_HEADS = 8
DIFF_HEAD_DIM = 64
LAMBDA_INIT_BASE = 0.8
LAMBDA_INIT_AMP = 0.6
LAMBDA_INIT_RATE = 0.3
REL_BUCKETS = 32
REL_MAX_DIST = 128
CROSS_HEADS = 4
EPS = 1e-6
NEG_INF = -1e30
LOG2E = math.log2(math.e)

POOL_HALO = 16
TOKEN_TILE = 512
MLP_TILE = 512
ATTN_TILE = 256
ATTN_HEADS = 2
ONES_ROWS = 16
ATTN_LEAD = 2
FF_CHUNK = 1024
VMEM_LIMIT = 56 * 1024 * 1024

BF16 = jnp.bfloat16
F32 = jnp.float32


def _resident(shape):
    n = len(shape)
    return pl.BlockSpec(shape, lambda *_: (0,) * n, pipeline_mode=pl.Buffered(1))


def _rms(x, g):
    return x * lax.rsqrt(jnp.mean(x * x, axis=-1, keepdims=True) + EPS) * g


def _dot(a, b):
    return jnp.dot(a, b, preferred_element_type=F32)


def _dot_nt(a, b):
    return lax.dot_general(a, b, (((1,), (1,)), ((), ())), preferred_element_type=F32)


def _mixer_front_kernel(x_ref, g_ref, win_ref, wg_ref, bg_ref, wp_ref, ps_ref, wa_ref,
                        q_ref, k_ref, vt_ref, gaya_ref, gb_ref, ubuf_ref, ga_ref, *, tiles_per_seq, q_scale):
    tm, d = x_ref.shape
    i = pl.program_id(0)
    tile_in_seq = i % tiles_per_seq
    xn = _rms(x_ref[...], g_ref[...]).astype(BF16)

    @pl.when(tile_in_seq == 0)
    def _():
        ubuf_ref[0:POOL_HALO, :] = jnp.zeros((POOL_HALO, d), F32)

    def w_in(lo, hi):
        return win_ref[:, lo:hi].astype(BF16)

    ubuf_ref[POOL_HALO:POOL_HALO + tm, :] = _dot(xn, w_in(0, d))

    width = q_ref.shape[1]
    q_ref[...] = (_dot(xn, w_in(d, d + width)) * q_scale).astype(BF16)
    k_ref[...] = _dot(xn, w_in(d + width, d + 2 * width)).astype(BF16)
    v = _dot(xn, w_in(d + 2 * width, d + 3 * width))
    hd = width // DIFF_HEADS
    for h in range(DIFF_HEADS):
        r0 = h * (hd + ONES_ROWS)
        vt_ref[r0:r0 + hd, :] = v[:, h * hd:(h + 1) * hd].T.astype(BF16)
        vt_ref[r0 + hd:r0 + hd + ONES_ROWS, :] = jnp.ones((ONES_ROWS, tm), BF16)
    z = _dot(xn, wg_ref[...].astype(BF16)) + bg_ref[...]
    gates = 1.0 / (1.0 + jnp.exp(-z))
    ga_ref[...] = gates[:, :d]
    gb_ref[...] = gates[:, d:].astype(gb_ref.dtype)

    pos = tile_in_seq * tm + lax.broadcasted_iota(jnp.int32, (tm, 1), 0)
    gd = d // len(POOL_WINDOWS)
    mapped = []
    for g, w in enumerate(POOL_WINDOWS):
        cols = slice(g * gd, (g + 1) * gd)
        u_g = ubuf_ref[POOL_HALO:POOL_HALO + tm, cols]
        acc = u_g
        for j in range(1, w):
            acc = acc + ubuf_ref[POOL_HALO - j:POOL_HALO - j + tm, cols]
        inv_count = 1.0 / jnp.minimum(pos + 1, w).astype(F32)
        pooled = acc * inv_count - u_g
        mapped.append(_dot(pooled.astype(BF16), wp_ref[g].astype(BF16)))
    ubuf_ref[0:POOL_HALO, :] = ubuf_ref[tm:tm + POOL_HALO, :]
    y = jnp.concatenate(mapped, axis=-1) * ps_ref[...]
    gaya_ref[...] = (ga_ref[...] * _dot(y.astype(BF16), wa_ref[...].astype(BF16))).astype(gaya_ref.dtype)


def _mixer_front(x2, g, win, wg, bg, wp, ps, wa, *, seq, q_scale):
    n, d = x2.shape
    tm = TOKEN_TILE
    width = (win.shape[1] - d) // 3
    row = lambda c: pl.BlockSpec((tm, c), lambda i: (i, 0))
    tiles_per_seq = seq // tm
    vt_rows = width + DIFF_HEADS * ONES_ROWS
    kern = functools.partial(_mixer_front_kernel, tiles_per_seq=tiles_per_seq, q_scale=q_scale)
    return pl.pallas_call(
        kern,
        grid=(n // tm,),
        in_specs=[row(d), _resident(g.shape), _resident(win.shape), _resident(wg.shape), _resident(bg.shape),
                  _resident(wp.shape), _resident(ps.shape), _resident(wa.shape)],
        out_specs=[row(width), row(width),
                   pl.BlockSpec((None, vt_rows, tm), lambda i: (i // tiles_per_seq, 0, i % tiles_per_seq)),
                   row(d), row(d)],
        out_shape=[jax.ShapeDtypeStruct((n, width), BF16)] * 2
        + [jax.ShapeDtypeStruct((n // seq, vt_rows, seq), BF16)] + [jax.ShapeDtypeStruct((n, d), BF16)] * 2,
        scratch_shapes=[pltpu.VMEM((POOL_HALO + tm, d), F32), pltpu.VMEM((tm, d), F32)],
        compiler_params=pltpu.CompilerParams(dimension_semantics=("arbitrary",), vmem_limit_bytes=VMEM_LIMIT),
        name="mixer_front",
    )(x2, g, win, wg, bg, wp, ps, wa)


def _rel_bucket(rel):
    nb = REL_BUCKETS // 2
    max_exact = nb // 2
    n = np.abs(rel)
    scaled = np.log(np.maximum(n, 1) / max_exact) / math.log(REL_MAX_DIST / max_exact) * (nb - max_exact)
    nearest = np.rint(scaled)
    scaled = np.where(np.abs(scaled - nearest) < 1e-9, nearest, scaled)
    large = np.minimum(max_exact + np.floor(scaled).astype(np.int64), nb - 1)
    return ((rel > 0) * nb + np.where(n < max_exact, n, large)).astype(np.int32)


def _rel_bias_kernel(table_ref, bucket_ref, out_ref, *, far_bucket):
    h = pl.program_id(0)
    bucket = bucket_ref[...]
    far = table_ref[far_bucket, h]
    acc = jnp.zeros(bucket.shape, F32)
    for b in range(REL_BUCKETS):
        acc = jnp.where(bucket == b, table_ref[b, h] - far, acc)
    t = bucket.shape[-1]
    half = t // 2
    bias = acc * LOG2E
    for part in range(4):
        src = (part // 2) * half
        out_ref[:, :, part * half:(part + 1) * half] = bias[:, :, src:src + half]


def _prev_bias_region():
    t = ATTN_TILE
    key = np.arange(t, dtype=np.int64)[:, None]
    query = np.arange(t, dtype=np.int64)[None, :]
    near = _rel_bucket(key - t - query) != REL_BUCKETS // 2 - 1
    first_row = int(np.nonzero(near.any(axis=1))[0].min()) // 8 * 8
    last_query = int(np.nonzero(near.any(axis=0))[0].max())
    n_cols = t if last_query < t // 2 else 2 * t
    return first_row, n_cols


def _rel_bias_tiles(rel_bias):
    t = ATTN_TILE
    assert t > REL_MAX_DIST
    heads = rel_bias.shape[1]
    key = np.arange(t, dtype=np.int64)[:, None]
    query = np.arange(t, dtype=np.int64)[None, :]
    bucket = jnp.asarray(np.stack([_rel_bucket(key - t - query), _rel_bucket(key - query)]))
    kern = functools.partial(_rel_bias_kernel, far_bucket=REL_BUCKETS // 2 - 1)
    return pl.pallas_call(
        kern,
        grid=(heads,),
        in_specs=[pl.BlockSpec(memory_space=pltpu.SMEM), pl.BlockSpec((2, t, t), lambda h: (0, 0, 0))],
        out_specs=pl.BlockSpec((None, 2, t, 2 * t), lambda h: (h, 0, 0, 0)),
        out_shape=jax.ShapeDtypeStruct((heads, 2, t, 2 * t), F32),
        name="rel_bias",
    )(rel_bias, bucket)


def _attn_schedule(nq, heads):
    tiles = [(h, i, j) for h in range(heads) for i in reversed(range(nq)) for j in range(i + 1)]
    events, slot_of, free, n_slots = [], {}, [], 0
    pv_pos = 0
    for pos in range(2 * len(tiles)):
        if pos < len(tiles):
            h, i, j = tiles[pos]
            if (h, i) not in slot_of:
                if not free:
                    free.append(n_slots)
                    n_slots += 1
                slot_of[h, i] = free.pop(0)
            events.append(("score", h, i, j, slot_of[h, i]))
        if pv_pos < len(tiles):
            h, i, j = tiles[pv_pos]
            scored = min(pos + 1, len(tiles))
            if scored >= min(tiles.index((h, i, i)) + 1 + ATTN_LEAD, len(tiles)):
                events.append(("pv", h, i, j, slot_of[h, i]))
                pv_pos += 1
                if j == i:
                    free.append(slot_of[h, i])
    assert pv_pos == len(tiles)
    return events, n_slots


def _diff_attn_kernel(base_ref, lam_ref, q_ref, k_ref, vt_ref, nb_ref, sg_ref, o_ref, s_ref, *, lambda_init):
    seq = q_ref.shape[0]
    hd = q_ref.shape[1] // ATTN_HEADS
    t = ATTN_TILE
    nq = seq // t
    events, n_slots = _attn_schedule(nq, ATTN_HEADS)
    assert s_ref.shape[0] == n_slots
    base = base_ref[0]
    half = t // 2
    lane = lax.broadcasted_iota(jnp.int32, (half, hd), 1)
    key_pos = lax.broadcasted_iota(jnp.int32, (t, 2 * t), 0)
    col = lax.broadcasted_iota(jnp.int32, (t, 2 * t), 1)
    query_pos = (col // t) * half + col % half
    allowed = key_pos // CHUNK <= query_pos // CHUNK
    lam = (jnp.exp(jnp.sum(lam_ref[0:1, :] * lam_ref[1:2, :], axis=-1, keepdims=True))
           - jnp.exp(jnp.sum(lam_ref[2:3, :] * lam_ref[3:4, :], axis=-1, keepdims=True)) + lambda_init)

    def stacked_q(h, i):
        parts = []
        for part in range(2):
            q = q_ref[i * t + part * half:i * t + (part + 1) * half, h * hd:(h + 1) * hd]
            zero = jnp.zeros_like(q)
            parts += [jnp.where(lane < hd // 2, q, zero), jnp.where(lane >= hd // 2, q, zero)]
        return jnp.concatenate(parts, axis=0)

    def sublane_max(s):
        return jnp.max(s.reshape(s.shape[0] // 8, 8, s.shape[1]), axis=0)

    def score_tile(h, i, j, slot, qq):
        heads = slice(h * hd, (h + 1) * hd)
        if j < i:
            s = _dot_nt(k_ref[j * t:(j + 1) * t, heads], qq)
            if j == i - 1:
                r0, nc = _prev_bias_region()
                corner = s[r0:, :nc] + nb_ref[h, 0, r0:t, 0:nc]
                s = jnp.concatenate([s[:r0], jnp.concatenate([corner, s[r0:, nc:]], axis=1)], axis=0)
            s_ref[base + slot, j] = s
            return sublane_max(s)
        s_top = _dot_nt(k_ref[j * t:j * t + half, heads], qq)
        s_top = jnp.where(allowed[:half], s_top + nb_ref[h, 1, 0:half, :], NEG_INF)
        s_bot = _dot_nt(k_ref[j * t + half:(j + 1) * t, heads], qq[t:])
        s_bot = jnp.where(allowed[half:, t:], s_bot + nb_ref[h, 1, half:t, t:2 * t], NEG_INF)
        s_ref[base + slot, j, 0:half, :] = s_top
        s_ref[base + slot, j, half:t, t:2 * t] = s_bot
        m_top = sublane_max(s_top)
        m_bot = sublane_max(s_bot)
        return jnp.concatenate([m_top[:, :t], jnp.maximum(m_top[:, t:], m_bot)], axis=1)

    def prob_tile(h, i, j, slot):
        if j < i:
            return jnp.exp2(s_ref[base + slot, j] - mx[h, i]).astype(BF16)
        p_top = jnp.exp2(s_ref[base + slot, j, 0:half, :] - mx[h, i]).astype(BF16)
        p_bot = jnp.exp2(s_ref[base + slot, j, half:t, t:2 * t] - mx[h, i][:, t:]).astype(BF16)
        p_bot = jnp.concatenate([jnp.zeros((half, t), BF16), p_bot], axis=1)
        return jnp.concatenate([p_top, p_bot], axis=0)

    def finish(h, i, acc):
        o_maps = acc[:hd] / acc[hd:hd + 1]
        o = jnp.concatenate([o_maps[:, part * t:part * t + half] - lam * o_maps[:, part * t + half:(part + 1) * t]
                             for part in range(2)], axis=1).T
        o = _rms(o, sg_ref[...]) * (1.0 - lambda_init)
        o_ref[i * t:(i + 1) * t, h * hd:(h + 1) * hd] = o.astype(o_ref.dtype)

    qq, mx, acc = {}, {}, {}
    for kind, h, i, j, slot in events:
        if kind == "score":
            if j == 0:
                qq[h, i] = stacked_q(h, i)
            tile_max = score_tile(h, i, j, slot, qq[h, i])
            mx[h, i] = tile_max if j == 0 else jnp.maximum(mx[h, i], tile_max)
        else:
            if j == 0:
                mx[h, i] = jnp.max(mx[h, i], axis=0, keepdims=True)
            vt = vt_ref[h * (hd + ONES_ROWS):(h + 1) * (hd + ONES_ROWS), j * t:(j + 1) * t]
            pv = _dot(vt, prob_tile(h, i, j, slot))
            acc[h, i] = pv if j == 0 else acc[h, i] + pv
            if j == i:
                finish(h, i, acc.pop((h, i)))


def _diff_attn(lam_vecs, q, k, vt, near_bias, subln_g, *, batch, seq, lambda_init):
    n, width = q.shape
    t = ATTN_TILE
    hd = width // DIFF_HEADS
    nq = seq // t
    q3, k3 = (a.reshape(batch, seq, width) for a in (q, k))
    head_block = pl.BlockSpec((None, seq, ATTN_HEADS * hd), lambda g, b: (b, 0, g))
    vt_block = pl.BlockSpec((None, ATTN_HEADS * (hd + ONES_ROWS), seq), lambda g, b: (b, g, 0))
    n_slots = _attn_schedule(nq, ATTN_HEADS)[1]
    kern = functools.partial(_diff_attn_kernel, lambda_init=lambda_init)
    out = pl.pallas_call(
        kern,
        grid=(DIFF_HEADS // ATTN_HEADS, batch),
        in_specs=[
            pl.BlockSpec(memory_space=pltpu.SMEM),
            pl.BlockSpec(lam_vecs.shape, lambda g, b: (0, 0)),
            head_block, head_block, vt_block,
            pl.BlockSpec((ATTN_HEADS, 2, t, 2 * t), lambda g, b: (g, 0, 0, 0)),
            pl.BlockSpec(subln_g.shape, lambda g, b: (0, 0)),
        ],
        out_specs=head_block,
        out_shape=jax.ShapeDtypeStruct((batch, seq, width), BF16),
        scratch_shapes=[pltpu.VMEM((n_slots, nq, t, 2 * t), F32)],
        compiler_params=pltpu.CompilerParams(dimension_semantics=("arbitrary",) * 2, vmem_limit_bytes=VMEM_LIMIT),
        name="diff_attn",
    )(jnp.zeros((1,), jnp.int32), lam_vecs, q3, k3, vt, near_bias, subln_g)
    return out.reshape(n, width)


def _mem_kv_kernel(mem_ref, g_ref, w_ref, kv_ref):
    mn = _rms(mem_ref[...], g_ref[...]).astype(BF16)
    kv_ref[...] = _dot(mn, w_ref[...]).astype(BF16)


def _mem_kv(mem2, g, w, *, mem_len):
    n, d = mem2.shape
    return pl.pallas_call(
        _mem_kv_kernel,
        grid=(n // mem_len,),
        in_specs=[pl.BlockSpec((mem_len, d), lambda i: (i, 0)), _resident(g.shape), _resident(w.shape)],
        out_specs=pl.BlockSpec((mem_len, w.shape[1]), lambda i: (i, 0)),
        out_shape=jax.ShapeDtypeStruct((n, w.shape[1]), BF16),
        compiler_params=pltpu.CompilerParams(dimension_semantics=("arbitrary",), vmem_limit_bytes=VMEM_LIMIT),
        name="mem_kv",
    )(mem2, g, w)


def _merge_cross_kernel(x_ref, o_ref, gaya_ref, gb_ref, wb_ref, wmo_ref, g_ref, wq_ref, ck_ref, cv_ref, wo_ref,
                        out_ref, *, q_scale):
    tm, d = x_ref.shape
    y_b = _dot(o_ref[...], wb_ref[...].astype(BF16))
    merged = gaya_ref[...].astype(F32) + gb_ref[...].astype(F32) * y_b
    h = x_ref[...] + _dot(merged.astype(BF16), wmo_ref[...].astype(BF16))

    hn = _rms(h, g_ref[...]).astype(BF16)
    cq = (_dot(hn, wq_ref[...].astype(BF16)) * q_scale).astype(BF16)
    hd = d // CROSS_HEADS
    cols = [slice(a * hd, (a + 1) * hd) for a in range(CROSS_HEADS)]
    scores = [_dot_nt(cq[:, c], ck_ref[:, c]) for c in cols]
    probs = [jnp.exp2(s - jnp.max(s, axis=-1, keepdims=True)) for s in scores]
    heads = [_dot(p.astype(BF16), cv_ref[:, c]) / jnp.sum(p, axis=-1, keepdims=True) for p, c in zip(probs, cols)]
    o = jnp.concatenate(heads, axis=-1).astype(BF16)
    out_ref[...] = h + _dot(o, wo_ref[...].astype(BF16))


def _merge_cross(x2, o, gaya, gb, wb, wmo, g, wq, ckv, wo, *, seq, mem_len):
    n, d = x2.shape
    tm = TOKEN_TILE
    tiles_per_seq = seq // tm
    row = lambda c: pl.BlockSpec((tm, c), lambda i: (i, 0))
    kern = functools.partial(_merge_cross_kernel, q_scale=(d // CROSS_HEADS) ** -0.5 * LOG2E)
    return pl.pallas_call(
        kern,
        grid=(n // tm,),
        in_specs=[row(d), row(o.shape[1]), row(d), row(d), _resident(wb.shape), _resident(wmo.shape),
                  _resident(g.shape), _resident(wq.shape),
                  pl.BlockSpec((mem_len, d), lambda i: (i // tiles_per_seq, 0)),
                  pl.BlockSpec((mem_len, d), lambda i: (i // tiles_per_seq, 1)),
                  _resident(wo.shape)],
        out_specs=row(d),
        out_shape=jax.ShapeDtypeStruct((n, d), F32),
        compiler_params=pltpu.CompilerParams(dimension_semantics=("arbitrary",), vmem_limit_bytes=VMEM_LIMIT),
        name="merge_cross",
    )(x2, o, gaya, gb, wb, wmo, g, wq, ckv, ckv, wo)


def _mlp_kernel(h_ref, g_ref, w1_ref, w2_ref, gf_ref, out_ref, *, final_norm):
    h = h_ref[...]
    hn = _rms(h, g_ref[...]).astype(BF16)
    acc = h
    for c in range(w1_ref.shape[1] // FF_CHUNK):
        cols = slice(c * FF_CHUNK, (c + 1) * FF_CHUNK)
        a = jnp.maximum(_dot(hn, w1_ref[:, cols].astype(BF16)), 0.0)
        acc = acc + _dot((a * a).astype(BF16), w2_ref[cols, :].astype(BF16))
    out_ref[...] = _rms(acc, gf_ref[...]) if final_norm else acc


def _mlp(h, g, w1, w2, gf, *, final_norm):
    n, d = h.shape
    tm = MLP_TILE
    row = pl.BlockSpec((tm, d), lambda i: (i, 0))
    kern = functools.partial(_mlp_kernel, final_norm=final_norm)
    return pl.pallas_call(
        kern,
        grid=(n // tm,),
        in_specs=[row, _resident(g.shape), _resident(w1.shape), _resident(w2.shape), _resident(gf.shape)],
        out_specs=row,
        out_shape=jax.ShapeDtypeStruct((n, d), F32),
        compiler_params=pltpu.CompilerParams(dimension_semantics=("arbitrary",), vmem_limit_bytes=VMEM_LIMIT),
        name="mlp",
    )(h, g, w1, w2, gf)


def kernel(x, mem, norm_mix_g, w_in, w_pool_group, pool_scale, w_a_proj, lambda_q1, lambda_k1, lambda_q2, lambda_k2, subln_g, rel_bias, w_b_proj, w_gate, b_gate, w_out, norm_cross_g, norm_mem_g, w_cq, w_ckv, w_co, norm_mlp_g, w_ff1, w_ff2, final_norm_g):
    batch, seq, d = x.shape
    mem_len = mem.shape[1]
    depth = w_in.shape[0]
    pool_width = w_a_proj.shape[1]
    assert pool_width == d and seq % TOKEN_TILE == 0 and seq % ATTN_TILE == 0
    row = lambda a: a.reshape(1, -1).astype(F32)
    bf = lambda a: a.astype(BF16)

    near_bias = _rel_bias_tiles(rel_bias.astype(F32))
    h = x.reshape(batch * seq, d)
    mem2 = mem.reshape(batch * mem_len, d)
    for l in range(depth):
        lambda_init = LAMBDA_INIT_BASE - LAMBDA_INIT_AMP * math.exp(-LAMBDA_INIT_RATE * l)
        q, k, vt, gaya, gb = _mixer_front(
            h, row(norm_mix_g[l]), w_in[l], w_gate[l], row(b_gate[l]), w_pool_group[l], row(pool_scale[l]), w_a_proj[l],
            seq=seq, q_scale=DIFF_HEAD_DIM ** -0.5 * LOG2E)
        lam_vecs = jnp.stack([lambda_q1[l], lambda_k1[l], lambda_q2[l], lambda_k2[l]]).astype(F32)
        o = _diff_attn(lam_vecs, q, k, vt, near_bias, row(subln_g[l]), batch=batch, seq=seq, lambda_init=lambda_init)
        ckv = _mem_kv(mem2, row(norm_mem_g[l]), bf(w_ckv[l]), mem_len=mem_len)
        h = _merge_cross(h, o, gaya, gb, w_b_proj[l], w_out[l], row(norm_cross_g[l]), w_cq[l], ckv, w_co[l],
                         seq=seq, mem_len=mem_len)
        h = _mlp(h, row(norm_mlp_g[l]), w_ff1[l], w_ff2[l], row(final_norm_g), final_norm=(l == depth - 1))
    return h.reshape(batch, seq, d)
```

```python
import functools
import math

import jax
import jax.numpy as jnp
import numpy as np
from jax import lax
from jax.experimental import pallas as pl
from jax.experimental.pallas import tpu as pltpu

CHUNK = 64
POOL_WINDOWS = (2, 4, 8, 16)
DIFF_HEADS = 8
DIFF_HEAD_DIM = 64
LAMBDA_INIT_BASE = 0.8
LAMBDA_INIT_AMP = 0.6
LAMBDA_INIT_RATE = 0.3
REL_BUCKETS = 32
REL_MAX_DIST = 128
CROSS_HEADS = 4
EPS = 1e-6
NEG_INF = -1e30
LOG2E = math.log2(math.e)

POOL_HALO = 16
TOKEN_TILE = 512
MLP_TILE = 512
MEM_TILE = 1024
ATTN_TILE = 256
ATTN_HEADS = 2
ONES_ROWS = 16
ATTN_LEAD = 2
FF_CHUNK = 1024
VMEM_LIMIT = 56 * 1024 * 1024

BF16 = jnp.bfloat16
F32 = jnp.float32


def _resident(shape):
    n = len(shape)
    return pl.BlockSpec(shape, lambda *_: (0,) * n, pipeline_mode=pl.Buffered(1))


def _rms(x, g):
    return x * lax.rsqrt(jnp.mean(x * x, axis=-1, keepdims=True) + EPS) * g


def _dot(a, b):
    return jnp.dot(a, b, preferred_element_type=F32)


def _dot_nt(a, b):
    return lax.dot_general(a, b, (((1,), (1,)), ((), ())), preferred_element_type=F32)


def _mixer_front_kernel(x_ref, g_ref, win_ref, wg_ref, bg_ref, wp_ref, ps_ref, wa_ref,
                        q_ref, k_ref, vt_ref, gaya_ref, gb_ref, ubuf_ref, ga_ref, *, tiles_per_seq, q_scale):
    tm, d = x_ref.shape
    i = pl.program_id(0)
    tile_in_seq = i % tiles_per_seq
    xn = _rms(x_ref[...], g_ref[...]).astype(BF16)

    @pl.when(tile_in_seq == 0)
    def _():
        ubuf_ref[0:POOL_HALO, :] = jnp.zeros((POOL_HALO, d), F32)

    def w_in(lo, hi):
        return win_ref[:, lo:hi].astype(BF16)

    ubuf_ref[POOL_HALO:POOL_HALO + tm, :] = _dot(xn, w_in(0, d))

    width = q_ref.shape[1]
    q_ref[...] = (_dot(xn, w_in(d, d + width)) * q_scale).astype(BF16)
    k_ref[...] = _dot(xn, w_in(d + width, d + 2 * width)).astype(BF16)
    v = _dot(xn, w_in(d + 2 * width, d + 3 * width))
    hd = width // DIFF_HEADS
    for h in range(DIFF_HEADS):
        r0 = h * (hd + ONES_ROWS)
        vt_ref[r0:r0 + hd, :] = v[:, h * hd:(h + 1) * hd].T.astype(BF16)
        vt_ref[r0 + hd:r0 + hd + ONES_ROWS, :] = jnp.ones((ONES_ROWS, tm), BF16)
    z = _dot(xn, wg_ref[...].astype(BF16)) + bg_ref[...]
    gates = 1.0 / (1.0 + jnp.exp(-z))
    ga_ref[...] = gates[:, :d]
    gb_ref[...] = gates[:, d:].astype(gb_ref.dtype)

    pos = tile_in_seq * tm + lax.broadcasted_iota(jnp.int32, (tm, 1), 0)
    gd = d // len(POOL_WINDOWS)
    mapped = []
    for g, w in enumerate(POOL_WINDOWS):
        cols = slice(g * gd, (g + 1) * gd)
        u_g = ubuf_ref[POOL_HALO:POOL_HALO + tm, cols]
        acc = u_g
        for j in range(1, w):
            acc = acc + ubuf_ref[POOL_HALO - j:POOL_HALO - j + tm, cols]
        inv_count = 1.0 / jnp.minimum(pos + 1, w).astype(F32)
        pooled = acc * inv_count - u_g
        mapped.append(_dot(pooled.astype(BF16), wp_ref[g].astype(BF16)))
    ubuf_ref[0:POOL_HALO, :] = ubuf_ref[tm:tm + POOL_HALO, :]
    y = jnp.concatenate(mapped, axis=-1) * ps_ref[...]
    gaya_ref[...] = (ga_ref[...] * _dot(y.astype(BF16), wa_ref[...].astype(BF16))).astype(gaya_ref.dtype)


def _mixer_front(x2, g, win, wg, bg, wp, ps, wa, *, seq, q_scale):
    n, d = x2.shape
    tm = TOKEN_TILE
    width = (win.shape[1] - d) // 3
    row = lambda c: pl.BlockSpec((tm, c), lambda i: (i, 0))
    tiles_per_seq = seq // tm
    vt_rows = width + DIFF_HEADS * ONES_ROWS
    kern = functools.partial(_mixer_front_kernel, tiles_per_seq=tiles_per_seq, q_scale=q_scale)
    return pl.pallas_call(
        kern,
        grid=(n // tm,),
        in_specs=[row(d), _resident(g.shape), _resident(win.shape), _resident(wg.shape), _resident(bg.shape),
                  _resident(wp.shape), _resident(ps.shape), _resident(wa.shape)],
        out_specs=[row(width), row(width),
                   pl.BlockSpec((None, vt_rows, tm), lambda i: (i // tiles_per_seq, 0, i % tiles_per_seq)),
                   row(d), row(d)],
        out_shape=[jax.ShapeDtypeStruct((n, width), BF16)] * 2
        + [jax.ShapeDtypeStruct((n // seq, vt_rows, seq), BF16)] + [jax.ShapeDtypeStruct((n, d), BF16)] * 2,
        scratch_shapes=[pltpu.VMEM((POOL_HALO + tm, d), F32), pltpu.VMEM((tm, d), F32)],
        compiler_params=pltpu.CompilerParams(dimension_semantics=("arbitrary",), vmem_limit_bytes=VMEM_LIMIT),
        name="mixer_front",
    )(x2, g, win, wg, bg, wp, ps, wa)


def _rel_bucket(rel):
    nb = REL_BUCKETS // 2
    max_exact = nb // 2
    n = np.abs(rel)
    scaled = np.log(np.maximum(n, 1) / max_exact) / math.log(REL_MAX_DIST / max_exact) * (nb - max_exact)
    nearest = np.rint(scaled)
    scaled = np.where(np.abs(scaled - nearest) < 1e-9, nearest, scaled)
    large = np.minimum(max_exact + np.floor(scaled).astype(np.int64), nb - 1)
    return ((rel > 0) * nb + np.where(n < max_exact, n, large)).astype(np.int32)


def _rel_bias_kernel(table_ref, bucket_ref, out_ref, *, far_bucket, tile_buckets):
    h = pl.program_id(0)
    far = table_ref[far_bucket, h]
    t = bucket_ref.shape[-1]
    half = t // 2
    for tile, buckets in enumerate(tile_buckets):
        bucket = bucket_ref[tile]
        acc = jnp.zeros(bucket.shape, F32)
        for b in buckets:
            acc = jnp.where(bucket == b, table_ref[b, h] - far, acc)
        bias = acc * LOG2E
        for part in range(4):
            src = (part // 2) * half
            out_ref[tile, :, part * half:(part + 1) * half] = bias[:, src:src + half]


def _prev_bias_region():
    t = ATTN_TILE
    key = np.arange(t, dtype=np.int64)[:, None]
    query = np.arange(t, dtype=np.int64)[None, :]
    near = _rel_bucket(key - t - query) != REL_BUCKETS // 2 - 1
    first_row = int(np.nonzero(near.any(axis=1))[0].min()) // 8 * 8
    last_query = int(np.nonzero(near.any(axis=0))[0].max())
    n_cols = t if last_query < t // 2 else 2 * t
    return first_row, n_cols


def _rel_bias_tiles(rel_bias):
    t = ATTN_TILE
    assert t > REL_MAX_DIST
    heads = rel_bias.shape[1]
    key = np.arange(t, dtype=np.int64)[:, None]
    query = np.arange(t, dtype=np.int64)[None, :]
    bucket = np.stack([_rel_bucket(key - t - query), _rel_bucket(key - query)])
    far_bucket = REL_BUCKETS // 2 - 1
    tile_buckets = tuple(tuple(int(b) for b in np.unique(tile) if b != far_bucket) for tile in bucket)
    bucket = jnp.asarray(bucket)
    kern = functools.partial(_rel_bias_kernel, far_bucket=far_bucket, tile_buckets=tile_buckets)
    return pl.pallas_call(
        kern,
        grid=(heads,),
        in_specs=[pl.BlockSpec(memory_space=pltpu.SMEM), pl.BlockSpec((2, t, t), lambda h: (0, 0, 0))],
        out_specs=pl.BlockSpec((None, 2, t, 2 * t), lambda h: (h, 0, 0, 0)),
        out_shape=jax.ShapeDtypeStruct((heads, 2, t, 2 * t), F32),
        name="rel_bias",
    )(rel_bias, bucket)


def _attn_schedule(nq, heads):
    tiles = [(h, i, j) for h in range(heads) for i in reversed(range(nq)) for j in range(i + 1)]
    events, slot_of, free, n_slots = [], {}, [], 0
    pv_pos = 0
    for pos in range(2 * len(tiles)):
        if pos < len(tiles):
            h, i, j = tiles[pos]
            if (h, i) not in slot_of:
                if not free:
                    free.append(n_slots)
                    n_slots += 1
                slot_of[h, i] = free.pop(0)
            events.append(("score", h, i, j, slot_of[h, i]))
        if pv_pos < len(tiles):
            h, i, j = tiles[pv_pos]
            scored = min(pos + 1, len(tiles))
            if scored >= min(tiles.index((h, i, i)) + 1 + ATTN_LEAD, len(tiles)):
                events.append(("pv", h, i, j, slot_of[h, i]))
                pv_pos += 1
                if j == i:
                    free.append(slot_of[h, i])
    assert pv_pos == len(tiles)
    return events, n_slots


def _diff_attn_kernel(base_ref, lam_ref, q_ref, k_ref, vt_ref, nb_ref, sg_ref, o_ref, s_ref, *, lambda_init):
    seq = q_ref.shape[0]
    hd = q_ref.shape[1] // ATTN_HEADS
    t = ATTN_TILE
    nq = seq // t
    events, n_slots = _attn_schedule(nq, ATTN_HEADS)
    assert s_ref.shape[0] == n_slots
    base = base_ref[0]
    half = t // 2
    lane = lax.broadcasted_iota(jnp.int32, (half, hd), 1)
    key_pos = lax.broadcasted_iota(jnp.int32, (t, 2 * t), 0)
    col = lax.broadcasted_iota(jnp.int32, (t, 2 * t), 1)
    query_pos = (col // t) * half + col % half
    allowed = key_pos // CHUNK <= query_pos // CHUNK
    lam = (jnp.exp(jnp.sum(lam_ref[0:1, :] * lam_ref[1:2, :], axis=-1, keepdims=True))
           - jnp.exp(jnp.sum(lam_ref[2:3, :] * lam_ref[3:4, :], axis=-1, keepdims=True)) + lambda_init)

    def stacked_q(h, i):
        parts = []
        for part in range(2):
            q = q_ref[i * t + part * half:i * t + (part + 1) * half, h * hd:(h + 1) * hd]
            zero = jnp.zeros_like(q)
            parts += [jnp.where(lane < hd // 2, q, zero), jnp.where(lane >= hd // 2, q, zero)]
        return jnp.concatenate(parts, axis=0)

    def sublane_max(s):
        return jnp.max(s.reshape(s.shape[0] // 8, 8, s.shape[1]), axis=0)

    def score_tile(h, i, j, slot, qq):
        heads = slice(h * hd, (h + 1) * hd)
        if j < i:
            s = _dot_nt(k_ref[j * t:(j + 1) * t, heads], qq)
            if j == i - 1:
                r0, nc = _prev_bias_region()
                corner = s[r0:, :nc] + nb_ref[h, 0, r0:t, 0:nc]
                s = jnp.concatenate([s[:r0], jnp.concatenate([corner, s[r0:, nc:]], axis=1)], axis=0)
            s_ref[base + slot, j] = s
            return sublane_max(s)
        s_top = _dot_nt(k_ref[j * t:j * t + half, heads], qq)
        s_top = jnp.where(allowed[:half], s_top + nb_ref[h, 1, 0:half, :], NEG_INF)
        s_bot = _dot_nt(k_ref[j * t + half:(j + 1) * t, heads], qq[t:])
        s_bot = jnp.where(allowed[half:, t:], s_bot + nb_ref[h, 1, half:t, t:2 * t], NEG_INF)
        s_ref[base + slot, j, 0:half, :] = s_top
        s_ref[base + slot, j, half:t, t:2 * t] = s_bot
        m_top = sublane_max(s_top)
        m_bot = sublane_max(s_bot)
        return jnp.concatenate([m_top[:, :t], jnp.maximum(m_top[:, t:], m_bot)], axis=1)

    def prob_tile(h, i, j, slot):
        if j < i:
            return jnp.exp2(s_ref[base + slot, j] - mx[h, i]).astype(BF16)
        p_top = jnp.exp2(s_ref[base + slot, j, 0:half, :] - mx[h, i]).astype(BF16)
        p_bot = jnp.exp2(s_ref[base + slot, j, half:t, t:2 * t] - mx[h, i][:, t:]).astype(BF16)
        p_bot = jnp.concatenate([jnp.zeros((half, t), BF16), p_bot], axis=1)
        return jnp.concatenate([p_top, p_bot], axis=0)

    def finish(h, i, acc):
        o_maps = acc[:hd] / acc[hd:hd + 1]
        o = jnp.concatenate([o_maps[:, part * t:part * t + half] - lam * o_maps[:, part * t + half:(part + 1) * t]
                             for part in range(2)], axis=1).T
        o = _rms(o, sg_ref[...]) * (1.0 - lambda_init)
        o_ref[i * t:(i + 1) * t, h * hd:(h + 1) * hd] = o.astype(o_ref.dtype)

    qq, mx, acc = {}, {}, {}
    for kind, h, i, j, slot in events:
        if kind == "score":
            if j == 0:
                qq[h, i] = stacked_q(h, i)
            tile_max = score_tile(h, i, j, slot, qq[h, i])
            mx[h, i] = tile_max if j == 0 else jnp.maximum(mx[h, i], tile_max)
        else:
            if j == 0:
                mx[h, i] = jnp.max(mx[h, i], axis=0, keepdims=True)
            vt = vt_ref[h * (hd + ONES_ROWS):(h + 1) * (hd + ONES_ROWS), j * t:(j + 1) * t]
            pv = _dot(vt, prob_tile(h, i, j, slot))
            acc[h, i] = pv if j == 0 else acc[h, i] + pv
            if j == i:
                finish(h, i, acc.pop((h, i)))


def _diff_attn(lam_vecs, q, k, vt, near_bias, subln_g, *, batch, seq, lambda_init):
    n, width = q.shape
    t = ATTN_TILE
    hd = width // DIFF_HEADS
    nq = seq // t
    q3, k3 = (a.reshape(batch, seq, width) for a in (q, k))
    head_block = pl.BlockSpec((None, seq, ATTN_HEADS * hd), lambda g, b: (b, 0, g))
    vt_block = pl.BlockSpec((None, ATTN_HEADS * (hd + ONES_ROWS), seq), lambda g, b: (b, g, 0))
    n_slots = _attn_schedule(nq, ATTN_HEADS)[1]
    kern = functools.partial(_diff_attn_kernel, lambda_init=lambda_init)
    out = pl.pallas_call(
        kern,
        grid=(DIFF_HEADS // ATTN_HEADS, batch),
        in_specs=[
            pl.BlockSpec(memory_space=pltpu.SMEM),
            pl.BlockSpec(lam_vecs.shape, lambda g, b: (0, 0)),
            head_block, head_block, vt_block,
            pl.BlockSpec((ATTN_HEADS, 2, t, 2 * t), lambda g, b: (g, 0, 0, 0)),
            pl.BlockSpec(subln_g.shape, lambda g, b: (0, 0)),
        ],
        out_specs=head_block,
        out_shape=jax.ShapeDtypeStruct((batch, seq, width), BF16),
        scratch_shapes=[pltpu.VMEM((n_slots, nq, t, 2 * t), F32)],
        compiler_params=pltpu.CompilerParams(dimension_semantics=("arbitrary",) * 2, vmem_limit_bytes=VMEM_LIMIT),
        name="diff_attn",
    )(jnp.zeros((1,), jnp.int32), lam_vecs, q3, k3, vt, near_bias, subln_g)
    return out.reshape(n, width)


def _mem_kv_kernel(mem_ref, g_ref, w_ref, kv_ref):
    mn = _rms(mem_ref[...], g_ref[...]).astype(BF16)
    kv_ref[...] = _dot(mn, w_ref[...].astype(BF16)).astype(BF16)


def _mem_kv(mem2, g, w):
    n, d = mem2.shape
    rows = math.gcd(n, MEM_TILE)
    return pl.pallas_call(
        _mem_kv_kernel,
        grid=(n // rows,),
        in_specs=[pl.BlockSpec((rows, d), lambda i: (i, 0)), _resident(g.shape), _resident(w.shape)],
        out_specs=pl.BlockSpec((rows, w.shape[1]), lambda i: (i, 0)),
        out_shape=jax.ShapeDtypeStruct((n, w.shape[1]), BF16),
        compiler_params=pltpu.CompilerParams(dimension_semantics=("arbitrary",), vmem_limit_bytes=VMEM_LIMIT),
        name="mem_kv",
    )(mem2, g, w)


def _merge_cross_kernel(x_ref, o_ref, gaya_ref, gb_ref, wb_ref, wmo_ref, g_ref, wq_ref, ck_ref, cv_ref, wo_ref,
                        out_ref, *, q_scale):
    tm, d = x_ref.shape
    y_b = _dot(o_ref[...], wb_ref[...].astype(BF16))
    merged = gaya_ref[...].astype(F32) + gb_ref[...].astype(F32) * y_b
    h = x_ref[...] + _dot(merged.astype(BF16), wmo_ref[...].astype(BF16))

    hn = _rms(h, g_ref[...]).astype(BF16)
    cq = (_dot(hn, wq_ref[...].astype(BF16)) * q_scale).astype(BF16)
    hd = d // CROSS_HEADS
    cols = [slice(a * hd, (a + 1) * hd) for a in range(CROSS_HEADS)]
    scores = [_dot_nt(cq[:, c], ck_ref[:, c]) for c in cols]
    probs = [jnp.exp2(s - jnp.max(s, axis=-1, keepdims=True)) for s in scores]
    heads = [_dot(p.astype(BF16), cv_ref[:, c]) / jnp.sum(p, axis=-1, keepdims=True) for p, c in zip(probs, cols)]
    o = jnp.concatenate(heads, axis=-1).astype(BF16)
    out_ref[...] = h + _dot(o, wo_ref[...].astype(BF16))


def _merge_cross(x2, o, gaya, gb, wb, wmo, g, wq, ckv, wo, *, seq, mem_len):
    n, d = x2.shape
    tm = TOKEN_TILE
    tiles_per_seq = seq // tm
    row = lambda c: pl.BlockSpec((tm, c), lambda i: (i, 0))
    kern = functools.partial(_merge_cross_kernel, q_scale=(d // CROSS_HEADS) ** -0.5 * LOG2E)
    return pl.pallas_call(
        kern,
        grid=(n // tm,),
        in_specs=[row(d), row(o.shape[1]), row(d), row(d), _resident(wb.shape), _resident(wmo.shape),
                  _resident(g.shape), _resident(wq.shape),
                  pl.BlockSpec((mem_len, d), lambda i: (i // tiles_per_seq, 0)),
                  pl.BlockSpec((mem_len, d), lambda i: (i // tiles_per_seq, 1)),
                  _resident(wo.shape)],
        out_specs=row(d),
        out_shape=jax.ShapeDtypeStruct((n, d), F32),
        compiler_params=pltpu.CompilerParams(dimension_semantics=("arbitrary",), vmem_limit_bytes=VMEM_LIMIT),
        name="merge_cross",
    )(x2, o, gaya, gb, wb, wmo, g, wq, ckv, ckv, wo)


def _mlp_kernel(h_ref, g_ref, w1_ref, w2_ref, gf_ref, out_ref, *, final_norm):
    h = h_ref[...]
    hn = _rms(h, g_ref[...]).astype(BF16)
    acc = h
    for c in range(w1_ref.shape[1] // FF_CHUNK):
        cols = slice(c * FF_CHUNK, (c + 1) * FF_CHUNK)
        a = jnp.maximum(_dot(hn, w1_ref[:, cols].astype(BF16)), 0.0)
        acc = acc + _dot((a * a).astype(BF16), w2_ref[cols, :].astype(BF16))
    out_ref[...] = _rms(acc, gf_ref[...]) if final_norm else acc


def _mlp(h, g, w1, w2, gf, *, final_norm):
    n, d = h.shape
    tm = MLP_TILE
    row = pl.BlockSpec((tm, d), lambda i: (i, 0))
    kern = functools.partial(_mlp_kernel, final_norm=final_norm)
    return pl.pallas_call(
        kern,
        grid=(n // tm,),
        in_specs=[row, _resident(g.shape), _resident(w1.shape), _resident(w2.shape), _resident(gf.shape)],
        out_specs=row,
        out_shape=jax.ShapeDtypeStruct((n, d), F32),
        compiler_params=pltpu.CompilerParams(dimension_semantics=("arbitrary",), vmem_limit_bytes=VMEM_LIMIT),
        name="mlp",
    )(h, g, w1, w2, gf)


def kernel(x, mem, norm_mix_g, w_in, w_pool_group, pool_scale, w_a_proj, lambda_q1, lambda_k1, lambda_q2, lambda_k2, subln_g, rel_bias, w_b_proj, w_gate, b_gate, w_out, norm_cross_g, norm_mem_g, w_cq, w_ckv, w_co, norm_mlp_g, w_ff1, w_ff2, final_norm_g):
    batch, seq, d = x.shape
    mem_len = mem.shape[1]
    depth = w_in.shape[0]
    pool_width = w_a_proj.shape[1]
    assert pool_width == d and seq % TOKEN_TILE == 0 and seq % ATTN_TILE == 0
    row = lambda a: a.reshape(1, -1).astype(F32)

    near_bias = _rel_bias_tiles(rel_bias.astype(F32))
    h = x.reshape(batch * seq, d)
    mem2 = mem.reshape(batch * mem_len, d)
    for l in range(depth):
        lambda_init = LAMBDA_INIT_BASE - LAMBDA_INIT_AMP * math.exp(-LAMBDA_INIT_RATE * l)
        q, k, vt, gaya, gb = _mixer_front(
            h, row(norm_mix_g[l]), w_in[l], w_gate[l], row(b_gate[l]), w_pool_group[l], row(pool_scale[l]), w_a_proj[l],
            seq=seq, q_scale=DIFF_HEAD_DIM ** -0.5 * LOG2E)
        lam_vecs = jnp.stack([lambda_q1[l], lambda_k1[l], lambda_q2[l], lambda_k2[l]]).astype(F32)
        o = _diff_attn(lam_vecs, q, k, vt, near_bias, row(subln_g[l]), batch=batch, seq=seq, lambda_init=lambda_init)
        ckv = _mem_kv(mem2, row(norm_mem_g[l]), w_ckv[l])
        h = _merge_cross(h, o, gaya, gb, w_b_proj[l], w_out[l], row(norm_cross_g[l]), w_cq[l], ckv, w_co[l],
                         seq=seq, mem_len=mem_len)
        h = _mlp(h, row(norm_mlp_g[l]), w_ff1[l], w_ff2[l], row(final_norm_g), final_norm=(l == depth - 1))
    return h.reshape(batch, seq, d)
```

```python
import functools
import math

import jax
import jax.numpy as jnp
import numpy as np
from jax import lax
from jax.experimental import pallas as pl
from jax.experimental.pallas import tpu as pltpu

CHUNK = 64
POOL_WINDOWS = (2, 4, 8, 16)
DIFF_HEADS = 8
DIFF_HEAD_DIM = 64
LAMBDA_INIT_BASE = 0.8
LAMBDA_INIT_AMP = 0.6
LAMBDA_INIT_RATE = 0.3
REL_BUCKETS = 32
REL_MAX_DIST = 128
CROSS_HEADS = 4
EPS = 1e-6
NEG_INF = -1e30
LOG2E = math.log2(math.e)

POOL_HALO = 16
TOKEN_TILE = 512
MLP_TILE = 512
MEM_TILE = 1024
ATTN_TILE = 256
ATTN_HEADS = 2
ONES_ROWS = 16
ATTN_LEAD = 2
FF_CHUNK = 1024
VMEM_LIMIT = 56 * 1024 * 1024

BF16 = jnp.bfloat16
F32 = jnp.float32


def _resident(shape):
    n = len(shape)
    return pl.BlockSpec(shape, lambda *_: (0,) * n, pipeline_mode=pl.Buffered(1))


def _rms(x, g):
    return x * lax.rsqrt(jnp.mean(x * x, axis=-1, keepdims=True) + EPS) * g


def _dot(a, b):
    return jnp.dot(a, b, preferred_element_type=F32)


def _dot_nt(a, b):
    return lax.dot_general(a, b, (((1,), (1,)), ((), ())), preferred_element_type=F32)


def _mixer_front_kernel(x_ref, g_ref, win_ref, wg_ref, bg_ref, wp_ref, ps_ref, wa_ref,
                        q_ref, k_ref, vt_ref, gaya_ref, gb_ref, ubuf_ref, ga_ref, *, tiles_per_seq, q_scale):
    tm, d = x_ref.shape
    i = pl.program_id(0)
    tile_in_seq = i % tiles_per_seq

    @pl.when(tile_in_seq == 0)
    def _():
        ubuf_ref[0:POOL_HALO, :] = jnp.zeros((POOL_HALO, d), F32)

    x = x_ref[...]
    xn = (x * g_ref[...]).astype(BF16)
    rs = lax.rsqrt(jnp.mean(x * x, axis=-1, keepdims=True) + EPS)

    def w_in(lo, hi):
        return win_ref[:, lo:hi].astype(BF16)

    ubuf_ref[POOL_HALO:POOL_HALO + tm, :] = _dot(xn, w_in(0, d)) * rs

    width = q_ref.shape[1]
    q_ref[...] = (_dot(xn, w_in(d, d + width)) * (rs * q_scale)).astype(BF16)
    k_ref[...] = (_dot(xn, w_in(d + width, d + 2 * width)) * rs).astype(BF16)
    v = _dot(xn, w_in(d + 2 * width, d + 3 * width)) * rs
    hd = width // DIFF_HEADS
    for h in range(DIFF_HEADS):
        r0 = h * (hd + ONES_ROWS)
        vt_ref[r0:r0 + hd, :] = v[:, h * hd:(h + 1) * hd].T.astype(BF16)
        vt_ref[r0 + hd:r0 + hd + ONES_ROWS, :] = jnp.ones((ONES_ROWS, tm), BF16)
    z = _dot(xn, wg_ref[...].astype(BF16)) * rs + bg_ref[...]
    gates = 1.0 / (1.0 + jnp.exp(-z))
    ga_ref[...] = gates[:, :d]
    gb_ref[...] = gates[:, d:].astype(gb_ref.dtype)

    pos = tile_in_seq * tm + lax.broadcasted_iota(jnp.int32, (tm, 1), 0)
    gd = d // len(POOL_WINDOWS)
    mapped = []
    for g, w in enumerate(POOL_WINDOWS):
        cols = slice(g * gd, (g + 1) * gd)
        u_g = ubuf_ref[POOL_HALO:POOL_HALO + tm, cols]
        acc = u_g
        for j in range(1, w):
            acc = acc + ubuf_ref[POOL_HALO - j:POOL_HALO - j + tm, cols]
        inv_count = 1.0 / jnp.minimum(pos + 1, w).astype(F32)
        pooled = acc * inv_count - u_g
        mapped.append(_dot(pooled.astype(BF16), wp_ref[g].astype(BF16)))
    ubuf_ref[0:POOL_HALO, :] = ubuf_ref[tm:tm + POOL_HALO, :]
    y = jnp.concatenate(mapped, axis=-1) * ps_ref[...]
    gaya_ref[...] = (ga_ref[...] * _dot(y.astype(BF16), wa_ref[...].astype(BF16))).astype(gaya_ref.dtype)


def _mixer_front(x2, g, win, wg, bg, wp, ps, wa, *, seq, q_scale):
    n, d = x2.shape
    tm = TOKEN_TILE
    width = (win.shape[1] - d) // 3
    row = lambda c: pl.BlockSpec((tm, c), lambda i: (i, 0))
    tiles_per_seq = seq // tm
    vt_rows = width + DIFF_HEADS * ONES_ROWS
    kern = functools.partial(_mixer_front_kernel, tiles_per_seq=tiles_per_seq, q_scale=q_scale)
    return pl.pallas_call(
        kern,
        grid=(n // tm,),
        in_specs=[row(d), _resident(g.shape), _resident(win.shape), _resident(wg.shape), _resident(bg.shape),
                  _resident(wp.shape), _resident(ps.shape), _resident(wa.shape)],
        out_specs=[row(width), row(width),
                   pl.BlockSpec((None, vt_rows, tm), lambda i: (i // tiles_per_seq, 0, i % tiles_per_seq)),
                   row(d), row(d)],
        out_shape=[jax.ShapeDtypeStruct((n, width), BF16)] * 2
        + [jax.ShapeDtypeStruct((n // seq, vt_rows, seq), BF16)] + [jax.ShapeDtypeStruct((n, d), BF16)] * 2,
        scratch_shapes=[pltpu.VMEM((POOL_HALO + tm, d), F32), pltpu.VMEM((tm, d), F32)],
        compiler_params=pltpu.CompilerParams(dimension_semantics=("arbitrary",), vmem_limit_bytes=VMEM_LIMIT),
        name="mixer_front",
    )(x2, g, win, wg, bg, wp, ps, wa)


def _rel_bucket(rel):
    nb = REL_BUCKETS // 2
    max_exact = nb // 2
    n = np.abs(rel)
    scaled = np.log(np.maximum(n, 1) / max_exact) / math.log(REL_MAX_DIST / max_exact) * (nb - max_exact)
    nearest = np.rint(scaled)
    scaled = np.where(np.abs(scaled - nearest) < 1e-9, nearest, scaled)
    large = np.minimum(max_exact + np.floor(scaled).astype(np.int64), nb - 1)
    return ((rel > 0) * nb + np.where(n < max_exact, n, large)).astype(np.int32)


def _rel_bias_kernel(table_ref, bucket_ref, out_ref, *, far_bucket, tile_buckets):
    h = pl.program_id(0)
    far = table_ref[far_bucket, h]
    t = bucket_ref.shape[-1]
    half = t // 2
    for tile, buckets in enumerate(tile_buckets):
        bucket = bucket_ref[tile]
        acc = jnp.zeros(bucket.shape, F32)
        for b in buckets:
            acc = jnp.where(bucket == b, table_ref[b, h] - far, acc)
        bias = acc * LOG2E
        for part in range(4):
            src = (part // 2) * half
            out_ref[tile, :, part * half:(part + 1) * half] = bias[:, src:src + half]


def _prev_bias_region():
    t = ATTN_TILE
    key = np.arange(t, dtype=np.int64)[:, None]
    query = np.arange(t, dtype=np.int64)[None, :]
    near = _rel_bucket(key - t - query) != REL_BUCKETS // 2 - 1
    first_row = int(np.nonzero(near.any(axis=1))[0].min()) // 8 * 8
    last_query = int(np.nonzero(near.any(axis=0))[0].max())
    n_cols = t if last_query < t // 2 else 2 * t
    return first_row, n_cols


def _rel_bias_tiles(rel_bias):
    t = ATTN_TILE
    assert t > REL_MAX_DIST
    heads = rel_bias.shape[1]
    key = np.arange(t, dtype=np.int64)[:, None]
    query = np.arange(t, dtype=np.int64)[None, :]
    bucket = np.stack([_rel_bucket(key - t - query), _rel_bucket(key - query)])
    far_bucket = REL_BUCKETS // 2 - 1
    tile_buckets = tuple(tuple(int(b) for b in np.unique(tile) if b != far_bucket) for tile in bucket)
    bucket = jnp.asarray(bucket)
    kern = functools.partial(_rel_bias_kernel, far_bucket=far_bucket, tile_buckets=tile_buckets)
    return pl.pallas_call(
        kern,
        grid=(heads,),
        in_specs=[pl.BlockSpec(memory_space=pltpu.SMEM), pl.BlockSpec((2, t, t), lambda h: (0, 0, 0))],
        out_specs=pl.BlockSpec((None, 2, t, 2 * t), lambda h: (h, 0, 0, 0)),
        out_shape=jax.ShapeDtypeStruct((heads, 2, t, 2 * t), F32),
        name="rel_bias",
    )(rel_bias, bucket)


def _attn_schedule(nq, heads):
    tiles = [(h, i, j) for h in range(heads) for i in reversed(range(nq)) for j in range(i + 1)]
    events, slot_of, free, n_slots = [], {}, [], 0
    pv_pos = 0
    for pos in range(2 * len(tiles)):
        if pos < len(tiles):
            h, i, j = tiles[pos]
            if (h, i) not in slot_of:
                if not free:
                    free.append(n_slots)
                    n_slots += 1
                slot_of[h, i] = free.pop(0)
            events.append(("score", h, i, j, slot_of[h, i]))
        if pv_pos < len(tiles):
            h, i, j = tiles[pv_pos]
            scored = min(pos + 1, len(tiles))
            if scored >= min(tiles.index((h, i, i)) + 1 + ATTN_LEAD, len(tiles)):
                events.append(("pv", h, i, j, slot_of[h, i]))
                pv_pos += 1
                if j == i:
                    free.append(slot_of[h, i])
    assert pv_pos == len(tiles)
    return events, n_slots


def _diff_attn_kernel(base_ref, lam_ref, q_ref, k_ref, vt_ref, nb_ref, sg_ref, o_ref, s_ref, *, lambda_init):
    seq = q_ref.shape[0]
    hd = q_ref.shape[1] // ATTN_HEADS
    t = ATTN_TILE
    nq = seq // t
    events, n_slots = _attn_schedule(nq, ATTN_HEADS)
    assert s_ref.shape[0] == n_slots
    base = base_ref[0]
    half = t // 2
    lane = lax.broadcasted_iota(jnp.int32, (half, hd), 1)
    key_pos = lax.broadcasted_iota(jnp.int32, (t, 2 * t), 0)
    col = lax.broadcasted_iota(jnp.int32, (t, 2 * t), 1)
    query_pos = (col // t) * half + col % half
    allowed = key_pos // CHUNK <= query_pos // CHUNK
    lam = (jnp.exp(jnp.sum(lam_ref[0:1, :] * lam_ref[1:2, :], axis=-1, keepdims=True))
           - jnp.exp(jnp.sum(lam_ref[2:3, :] * lam_ref[3:4, :], axis=-1, keepdims=True)) + lambda_init)

    def stacked_q(h, i):
        parts = []
        for part in range(2):
            q = q_ref[i * t + part * half:i * t + (part + 1) * half, h * hd:(h + 1) * hd]
            zero = jnp.zeros_like(q)
            parts += [jnp.where(lane < hd // 2, q, zero), jnp.where(lane >= hd // 2, q, zero)]
        return jnp.concatenate(parts, axis=0)

    def sublane_max(s):
        return jnp.max(s.reshape(s.shape[0] // 8, 8, s.shape[1]), axis=0)

    def score_tile(h, i, j, slot, qq):
        heads = slice(h * hd, (h + 1) * hd)
        if j < i:
            s = _dot_nt(k_ref[j * t:(j + 1) * t, heads], qq)
            if j == i - 1:
                r0, nc = _prev_bias_region()
                corner = s[r0:, :nc] + nb_ref[h, 0, r0:t, 0:nc]
                s = jnp.concatenate([s[:r0], jnp.concatenate([corner, s[r0:, nc:]], axis=1)], axis=0)
            s_ref[base + slot, j] = s
            return sublane_max(s)
        s_top = _dot_nt(k_ref[j * t:j * t + half, heads], qq)
        s_top = jnp.where(allowed[:half], s_top + nb_ref[h, 1, 0:half, :], NEG_INF)
        s_bot = _dot_nt(k_ref[j * t + half:(j + 1) * t, heads], qq[t:])
        s_bot = jnp.where(allowed[half:, t:], s_bot + nb_ref[h, 1, half:t, t:2 * t], NEG_INF)
        s_ref[base + slot, j, 0:half, :] = s_top
        s_ref[base + slot, j, half:t, t:2 * t] = s_bot
        m_top = sublane_max(s_top)
        m_bot = sublane_max(s_bot)
        return jnp.concatenate([m_top[:, :t], jnp.maximum(m_top[:, t:], m_bot)], axis=1)

    def prob_tile(h, i, j, slot):
        if j < i:
            return jnp.exp2(s_ref[base + slot, j] - mx[h, i]).astype(BF16)
        p_top = jnp.exp2(s_ref[base + slot, j, 0:half, :] - mx[h, i]).astype(BF16)
        p_bot = jnp.exp2(s_ref[base + slot, j, half:t, t:2 * t] - mx[h, i][:, t:]).astype(BF16)
        p_bot = jnp.concatenate([jnp.zeros((half, t), BF16), p_bot], axis=1)
        return jnp.concatenate([p_top, p_bot], axis=0)

    def finish(h, i, acc):
        o_maps = acc[:hd] / acc[hd:hd + 1]
        o = jnp.concatenate([o_maps[:, part * t:part * t + half] - lam * o_maps[:, part * t + half:(part + 1) * t]
                             for part in range(2)], axis=1).T
        o = _rms(o, sg_ref[...]) * (1.0 - lambda_init)
        o_ref[i * t:(i + 1) * t, h * hd:(h + 1) * hd] = o.astype(o_ref.dtype)

    qq, mx, acc = {}, {}, {}
    for kind, h, i, j, slot in events:
        if kind == "score":
            if j == 0:
                qq[h, i] = stacked_q(h, i)
            tile_max = score_tile(h, i, j, slot, qq[h, i])
            mx[h, i] = tile_max if j == 0 else jnp.maximum(mx[h, i], tile_max)
        else:
            if j == 0:
                mx[h, i] = jnp.max(mx[h, i], axis=0, keepdims=True)
            vt = vt_ref[h * (hd + ONES_ROWS):(h + 1) * (hd + ONES_ROWS), j * t:(j + 1) * t]
            pv = _dot(vt, prob_tile(h, i, j, slot))
            acc[h, i] = pv if j == 0 else acc[h, i] + pv
            if j == i:
                finish(h, i, acc.pop((h, i)))


def _diff_attn(lam_vecs, q, k, vt, near_bias, subln_g, *, batch, seq, lambda_init):
    n, width = q.shape
    t = ATTN_TILE
    hd = width // DIFF_HEADS
    nq = seq // t
    q3, k3 = (a.reshape(batch, seq, width) for a in (q, k))
    head_block = pl.BlockSpec((None, seq, ATTN_HEADS * hd), lambda g, b: (b, 0, g))
    vt_block = pl.BlockSpec((None, ATTN_HEADS * (hd + ONES_ROWS), seq), lambda g, b: (b, g, 0))
    n_slots = _attn_schedule(nq, ATTN_HEADS)[1]
    kern = functools.partial(_diff_attn_kernel, lambda_init=lambda_init)
    out = pl.pallas_call(
        kern,
        grid=(DIFF_HEADS // ATTN_HEADS, batch),
        in_specs=[
            pl.BlockSpec(memory_space=pltpu.SMEM),
            pl.BlockSpec(lam_vecs.shape, lambda g, b: (0, 0)),
            head_block, head_block, vt_block,
            pl.BlockSpec((ATTN_HEADS, 2, t, 2 * t), lambda g, b: (g, 0, 0, 0)),
            pl.BlockSpec(subln_g.shape, lambda g, b: (0, 0)),
        ],
        out_specs=head_block,
        out_shape=jax.ShapeDtypeStruct((batch, seq, width), BF16),
        scratch_shapes=[pltpu.VMEM((n_slots, nq, t, 2 * t), F32)],
        compiler_params=pltpu.CompilerParams(dimension_semantics=("arbitrary",) * 2, vmem_limit_bytes=VMEM_LIMIT),
        name="diff_attn",
    )(jnp.zeros((1,), jnp.int32), lam_vecs, q3, k3, vt, near_bias, subln_g)
    return out.reshape(n, width)


def _mem_kv_kernel(mem_ref, g_ref, w_ref, kv_ref):
    mn = _rms(mem_ref[...], g_ref[...]).astype(BF16)
    kv_ref[...] = _dot(mn, w_ref[...].astype(BF16)).astype(BF16)


def _mem_kv(mem2, g, w):
    n, d = mem2.shape
    rows = math.gcd(n, MEM_TILE)
    return pl.pallas_call(
        _mem_kv_kernel,
        grid=(n // rows,),
        in_specs=[pl.BlockSpec((rows, d), lambda i: (i, 0)), _resident(g.shape), _resident(w.shape)],
        out_specs=pl.BlockSpec((rows, w.shape[1]), lambda i: (i, 0)),
        out_shape=jax.ShapeDtypeStruct((n, w.shape[1]), BF16),
        compiler_params=pltpu.CompilerParams(dimension_semantics=("arbitrary",), vmem_limit_bytes=VMEM_LIMIT),
        name="mem_kv",
    )(mem2, g, w)


def _merge_cross_kernel(x_ref, o_ref, gaya_ref, gb_ref, wb_ref, wmo_ref, g_ref, wq_ref, ck_ref, cv_ref, wo_ref,
                        out_ref, *, q_scale):
    tm, d = x_ref.shape
    y_b = _dot(o_ref[...], wb_ref[...].astype(BF16))
    merged = gaya_ref[...].astype(F32) + gb_ref[...].astype(F32) * y_b
    h = x_ref[...] + _dot(merged.astype(BF16), wmo_ref[...].astype(BF16))

    hn = _rms(h, g_ref[...]).astype(BF16)
    cq = (_dot(hn, wq_ref[...].astype(BF16)) * q_scale).astype(BF16)
    hd = d // CROSS_HEADS
    cols = [slice(a * hd, (a + 1) * hd) for a in range(CROSS_HEADS)]
    scores = [_dot_nt(cq[:, c], ck_ref[:, c]) for c in cols]
    probs = [jnp.exp2(s - jnp.max(s, axis=-1, keepdims=True)) for s in scores]
    heads = [_dot(p.astype(BF16), cv_ref[:, c]) / jnp.sum(p, axis=-1, keepdims=True) for p, c in zip(probs, cols)]
    o = jnp.concatenate(heads, axis=-1).astype(BF16)
    out_ref[...] = h + _dot(o, wo_ref[...].astype(BF16))


def _merge_cross(x2, o, gaya, gb, wb, wmo, g, wq, ckv, wo, *, seq, mem_len):
    n, d = x2.shape
    tm = TOKEN_TILE
    tiles_per_seq = seq // tm
    row = lambda c: pl.BlockSpec((tm, c), lambda i: (i, 0))
    kern = functools.partial(_merge_cross_kernel, q_scale=(d // CROSS_HEADS) ** -0.5 * LOG2E)
    return pl.pallas_call(
        kern,
        grid=(n // tm,),
        in_specs=[row(d), row(o.shape[1]), row(d), row(d), _resident(wb.shape), _resident(wmo.shape),
                  _resident(g.shape), _resident(wq.shape),
                  pl.BlockSpec((mem_len, d), lambda i: (i // tiles_per_seq, 0)),
                  pl.BlockSpec((mem_len, d), lambda i: (i // tiles_per_seq, 1)),
                  _resident(wo.shape)],
        out_specs=row(d),
        out_shape=jax.ShapeDtypeStruct((n, d), F32),
        compiler_params=pltpu.CompilerParams(dimension_semantics=("arbitrary",), vmem_limit_bytes=VMEM_LIMIT),
        name="merge_cross",
    )(x2, o, gaya, gb, wb, wmo, g, wq, ckv, ckv, wo)


def _mlp_kernel(h_ref, g_ref, w1_ref, w2_ref, gf_ref, out_ref, *, final_norm):
    h = h_ref[...]
    hn = _rms(h, g_ref[...]).astype(BF16)
    acc = h
    for c in range(w1_ref.shape[1] // FF_CHUNK):
        cols = slice(c * FF_CHUNK, (c + 1) * FF_CHUNK)
        a = jnp.maximum(_dot(hn, w1_ref[:, cols].astype(BF16)), 0.0)
        acc = acc + _dot((a * a).astype(BF16), w2_ref[cols, :].astype(BF16))
    out_ref[...] = _rms(acc, gf_ref[...]) if final_norm else acc


def _mlp(h, g, w1, w2, gf, *, final_norm):
    n, d = h.shape
    tm = MLP_TILE
    row = pl.BlockSpec((tm, d), lambda i: (i, 0))
    kern = functools.partial(_mlp_kernel, final_norm=final_norm)
    return pl.pallas_call(
        kern,
        grid=(n // tm,),
        in_specs=[row, _resident(g.shape), _resident(w1.shape), _resident(w2.shape), _resident(gf.shape)],
        out_specs=row,
        out_shape=jax.ShapeDtypeStruct((n, d), F32),
        compiler_params=pltpu.CompilerParams(dimension_semantics=("arbitrary",), vmem_limit_bytes=VMEM_LIMIT),
        name="mlp",
    )(h, g, w1, w2, gf)


def kernel(x, mem, norm_mix_g, w_in, w_pool_group, pool_scale, w_a_proj, lambda_q1, lambda_k1, lambda_q2, lambda_k2, subln_g, rel_bias, w_b_proj, w_gate, b_gate, w_out, norm_cross_g, norm_mem_g, w_cq, w_ckv, w_co, norm_mlp_g, w_ff1, w_ff2, final_norm_g):
    batch, seq, d = x.shape
    mem_len = mem.shape[1]
    depth = w_in.shape[0]
    pool_width = w_a_proj.shape[1]
    assert pool_width == d and seq % TOKEN_TILE == 0 and seq % ATTN_TILE == 0
    row = lambda a: a.reshape(1, -1).astype(F32)

    near_bias = _rel_bias_tiles(rel_bias.astype(F32))
    h = x.reshape(batch * seq, d)
    mem2 = mem.reshape(batch * mem_len, d)
    for l in range(depth):
        lambda_init = LAMBDA_INIT_BASE - LAMBDA_INIT_AMP * math.exp(-LAMBDA_INIT_RATE * l)
        q, k, vt, gaya, gb = _mixer_front(
            h, row(norm_mix_g[l]), w_in[l], w_gate[l], row(b_gate[l]), w_pool_group[l], row(pool_scale[l]), w_a_proj[l],
            seq=seq, q_scale=DIFF_HEAD_DIM ** -0.5 * LOG2E)
        lam_vecs = jnp.stack([lambda_q1[l], lambda_k1[l], lambda_q2[l], lambda_k2[l]]).astype(F32)
        o = _diff_attn(lam_vecs, q, k, vt, near_bias, row(subln_g[l]), batch=batch, seq=seq, lambda_init=lambda_init)
        ckv = _mem_kv(mem2, row(norm_mem_g[l]), w_ckv[l])
        h = _merge_cross(h, o, gaya, gb, w_b_proj[l], w_out[l], row(norm_cross_g[l]), w_cq[l], ckv, w_co[l],
                         seq=seq, mem_len=mem_len)
        h = _mlp(h, row(norm_mlp_g[l]), w_ff1[l], w_ff2[l], row(final_norm_g), final_norm=(l == depth - 1))
    return h.reshape(batch, seq, d)
```

```python
import functools
import math

import jax
import jax.numpy as jnp
import numpy as np
from jax import lax
from jax.experimental import pallas as pl
from jax.experimental.pallas import tpu as pltpu

CHUNK = 64
POOL_WINDOWS = (2, 4, 8, 16)
DIFF_HEADS = 8
DIFF_HEAD_DIM = 64
LAMBDA_INIT_BASE = 0.8
LAMBDA_INIT_AMP = 0.6
LAMBDA_INIT_RATE = 0.3
REL_BUCKETS = 32
REL_MAX_DIST = 128
CROSS_HEADS = 4
EPS = 1e-6
NEG_INF = -1e30
LOG2E = math.log2(math.e)

POOL_HALO = 16
TOKEN_TILE = 512
MLP_TILE = 512
MEM_TILE = 1024
ATTN_TILE = 256
ATTN_HEADS = 2
ONES_ROWS = 16
ATTN_LEAD = 2
FF_CHUNK = 1024
VMEM_LIMIT = 56 * 1024 * 1024

BF16 = jnp.bfloat16
F32 = jnp.float32


def _resident(shape):
    n = len(shape)
    return pl.BlockSpec(shape, lambda *_: (0,) * n, pipeline_mode=pl.Buffered(1))


def _rms(x, g):
    return x * lax.rsqrt(jnp.mean(x * x, axis=-1, keepdims=True) + EPS) * g


def _dot(a, b):
    return jnp.dot(a, b, preferred_element_type=F32)


def _dot_nt(a, b):
    return lax.dot_general(a, b, (((1,), (1,)), ((), ())), preferred_element_type=F32)


def _mixer_front_kernel(x_ref, g_ref, win_ref, wg_ref, bg_ref, wp_ref, ps_ref, wa_ref,
                        q_ref, k_ref, vt_ref, gaya_ref, gb_ref, ubuf_ref, ga_ref, *, tiles_per_seq, q_scale):
    tm, d = x_ref.shape
    i = pl.program_id(0)
    tile_in_seq = i % tiles_per_seq

    @pl.when(tile_in_seq == 0)
    def _():
        ubuf_ref[0:POOL_HALO, :] = jnp.zeros((POOL_HALO, d), F32)

    xn = _rms(x_ref[...], g_ref[...]).astype(BF16)

    def w_in(lo, hi):
        return win_ref[:, lo:hi].astype(BF16)

    ubuf_ref[POOL_HALO:POOL_HALO + tm, :] = _dot(xn, w_in(0, d))

    width = q_ref.shape[1]
    q_ref[...] = (_dot(xn, w_in(d, d + width)) * q_scale).astype(BF16)
    k_ref[...] = _dot(xn, w_in(d + width, d + 2 * width)).astype(BF16)
    v = _dot(xn, w_in(d + 2 * width, d + 3 * width))
    hd = width // DIFF_HEADS
    for h in range(DIFF_HEADS):
        r0 = h * (hd + ONES_ROWS)
        vt_ref[r0:r0 + hd, :] = v[:, h * hd:(h + 1) * hd].T.astype(BF16)
        vt_ref[r0 + hd:r0 + hd + ONES_ROWS, :] = jnp.ones((ONES_ROWS, tm), BF16)
    z = _dot(xn, wg_ref[...].astype(BF16)) + bg_ref[...]
    gates = 1.0 / (1.0 + jnp.exp(-z))
    ga_ref[...] = gates[:, :d]
    gb_ref[...] = gates[:, d:].astype(gb_ref.dtype)

    pos = tile_in_seq * tm + lax.broadcasted_iota(jnp.int32, (tm, 1), 0)
    gd = d // len(POOL_WINDOWS)
    mapped = []
    for g, w in enumerate(POOL_WINDOWS):
        cols = slice(g * gd, (g + 1) * gd)
        u_g = ubuf_ref[POOL_HALO:POOL_HALO + tm, cols]
        acc = u_g
        for j in range(1, w):
            acc = acc + ubuf_ref[POOL_HALO - j:POOL_HALO - j + tm, cols]
        inv_count = 1.0 / jnp.minimum(pos + 1, w).astype(F32)
        pooled = acc * inv_count - u_g
        mapped.append(_dot(pooled.astype(BF16), wp_ref[g].astype(BF16)))
    ubuf_ref[0:POOL_HALO, :] = ubuf_ref[tm:tm + POOL_HALO, :]
    y = jnp.concatenate(mapped, axis=-1) * ps_ref[...]
    gaya_ref[...] = (ga_ref[...] * _dot(y.astype(BF16), wa_ref[...].astype(BF16))).astype(gaya_ref.dtype)


def _mixer_front(x2, g, win, wg, bg, wp, ps, wa, *, seq, q_scale):
    n, d = x2.shape
    tm = TOKEN_TILE
    width = (win.shape[1] - d) // 3
    row = lambda c: pl.BlockSpec((tm, c), lambda i: (i, 0))
    tiles_per_seq = seq // tm
    vt_rows = width + DIFF_HEADS * ONES_ROWS
    kern = functools.partial(_mixer_front_kernel, tiles_per_seq=tiles_per_seq, q_scale=q_scale)
    return pl.pallas_call(
        kern,
        grid=(n // tm,),
        in_specs=[row(d), _resident(g.shape), _resident(win.shape), _resident(wg.shape), _resident(bg.shape),
                  _resident(wp.shape), _resident(ps.shape), _resident(wa.shape)],
        out_specs=[row(width), row(width),
                   pl.BlockSpec((None, vt_rows, tm), lambda i: (i // tiles_per_seq, 0, i % tiles_per_seq)),
                   row(d), row(d)],
        out_shape=[jax.ShapeDtypeStruct((n, width), BF16)] * 2
        + [jax.ShapeDtypeStruct((n // seq, vt_rows, seq), BF16)] + [jax.ShapeDtypeStruct((n, d), BF16)] * 2,
        scratch_shapes=[pltpu.VMEM((POOL_HALO + tm, d), F32), pltpu.VMEM((tm, d), F32)],
        compiler_params=pltpu.CompilerParams(dimension_semantics=("arbitrary",), vmem_limit_bytes=VMEM_LIMIT),
        name="mixer_front",
    )(x2, g, win, wg, bg, wp, ps, wa)


def _rel_bucket(rel):
    nb = REL_BUCKETS // 2
    max_exact = nb // 2
    n = np.abs(rel)
    scaled = np.log(np.maximum(n, 1) / max_exact) / math.log(REL_MAX_DIST / max_exact) * (nb - max_exact)
    nearest = np.rint(scaled)
    scaled = np.where(np.abs(scaled - nearest) < 1e-9, nearest, scaled)
    large = np.minimum(max_exact + np.floor(scaled).astype(np.int64), nb - 1)
    return ((rel > 0) * nb + np.where(n < max_exact, n, large)).astype(np.int32)


def _rel_bias_kernel(table_ref, bucket_ref, out_ref, *, far_bucket, tile_buckets):
    h = pl.program_id(0)
    far = table_ref[far_bucket, h]
    t = bucket_ref.shape[-1]
    half = t // 2
    for tile, buckets in enumerate(tile_buckets):
        bucket = bucket_ref[tile]
        acc = jnp.zeros(bucket.shape, F32)
        for b in buckets:
            acc = jnp.where(bucket == b, table_ref[b, h] - far, acc)
        bias = acc * LOG2E
        for part in range(4):
            src = (part // 2) * half
            out_ref[tile, :, part * half:(part + 1) * half] = bias[:, src:src + half]


def _prev_bias_region():
    t = ATTN_TILE
    key = np.arange(t, dtype=np.int64)[:, None]
    query = np.arange(t, dtype=np.int64)[None, :]
    near = _rel_bucket(key - t - query) != REL_BUCKETS // 2 - 1
    first_row = int(np.nonzero(near.any(axis=1))[0].min()) // 8 * 8
    last_query = int(np.nonzero(near.any(axis=0))[0].max())
    n_cols = t if last_query < t // 2 else 2 * t
    return first_row, n_cols


def _rel_bias_tiles(rel_bias):
    t = ATTN_TILE
    assert t > REL_MAX_DIST
    heads = rel_bias.shape[1]
    key = np.arange(t, dtype=np.int64)[:, None]
    query = np.arange(t, dtype=np.int64)[None, :]
    bucket = np.stack([_rel_bucket(key - t - query), _rel_bucket(key - query)])
    far_bucket = REL_BUCKETS // 2 - 1
    tile_buckets = tuple(tuple(int(b) for b in np.unique(tile) if b != far_bucket) for tile in bucket)
    bucket = jnp.asarray(bucket)
    kern = functools.partial(_rel_bias_kernel, far_bucket=far_bucket, tile_buckets=tile_buckets)
    return pl.pallas_call(
        kern,
        grid=(heads,),
        in_specs=[pl.BlockSpec(memory_space=pltpu.SMEM), pl.BlockSpec((2, t, t), lambda h: (0, 0, 0))],
        out_specs=pl.BlockSpec((None, 2, t, 2 * t), lambda h: (h, 0, 0, 0)),
        out_shape=jax.ShapeDtypeStruct((heads, 2, t, 2 * t), F32),
        name="rel_bias",
    )(rel_bias, bucket)


def _attn_schedule(nq, heads):
    tiles = [(h, i, j) for h in range(heads) for i in reversed(range(nq)) for j in range(i + 1)]
    events, slot_of, free, n_slots = [], {}, [], 0
    pv_pos = 0
    for pos in range(2 * len(tiles)):
        if pos < len(tiles):
            h, i, j = tiles[pos]
            if (h, i) not in slot_of:
                if not free:
                    free.append(n_slots)
                    n_slots += 1
                slot_of[h, i] = free.pop(0)
            events.append(("score", h, i, j, slot_of[h, i]))
        if pv_pos < len(tiles):
            h, i, j = tiles[pv_pos]
            scored = min(pos + 1, len(tiles))
            if scored >= min(tiles.index((h, i, i)) + 1 + ATTN_LEAD, len(tiles)):
                events.append(("pv", h, i, j, slot_of[h, i]))
                pv_pos += 1
                if j == i:
                    free.append(slot_of[h, i])
    assert pv_pos == len(tiles)
    return events, n_slots


def _diff_attn_kernel(base_ref, lam_ref, q_ref, k_ref, vt_ref, nb_ref, sg_ref, o_ref, s_ref, *, lambda_init):
    seq = q_ref.shape[0]
    hd = q_ref.shape[1] // ATTN_HEADS
    t = ATTN_TILE
    nq = seq // t
    events, n_slots = _attn_schedule(nq, ATTN_HEADS)
    assert s_ref.shape[0] == n_slots
    base = base_ref[0]
    half = t // 2
    lane = lax.broadcasted_iota(jnp.int32, (half, hd), 1)
    key_pos = lax.broadcasted_iota(jnp.int32, (t, 2 * t), 0)
    col = lax.broadcasted_iota(jnp.int32, (t, 2 * t), 1)
    query_pos = (col // t) * half + col % half
    allowed = key_pos // CHUNK <= query_pos // CHUNK
    lam = (jnp.exp(jnp.sum(lam_ref[0:1, :] * lam_ref[1:2, :], axis=-1, keepdims=True))
           - jnp.exp(jnp.sum(lam_ref[2:3, :] * lam_ref[3:4, :], axis=-1, keepdims=True)) + lambda_init)

    def stacked_q(h, i):
        parts = []
        for part in range(2):
            q = q_ref[i * t + part * half:i * t + (part + 1) * half, h * hd:(h + 1) * hd]
            zero = jnp.zeros_like(q)
            parts += [jnp.where(lane < hd // 2, q, zero), jnp.where(lane >= hd // 2, q, zero)]
        return jnp.concatenate(parts, axis=0)

    def sublane_max(s):
        return jnp.max(s.reshape(s.shape[0] // 8, 8, s.shape[1]), axis=0)

    def score_tile(h, i, j, slot, qq):
        heads = slice(h * hd, (h + 1) * hd)
        if j < i:
            s = _dot_nt(k_ref[j * t:(j + 1) * t, heads], qq)
            if j == i - 1:
                r0, nc = _prev_bias_region()
                corner = s[r0:, :nc] + nb_ref[h, 0, r0:t, 0:nc]
                s = jnp.concatenate([s[:r0], jnp.concatenate([corner, s[r0:, nc:]], axis=1)], axis=0)
            s_ref[base + slot, j] = s
            return sublane_max(s)
        s_top = _dot_nt(k_ref[j * t:j * t + half, heads], qq)
        s_top = jnp.where(allowed[:half], s_top + nb_ref[h, 1, 0:half, :], NEG_INF)
        s_bot = _dot_nt(k_ref[j * t + half:(j + 1) * t, heads], qq[t:])
        s_bot = jnp.where(allowed[half:, t:], s_bot + nb_ref[h, 1, half:t, t:2 * t], NEG_INF)
        s_ref[base + slot, j, 0:half, :] = s_top
        s_ref[base + slot, j, half:t, t:2 * t] = s_bot
        m_top = sublane_max(s_top)
        m_bot = sublane_max(s_bot)
        return jnp.concatenate([m_top[:, :t], jnp.maximum(m_top[:, t:], m_bot)], axis=1)

    def prob_tile(h, i, j, slot):
        if j < i:
            return jnp.exp2(s_ref[base + slot, j] - mx[h, i]).astype(BF16)
        p_top = jnp.exp2(s_ref[base + slot, j, 0:half, :] - mx[h, i]).astype(BF16)
        p_bot = jnp.exp2(s_ref[base + slot, j, half:t, t:2 * t] - mx[h, i][:, t:]).astype(BF16)
        p_bot = jnp.concatenate([jnp.zeros((half, t), BF16), p_bot], axis=1)
        return jnp.concatenate([p_top, p_bot], axis=0)

    def finish(h, i, acc):
        o_maps = acc[:hd] / acc[hd:hd + 1]
        o = jnp.concatenate([o_maps[:, part * t:part * t + half] - lam * o_maps[:, part * t + half:(part + 1) * t]
                             for part in range(2)], axis=1).T
        o = _rms(o, sg_ref[...]) * (1.0 - lambda_init)
        o_ref[i * t:(i + 1) * t, h * hd:(h + 1) * hd] = o.astype(o_ref.dtype)

    qq, mx, acc = {}, {}, {}
    for kind, h, i, j, slot in events:
        if kind == "score":
            if j == 0:
                qq[h, i] = stacked_q(h, i)
            tile_max = score_tile(h, i, j, slot, qq[h, i])
            mx[h, i] = tile_max if j == 0 else jnp.maximum(mx[h, i], tile_max)
        else:
            if j == 0:
                mx[h, i] = jnp.max(mx[h, i], axis=0, keepdims=True)
            vt = vt_ref[h * (hd + ONES_ROWS):(h + 1) * (hd + ONES_ROWS), j * t:(j + 1) * t]
            pv = _dot(vt, prob_tile(h, i, j, slot))
            acc[h, i] = pv if j == 0 else acc[h, i] + pv
            if j == i:
                finish(h, i, acc.pop((h, i)))


def _diff_attn(lam_vecs, q, k, vt, near_bias, subln_g, *, batch, seq, lambda_init):
    n, width = q.shape
    t = ATTN_TILE
    hd = width // DIFF_HEADS
    nq = seq // t
    q3, k3 = (a.reshape(batch, seq, width) for a in (q, k))
    head_block = pl.BlockSpec((None, seq, ATTN_HEADS * hd), lambda g, b: (b, 0, g))
    vt_block = pl.BlockSpec((None, ATTN_HEADS * (hd + ONES_ROWS), seq), lambda g, b: (b, g, 0))
    n_slots = _attn_schedule(nq, ATTN_HEADS)[1]
    kern = functools.partial(_diff_attn_kernel, lambda_init=lambda_init)
    out = pl.pallas_call(
        kern,
        grid=(DIFF_HEADS // ATTN_HEADS, batch),
        in_specs=[
            pl.BlockSpec(memory_space=pltpu.SMEM),
            pl.BlockSpec(lam_vecs.shape, lambda g, b: (0, 0)),
            head_block, head_block, vt_block,
            pl.BlockSpec((ATTN_HEADS, 2, t, 2 * t), lambda g, b: (g, 0, 0, 0)),
            pl.BlockSpec(subln_g.shape, lambda g, b: (0, 0)),
        ],
        out_specs=head_block,
        out_shape=jax.ShapeDtypeStruct((batch, seq, width), BF16),
        scratch_shapes=[pltpu.VMEM((n_slots, nq, t, 2 * t), F32)],
        compiler_params=pltpu.CompilerParams(dimension_semantics=("arbitrary",) * 2, vmem_limit_bytes=VMEM_LIMIT),
        name="diff_attn",
    )(jnp.zeros((1,), jnp.int32), lam_vecs, q3, k3, vt, near_bias, subln_g)
    return out.reshape(n, width)


def _mem_kv_kernel(mem_ref, g_ref, w_ref, kv_ref):
    mn = _rms(mem_ref[...], g_ref[...]).astype(BF16)
    kv_ref[...] = _dot(mn, w_ref[...].astype(BF16)).astype(BF16)


def _mem_kv(mem2, g, w):
    n, d = mem2.shape
    rows = math.gcd(n, MEM_TILE)
    return pl.pallas_call(
        _mem_kv_kernel,
        grid=(n // rows,),
        in_specs=[pl.BlockSpec((rows, d), lambda i: (i, 0)), _resident(g.shape), _resident(w.shape)],
        out_specs=pl.BlockSpec((rows, w.shape[1]), lambda i: (i, 0)),
        out_shape=jax.ShapeDtypeStruct((n, w.shape[1]), BF16),
        compiler_params=pltpu.CompilerParams(dimension_semantics=("arbitrary",), vmem_limit_bytes=VMEM_LIMIT),
        name="mem_kv",
    )(mem2, g, w)


def _merge_cross_kernel(x_ref, o_ref, gaya_ref, gb_ref, wb_ref, wmo_ref, g_ref, wq_ref, ck_ref, cv_ref, wo_ref,
                        out_ref, *, q_scale):
    tm, d = x_ref.shape
    y_b = _dot(o_ref[...], wb_ref[...].astype(BF16))
    merged = gaya_ref[...].astype(F32) + gb_ref[...].astype(F32) * y_b
    h = x_ref[...] + _dot(merged.astype(BF16), wmo_ref[...].astype(BF16))

    hn = _rms(h, g_ref[...]).astype(BF16)
    cq = (_dot(hn, wq_ref[...].astype(BF16)) * q_scale).astype(BF16)
    hd = d // CROSS_HEADS
    cols = [slice(a * hd, (a + 1) * hd) for a in range(CROSS_HEADS)]
    scores = [_dot_nt(cq[:, c], ck_ref[:, c]) for c in cols]
    probs = [jnp.exp2(s - jnp.max(s, axis=-1, keepdims=True)) for s in scores]
    heads = [_dot(p.astype(BF16), cv_ref[:, c]) / jnp.sum(p, axis=-1, keepdims=True) for p, c in zip(probs, cols)]
    o = jnp.concatenate(heads, axis=-1).astype(BF16)
    out_ref[...] = h + _dot(o, wo_ref[...].astype(BF16))


def _merge_cross(x2, o, gaya, gb, wb, wmo, g, wq, ckv, wo, *, seq, mem_len):
    n, d = x2.shape
    tm = TOKEN_TILE
    tiles_per_seq = seq // tm
    row = lambda c: pl.BlockSpec((tm, c), lambda i: (i, 0))
    kern = functools.partial(_merge_cross_kernel, q_scale=(d // CROSS_HEADS) ** -0.5 * LOG2E)
    return pl.pallas_call(
        kern,
        grid=(n // tm,),
        in_specs=[row(d), row(o.shape[1]), row(d), row(d), _resident(wb.shape), _resident(wmo.shape),
                  _resident(g.shape), _resident(wq.shape),
                  pl.BlockSpec((mem_len, d), lambda i: (i // tiles_per_seq, 0)),
                  pl.BlockSpec((mem_len, d), lambda i: (i // tiles_per_seq, 1)),
                  _resident(wo.shape)],
        out_specs=row(d),
        out_shape=jax.ShapeDtypeStruct((n, d), F32),
        compiler_params=pltpu.CompilerParams(dimension_semantics=("arbitrary",), vmem_limit_bytes=VMEM_LIMIT),
        name="merge_cross",
    )(x2, o, gaya, gb, wb, wmo, g, wq, ckv, ckv, wo)


def _mlp_kernel(h_ref, g_ref, w1_ref, w2_ref, gf_ref, out_ref, *, final_norm):
    h = h_ref[...]
    hn = _rms(h, g_ref[...]).astype(BF16)
    acc = h
    for c in range(w1_ref.shape[1] // FF_CHUNK):
        cols = slice(c * FF_CHUNK, (c + 1) * FF_CHUNK)
        a = jnp.maximum(_dot(hn, w1_ref[:, cols].astype(BF16)), 0.0)
        acc = acc + _dot((a * a).astype(BF16), w2_ref[cols, :].astype(BF16))
    out_ref[...] = _rms(acc, gf_ref[...]) if final_norm else acc


def _mlp(h, g, w1, w2, gf, *, final_norm):
    n, d = h.shape
    tm = MLP_TILE
    row = pl.BlockSpec((tm, d), lambda i: (i, 0))
    kern = functools.partial(_mlp_kernel, final_norm=final_norm)
    return pl.pallas_call(
        kern,
        grid=(n // tm,),
        in_specs=[row, _resident(g.shape), _resident(w1.shape), _resident(w2.shape), _resident(gf.shape)],
        out_specs=row,
        out_shape=jax.ShapeDtypeStruct((n, d), F32),
        compiler_params=pltpu.CompilerParams(dimension_semantics=("arbitrary",), vmem_limit_bytes=VMEM_LIMIT),
        name="mlp",
    )(h, g, w1, w2, gf)


def kernel(x, mem, norm_mix_g, w_in, w_pool_group, pool_scale, w_a_proj, lambda_q1, lambda_k1, lambda_q2, lambda_k2, subln_g, rel_bias, w_b_proj, w_gate, b_gate, w_out, norm_cross_g, norm_mem_g, w_cq, w_ckv, w_co, norm_mlp_g, w_ff1, w_ff2, final_norm_g):
    batch, seq, d = x.shape
    mem_len = mem.shape[1]
    depth = w_in.shape[0]
    pool_width = w_a_proj.shape[1]
    assert pool_width == d and seq % TOKEN_TILE == 0 and seq % ATTN_TILE == 0
    row = lambda a: a.reshape(1, -1).astype(F32)

    near_bias = _rel_bias_tiles(rel_bias.astype(F32))
    h = x.reshape(batch * seq, d)
    mem2 = mem.reshape(batch * mem_len, d)
    for l in range(depth):
        lambda_init = LAMBDA_INIT_BASE - LAMBDA_INIT_AMP * math.exp(-LAMBDA_INIT_RATE * l)
        q, k, vt, gaya, gb = _mixer_front(
            h, row(norm_mix_g[l]), w_in[l], w_gate[l], row(b_gate[l]), w_pool_group[l], row(pool_scale[l]), w_a_proj[l],
            seq=seq, q_scale=DIFF_HEAD_DIM ** -0.5 * LOG2E)
        lam_vecs = jnp.stack([lambda_q1[l], lambda_k1[l], lambda_q2[l], lambda_k2[l]]).astype(F32)
        o = _diff_attn(lam_vecs, q, k, vt, near_bias, row(subln_g[l]), batch=batch, seq=seq, lambda_init=lambda_init)
        ckv = _mem_kv(mem2, row(norm_mem_g[l]), w_ckv[l])
        h = _merge_cross(h, o, gaya, gb, w_b_proj[l], w_out[l], row(norm_cross_g[l]), w_cq[l], ckv, w_co[l],
                         seq=seq, mem_len=mem_len)
        h = _mlp(h, row(norm_mlp_g[l]), w_ff1[l], w_ff2[l], row(final_norm_g), final_norm=(l == depth - 1))
    return h.reshape(batch, seq, d)
```

```python
import functools
import math

import jax
import jax.numpy as jnp
import numpy as np
from jax import lax
from jax.experimental import pallas as pl
from jax.experimental.pallas import tpu as pltpu

CHUNK = 64
POOL_WINDOWS = (2, 4, 8, 16)
DIFF_HEADS = 8
DIFF_HEAD_DIM = 64
LAMBDA_INIT_BASE = 0.8
LAMBDA_INIT_AMP = 0.6
LAMBDA_INIT_RATE = 0.3
REL_BUCKETS = 32
REL_MAX_DIST = 128
CROSS_HEADS = 4
EPS = 1e-6
NEG_INF = -1e30
LOG2E = math.log2(math.e)

POOL_HALO = 16
TOKEN_TILE = 512
MLP_TILE = 512
MEM_TILE = 1024
ATTN_TILE = 256
ATTN_HEADS = 2
ONES_ROWS = 16
ATTN_LEAD = 2
FF_CHUNK = 1024
VMEM_LIMIT = 56 * 1024 * 1024

BF16 = jnp.bfloat16
F32 = jnp.float32


def _resident(shape):
    n = len(shape)
    return pl.BlockSpec(shape, lambda *_: (0,) * n, pipeline_mode=pl.Buffered(1))


def _rms(x, g):
    return x * lax.rsqrt(jnp.mean(x * x, axis=-1, keepdims=True) + EPS) * g


def _dot(a, b):
    return jnp.dot(a, b, preferred_element_type=F32)


def _dot_nt(a, b):
    return lax.dot_general(a, b, (((1,), (1,)), ((), ())), preferred_element_type=F32)


def _mixer_front_kernel(x_ref, g_ref, win_ref, wg_ref, bg_ref, wp_ref, ps_ref, wa_ref,
                        q_ref, k_ref, vt_ref, gaya_ref, gb_ref, ubuf_ref, ga_ref, *, tiles_per_seq, q_scale):
    tm, d = x_ref.shape
    i = pl.program_id(0)
    tile_in_seq = i % tiles_per_seq

    @pl.when(tile_in_seq == 0)
    def _():
        ubuf_ref[0:POOL_HALO, :] = jnp.zeros((POOL_HALO, d), F32)

    xn = _rms(x_ref[...], g_ref[...]).astype(BF16)

    def w_in(lo, hi):
        return win_ref[:, lo:hi].astype(BF16)

    ubuf_ref[POOL_HALO:POOL_HALO + tm, :] = _dot(xn, w_in(0, d))

    width = q_ref.shape[1]
    q_ref[...] = (_dot(xn, w_in(d, d + width)) * q_scale).astype(BF16)
    k_ref[...] = _dot(xn, w_in(d + width, d + 2 * width)).astype(BF16)
    v = _dot(xn, w_in(d + 2 * width, d + 3 * width))
    hd = width // DIFF_HEADS
    for h in range(DIFF_HEADS):
        r0 = h * (hd + ONES_ROWS)
        vt_ref[r0:r0 + hd, :] = v[:, h * hd:(h + 1) * hd].T.astype(BF16)
        vt_ref[r0 + hd:r0 + hd + ONES_ROWS, :] = jnp.ones((ONES_ROWS, tm), BF16)
    z = _dot(xn, wg_ref[...].astype(BF16)) + bg_ref[...]
    gates = 1.0 / (1.0 + jnp.exp(-z))
    ga_ref[...] = gates[:, :d]
    gb_ref[...] = gates[:, d:].astype(gb_ref.dtype)

    pos = tile_in_seq * tm + lax.broadcasted_iota(jnp.int32, (tm, 1), 0)
    gd = d // len(POOL_WINDOWS)
    mapped = []
    for g, w in enumerate(POOL_WINDOWS):
        cols = slice(g * gd, (g + 1) * gd)
        u_g = ubuf_ref[POOL_HALO:POOL_HALO + tm, cols]
        acc = u_g
        for j in range(1, w):
            acc = acc + ubuf_ref[POOL_HALO - j:POOL_HALO - j + tm, cols]
        inv_count = 1.0 / jnp.minimum(pos + 1, w).astype(F32)
        pooled = acc * inv_count - u_g
        mapped.append(_dot(pooled.astype(BF16), wp_ref[g].astype(BF16)))
    ubuf_ref[0:POOL_HALO, :] = ubuf_ref[tm:tm + POOL_HALO, :]
    y = jnp.concatenate(mapped, axis=-1) * ps_ref[...]
    gaya_ref[...] = (ga_ref[...] * _dot(y.astype(BF16), wa_ref[...].astype(BF16))).astype(gaya_ref.dtype)


def _mixer_front(x2, g, win, wg, bg, wp, ps, wa, *, seq, q_scale):
    n, d = x2.shape
    tm = TOKEN_TILE
    width = (win.shape[1] - d) // 3
    row = lambda c: pl.BlockSpec((tm, c), lambda i: (i, 0))
    tiles_per_seq = seq // tm
    vt_rows = width + DIFF_HEADS * ONES_ROWS
    kern = functools.partial(_mixer_front_kernel, tiles_per_seq=tiles_per_seq, q_scale=q_scale)
    return pl.pallas_call(
        kern,
        grid=(n // tm,),
        in_specs=[row(d), _resident(g.shape), _resident(win.shape), _resident(wg.shape), _resident(bg.shape),
                  _resident(wp.shape), _resident(ps.shape), _resident(wa.shape)],
        out_specs=[row(width), row(width),
                   pl.BlockSpec((None, vt_rows, tm), lambda i: (i // tiles_per_seq, 0, i % tiles_per_seq)),
                   row(d), row(d)],
        out_shape=[jax.ShapeDtypeStruct((n, width), BF16)] * 2
        + [jax.ShapeDtypeStruct((n // seq, vt_rows, seq), BF16)] + [jax.ShapeDtypeStruct((n, d), BF16)] * 2,
        scratch_shapes=[pltpu.VMEM((POOL_HALO + tm, d), F32), pltpu.VMEM((tm, d), F32)],
        compiler_params=pltpu.CompilerParams(dimension_semantics=("arbitrary",), vmem_limit_bytes=VMEM_LIMIT),
        name="mixer_front",
    )(x2, g, win, wg, bg, wp, ps, wa)


def _rel_bucket(rel):
    nb = REL_BUCKETS // 2
    max_exact = nb // 2
    n = np.abs(rel)
    scaled = np.log(np.maximum(n, 1) / max_exact) / math.log(REL_MAX_DIST / max_exact) * (nb - max_exact)
    nearest = np.rint(scaled)
    scaled = np.where(np.abs(scaled - nearest) < 1e-9, nearest, scaled)
    large = np.minimum(max_exact + np.floor(scaled).astype(np.int64), nb - 1)
    return ((rel > 0) * nb + np.where(n < max_exact, n, large)).astype(np.int32)


def _rel_bias_kernel(table_ref, bucket_ref, out_ref, *, far_bucket, tile_buckets):
    h = pl.program_id(0)
    far = table_ref[far_bucket, h]
    t = bucket_ref.shape[-1]
    half = t // 2
    for tile, buckets in enumerate(tile_buckets):
        bucket = bucket_ref[tile]
        acc = jnp.zeros(bucket.shape, F32)
        for b in buckets:
            acc = jnp.where(bucket == b, table_ref[b, h] - far, acc)
        bias = acc * LOG2E
        for part in range(4):
            src = (part // 2) * half
            out_ref[tile, :, part * half:(part + 1) * half] = bias[:, src:src + half]


def _prev_bias_region():
    t = ATTN_TILE
    key = np.arange(t, dtype=np.int64)[:, None]
    query = np.arange(t, dtype=np.int64)[None, :]
    near = _rel_bucket(key - t - query) != REL_BUCKETS // 2 - 1
    first_row = int(np.nonzero(near.any(axis=1))[0].min()) // 8 * 8
    last_query = int(np.nonzero(near.any(axis=0))[0].max())
    n_cols = t if last_query < t // 2 else 2 * t
    return first_row, n_cols


def _rel_bias_tiles(rel_bias):
    t = ATTN_TILE
    assert t > REL_MAX_DIST
    heads = rel_bias.shape[1]
    key = np.arange(t, dtype=np.int64)[:, None]
    query = np.arange(t, dtype=np.int64)[None, :]
    bucket = np.stack([_rel_bucket(key - t - query), _rel_bucket(key - query)])
    far_bucket = REL_BUCKETS // 2 - 1
    tile_buckets = tuple(tuple(int(b) for b in np.unique(tile) if b != far_bucket) for tile in bucket)
    bucket = jnp.asarray(bucket)
    kern = functools.partial(_rel_bias_kernel, far_bucket=far_bucket, tile_buckets=tile_buckets)
    return pl.pallas_call(
        kern,
        grid=(heads,),
        in_specs=[pl.BlockSpec(memory_space=pltpu.SMEM), pl.BlockSpec((2, t, t), lambda h: (0, 0, 0))],
        out_specs=pl.BlockSpec((None, 2, t, 2 * t), lambda h: (h, 0, 0, 0)),
        out_shape=jax.ShapeDtypeStruct((heads, 2, t, 2 * t), F32),
        name="rel_bias",
    )(rel_bias, bucket)


def _attn_schedule(nq, heads):
    tiles = [(h, i, j) for h in range(heads) for i in reversed(range(nq)) for j in range(i + 1)]
    events, slot_of, free, n_slots = [], {}, [], 0
    pv_pos = 0
    for pos in range(2 * len(tiles)):
        if pos < len(tiles):
            h, i, j = tiles[pos]
            if (h, i) not in slot_of:
                if not free:
                    free.append(n_slots)
                    n_slots += 1
                slot_of[h, i] = free.pop(0)
            events.append(("score", h, i, j, slot_of[h, i]))
        if pv_pos < len(tiles):
            h, i, j = tiles[pv_pos]
            scored = min(pos + 1, len(tiles))
            if scored >= min(tiles.index((h, i, i)) + 1 + ATTN_LEAD, len(tiles)):
                events.append(("pv", h, i, j, slot_of[h, i]))
                pv_pos += 1
                if j == i:
                    free.append(slot_of[h, i])
    assert pv_pos == len(tiles)
    return events, n_slots


def _diff_attn_kernel(base_ref, lam_ref, q_ref, k_ref, vt_ref, nb_ref, sg_ref, o_ref, s_ref, *, lambda_init):
    seq = q_ref.shape[0]
    hd = q_ref.shape[1] // ATTN_HEADS
    t = ATTN_TILE
    nq = seq // t
    events, n_slots = _attn_schedule(nq, ATTN_HEADS)
    assert s_ref.shape[0] == n_slots
    base = base_ref[0]
    half = t // 2
    lane = lax.broadcasted_iota(jnp.int32, (half, hd), 1)
    assert half == 2 * CHUNK
    later_chunk = lax.broadcasted_iota(jnp.int32, (CHUNK, t), 1) % half >= CHUNK

    def mask_first_query_chunk(block):
        return jnp.where(later_chunk, block, NEG_INF)
    lam = (jnp.exp(jnp.sum(lam_ref[0:1, :] * lam_ref[1:2, :], axis=-1, keepdims=True))
           - jnp.exp(jnp.sum(lam_ref[2:3, :] * lam_ref[3:4, :], axis=-1, keepdims=True)) + lambda_init)

    def stacked_q(h, i):
        parts = []
        for part in range(2):
            q = q_ref[i * t + part * half:i * t + (part + 1) * half, h * hd:(h + 1) * hd]
            zero = jnp.zeros_like(q)
            parts += [jnp.where(lane < hd // 2, q, zero), jnp.where(lane >= hd // 2, q, zero)]
        return jnp.concatenate(parts, axis=0)

    def sublane_max(s):
        return jnp.max(s.reshape(s.shape[0] // 8, 8, s.shape[1]), axis=0)

    def score_tile(h, i, j, slot, qq):
        heads = slice(h * hd, (h + 1) * hd)
        if j < i:
            s = _dot_nt(k_ref[j * t:(j + 1) * t, heads], qq)
            if j == i - 1:
                r0, nc = _prev_bias_region()
                corner = s[r0:, :nc] + nb_ref[h, 0, r0:t, 0:nc]
                s = jnp.concatenate([s[:r0], jnp.concatenate([corner, s[r0:, nc:]], axis=1)], axis=0)
            s_ref[base + slot, j] = s
            return sublane_max(s)
        s_top = _dot_nt(k_ref[j * t:j * t + half, heads], qq)
        s_top = s_top + nb_ref[h, 1, 0:half, :]
        s_top = jnp.concatenate(
            [s_top[:CHUNK],
             jnp.concatenate([mask_first_query_chunk(s_top[CHUNK:, :t]), s_top[CHUNK:, t:]], axis=1)], axis=0)
        s_bot = _dot_nt(k_ref[j * t + half:(j + 1) * t, heads], qq[t:])
        s_bot = s_bot + nb_ref[h, 1, half:t, t:2 * t]
        s_bot = jnp.concatenate([s_bot[:CHUNK], mask_first_query_chunk(s_bot[CHUNK:])], axis=0)
        s_ref[base + slot, j, 0:half, :] = s_top
        s_ref[base + slot, j, half:t, t:2 * t] = s_bot
        m_top = sublane_max(s_top)
        m_bot = sublane_max(s_bot)
        return jnp.concatenate([m_top[:, :t], jnp.maximum(m_top[:, t:], m_bot)], axis=1)

    def prob_tile(h, i, j, slot):
        if j < i:
            return jnp.exp2(s_ref[base + slot, j] - mx[h, i]).astype(BF16)
        p_top = jnp.exp2(s_ref[base + slot, j, 0:half, :] - mx[h, i]).astype(BF16)
        p_bot = jnp.exp2(s_ref[base + slot, j, half:t, t:2 * t] - mx[h, i][:, t:]).astype(BF16)
        p_bot = jnp.concatenate([jnp.zeros((half, t), BF16), p_bot], axis=1)
        return jnp.concatenate([p_top, p_bot], axis=0)

    def finish(h, i, acc):
        o_maps = acc[:hd] / acc[hd:hd + 1]
        o = jnp.concatenate([o_maps[:, part * t:part * t + half] - lam * o_maps[:, part * t + half:(part + 1) * t]
                             for part in range(2)], axis=1).T
        o = _rms(o, sg_ref[...]) * (1.0 - lambda_init)
        o_ref[i * t:(i + 1) * t, h * hd:(h + 1) * hd] = o.astype(o_ref.dtype)

    qq, mx, acc = {}, {}, {}
    for kind, h, i, j, slot in events:
        if kind == "score":
            if j == 0:
                qq[h, i] = stacked_q(h, i)
            tile_max = score_tile(h, i, j, slot, qq[h, i])
            mx[h, i] = tile_max if j == 0 else jnp.maximum(mx[h, i], tile_max)
        else:
            if j == 0:
                mx[h, i] = jnp.max(mx[h, i], axis=0, keepdims=True)
            vt = vt_ref[h * (hd + ONES_ROWS):(h + 1) * (hd + ONES_ROWS), j * t:(j + 1) * t]
            pv = _dot(vt, prob_tile(h, i, j, slot))
            acc[h, i] = pv if j == 0 else acc[h, i] + pv
            if j == i:
                finish(h, i, acc.pop((h, i)))


def _diff_attn(lam_vecs, q, k, vt, near_bias, subln_g, *, batch, seq, lambda_init):
    n, width = q.shape
    t = ATTN_TILE
    hd = width // DIFF_HEADS
    nq = seq // t
    q3, k3 = (a.reshape(batch, seq, width) for a in (q, k))
    head_block = pl.BlockSpec((None, seq, ATTN_HEADS * hd), lambda g, b: (b, 0, g))
    vt_block = pl.BlockSpec((None, ATTN_HEADS * (hd + ONES_ROWS), seq), lambda g, b: (b, g, 0))
    n_slots = _attn_schedule(nq, ATTN_HEADS)[1]
    kern = functools.partial(_diff_attn_kernel, lambda_init=lambda_init)
    out = pl.pallas_call(
        kern,
        grid=(DIFF_HEADS // ATTN_HEADS, batch),
        in_specs=[
            pl.BlockSpec(memory_space=pltpu.SMEM),
            pl.BlockSpec(lam_vecs.shape, lambda g, b: (0, 0)),
            head_block, head_block, vt_block,
            pl.BlockSpec((ATTN_HEADS, 2, t, 2 * t), lambda g, b: (g, 0, 0, 0)),
            pl.BlockSpec(subln_g.shape, lambda g, b: (0, 0)),
        ],
        out_specs=head_block,
        out_shape=jax.ShapeDtypeStruct((batch, seq, width), BF16),
        scratch_shapes=[pltpu.VMEM((n_slots, nq, t, 2 * t), F32)],
        compiler_params=pltpu.CompilerParams(dimension_semantics=("arbitrary",) * 2, vmem_limit_bytes=VMEM_LIMIT),
        name="diff_attn",
    )(jnp.zeros((1,), jnp.int32), lam_vecs, q3, k3, vt, near_bias, subln_g)
    return out.reshape(n, width)


def _mem_kv_kernel(mem_ref, g_ref, w_ref, kv_ref):
    mn = _rms(mem_ref[...], g_ref[...]).astype(BF16)
    kv_ref[...] = _dot(mn, w_ref[...].astype(BF16)).astype(BF16)


def _mem_kv(mem2, g, w):
    n, d = mem2.shape
    rows = math.gcd(n, MEM_TILE)
    return pl.pallas_call(
        _mem_kv_kernel,
        grid=(n // rows,),
        in_specs=[pl.BlockSpec((rows, d), lambda i: (i, 0)), _resident(g.shape), _resident(w.shape)],
        out_specs=pl.BlockSpec((rows, w.shape[1]), lambda i: (i, 0)),
        out_shape=jax.ShapeDtypeStruct((n, w.shape[1]), BF16),
        compiler_params=pltpu.CompilerParams(dimension_semantics=("arbitrary",), vmem_limit_bytes=VMEM_LIMIT),
        name="mem_kv",
    )(mem2, g, w)


def _merge_cross_kernel(x_ref, o_ref, gaya_ref, gb_ref, wb_ref, wmo_ref, g_ref, wq_ref, ck_ref, cv_ref, wo_ref,
                        out_ref, *, q_scale):
    tm, d = x_ref.shape
    y_b = _dot(o_ref[...], wb_ref[...].astype(BF16))
    merged = gaya_ref[...].astype(F32) + gb_ref[...].astype(F32) * y_b
    h = x_ref[...] + _dot(merged.astype(BF16), wmo_ref[...].astype(BF16))

    hn = _rms(h, g_ref[...]).astype(BF16)
    cq = (_dot(hn, wq_ref[...].astype(BF16)) * q_scale).astype(BF16)
    hd = d // CROSS_HEADS
    cols = [slice(a * hd, (a + 1) * hd) for a in range(CROSS_HEADS)]
    scores = [_dot_nt(cq[:, c], ck_ref[:, c]) for c in cols]
    probs = [jnp.exp2(s - jnp.max(s, axis=-1, keepdims=True)) for s in scores]
    heads = [_dot(p.astype(BF16), cv_ref[:, c]) / jnp.sum(p, axis=-1, keepdims=True) for p, c in zip(probs, cols)]
    o = jnp.concatenate(heads, axis=-1).astype(BF16)
    out_ref[...] = h + _dot(o, wo_ref[...].astype(BF16))


def _merge_cross(x2, o, gaya, gb, wb, wmo, g, wq, ckv, wo, *, seq, mem_len):
    n, d = x2.shape
    tm = TOKEN_TILE
    tiles_per_seq = seq // tm
    row = lambda c: pl.BlockSpec((tm, c), lambda i: (i, 0))
    kern = functools.partial(_merge_cross_kernel, q_scale=(d // CROSS_HEADS) ** -0.5 * LOG2E)
    return pl.pallas_call(
        kern,
        grid=(n // tm,),
        in_specs=[row(d), row(o.shape[1]), row(d), row(d), _resident(wb.shape), _resident(wmo.shape),
                  _resident(g.shape), _resident(wq.shape),
                  pl.BlockSpec((mem_len, d), lambda i: (i // tiles_per_seq, 0)),
                  pl.BlockSpec((mem_len, d), lambda i: (i // tiles_per_seq, 1)),
                  _resident(wo.shape)],
        out_specs=row(d),
        out_shape=jax.ShapeDtypeStruct((n, d), F32),
        compiler_params=pltpu.CompilerParams(dimension_semantics=("arbitrary",), vmem_limit_bytes=VMEM_LIMIT),
        name="merge_cross",
    )(x2, o, gaya, gb, wb, wmo, g, wq, ckv, ckv, wo)


def _mlp_kernel(h_ref, g_ref, w1_ref, w2_ref, gf_ref, out_ref, *, final_norm):
    h = h_ref[...]
    hn = _rms(h, g_ref[...]).astype(BF16)
    acc = h
    for c in range(w1_ref.shape[1] // FF_CHUNK):
        cols = slice(c * FF_CHUNK, (c + 1) * FF_CHUNK)
        a = jnp.maximum(_dot(hn, w1_ref[:, cols].astype(BF16)), 0.0)
        acc = acc + _dot((a * a).astype(BF16), w2_ref[cols, :].astype(BF16))
    out_ref[...] = _rms(acc, gf_ref[...]) if final_norm else acc


def _mlp(h, g, w1, w2, gf, *, final_norm):
    n, d = h.shape
    tm = MLP_TILE
    row = pl.BlockSpec((tm, d), lambda i: (i, 0))
    kern = functools.partial(_mlp_kernel, final_norm=final_norm)
    return pl.pallas_call(
        kern,
        grid=(n // tm,),
        in_specs=[row, _resident(g.shape), _resident(w1.shape), _resident(w2.shape), _resident(gf.shape)],
        out_specs=row,
        out_shape=jax.ShapeDtypeStruct((n, d), F32),
        compiler_params=pltpu.CompilerParams(dimension_semantics=("arbitrary",), vmem_limit_bytes=VMEM_LIMIT),
        name="mlp",
    )(h, g, w1, w2, gf)


def kernel(x, mem, norm_mix_g, w_in, w_pool_group, pool_scale, w_a_proj, lambda_q1, lambda_k1, lambda_q2, lambda_k2, subln_g, rel_bias, w_b_proj, w_gate, b_gate, w_out, norm_cross_g, norm_mem_g, w_cq, w_ckv, w_co, norm_mlp_g, w_ff1, w_ff2, final_norm_g):
    batch, seq, d = x.shape
    mem_len = mem.shape[1]
    depth = w_in.shape[0]
    pool_width = w_a_proj.shape[1]
    assert pool_width == d and seq % TOKEN_TILE == 0 and seq % ATTN_TILE == 0
    row = lambda a: a.reshape(1, -1).astype(F32)

    near_bias = _rel_bias_tiles(rel_bias.astype(F32))
    h = x.reshape(batch * seq, d)
    mem2 = mem.reshape(batch * mem_len, d)
    for l in range(depth):
        lambda_init = LAMBDA_INIT_BASE - LAMBDA_INIT_AMP * math.exp(-LAMBDA_INIT_RATE * l)
        q, k, vt, gaya, gb = _mixer_front(
            h, row(norm_mix_g[l]), w_in[l], w_gate[l], row(b_gate[l]), w_pool_group[l], row(pool_scale[l]), w_a_proj[l],
            seq=seq, q_scale=DIFF_HEAD_DIM ** -0.5 * LOG2E)
        lam_vecs = jnp.stack([lambda_q1[l], lambda_k1[l], lambda_q2[l], lambda_k2[l]]).astype(F32)
        o = _diff_attn(lam_vecs, q, k, vt, near_bias, row(subln_g[l]), batch=batch, seq=seq, lambda_init=lambda_init)
        ckv = _mem_kv(mem2, row(norm_mem_g[l]), w_ckv[l])
        h = _merge_cross(h, o, gaya, gb, w_b_proj[l], w_out[l], row(norm_cross_g[l]), w_cq[l], ckv, w_co[l],
                         seq=seq, mem_len=mem_len)
        h = _mlp(h, row(norm_mlp_g[l]), w_ff1[l], w_ff2[l], row(final_norm_g), final_norm=(l == depth - 1))
    return h.reshape(batch, seq, d)
```

```python
import functools
import math

import jax
import jax.numpy as jnp
import numpy as np
from jax import lax
from jax.experimental import pallas as pl
from jax.experimental.pallas import tpu as pltpu

CHUNK = 64
POOL_WINDOWS = (2, 4, 8, 16)
DIFF_HEADS = 8
DIFF_HEAD_DIM = 64
LAMBDA_INIT_BASE = 0.8
LAMBDA_INIT_AMP = 0.6
LAMBDA_INIT_RATE = 0.3
REL_BUCKETS = 32
REL_MAX_DIST = 128
CROSS_HEADS = 4
EPS = 1e-6
NEG_INF = -1e30
LOG2E = math.log2(math.e)

POOL_HALO = 16
TOKEN_TILE = 512
MLP_TILE = 512
ATTN_TILE = 256
ATTN_HEADS = 2
ONES_ROWS = 16
ATTN_LEAD = 2
FF_CHUNK = 1024
VMEM_LIMIT = 56 * 1024 * 1024

BF16 = jnp.bfloat16
F32 = jnp.float32


def _resident(shape):
    n = len(shape)
    return pl.BlockSpec(shape, lambda *_: (0,) * n, pipeline_mode=pl.Buffered(1))


def _rms(x, g):
    return x * lax.rsqrt(jnp.mean(x * x, axis=-1, keepdims=True) + EPS) * g


def _dot(a, b):
    return jnp.dot(a, b, preferred_element_type=F32)


def _dot_nt(a, b):
    return lax.dot_general(a, b, (((1,), (1,)), ((), ())), preferred_element_type=F32)


def _mixer_front_kernel(x_ref, g_ref, win_ref, wg_ref, bg_ref, wp_ref, ps_ref, wa_ref,
                        q_ref, k_ref, vt_ref, gaya_ref, gb_ref, ubuf_ref, ga_ref, *, tiles_per_seq, q_scale):
    tm, d = x_ref.shape
    i = pl.program_id(0)
    tile_in_seq = i % tiles_per_seq

    @pl.when(tile_in_seq == 0)
    def _():
        ubuf_ref[0:POOL_HALO, :] = jnp.zeros((POOL_HALO, d), F32)

    xn = _rms(x_ref[...], g_ref[...]).astype(BF16)

    def w_in(lo, hi):
        return win_ref[:, lo:hi].astype(BF16)

    ubuf_ref[POOL_HALO:POOL_HALO + tm, :] = _dot(xn, w_in(0, d))

    width = q_ref.shape[1]
    q_ref[...] = (_dot(xn, w_in(d, d + width)) * q_scale).astype(BF16)
    k_ref[...] = _dot(xn, w_in(d + width, d + 2 * width)).astype(BF16)
    v = _dot(xn, w_in(d + 2 * width, d + 3 * width))
    hd = width // DIFF_HEADS
    for h in range(DIFF_HEADS):
        r0 = h * (hd + ONES_ROWS)
        vt_ref[r0:r0 + hd, :] = v[:, h * hd:(h + 1) * hd].T.astype(BF16)
        vt_ref[r0 + hd:r0 + hd + ONES_ROWS, :] = jnp.ones((ONES_ROWS, tm), BF16)
    z = _dot(xn, wg_ref[...].astype(BF16)) + bg_ref[...]
    gates = 1.0 / (1.0 + jnp.exp(-z))
    ga_ref[...] = gates[:, :d]
    gb_ref[...] = gates[:, d:].astype(gb_ref.dtype)

    pos = tile_in_seq * tm + lax.broadcasted_iota(jnp.int32, (tm, 1), 0)
    gd = d // len(POOL_WINDOWS)
    mapped = []
    for g, w in enumerate(POOL_WINDOWS):
        cols = slice(g * gd, (g + 1) * gd)
        u_g = ubuf_ref[POOL_HALO:POOL_HALO + tm, cols]
        acc = u_g
        for j in range(1, w):
            acc = acc + ubuf_ref[POOL_HALO - j:POOL_HALO - j + tm, cols]
        inv_count = 1.0 / jnp.minimum(pos + 1, w).astype(F32)
        pooled = acc * inv_count - u_g
        mapped.append(_dot(pooled.astype(BF16), wp_ref[g].astype(BF16)))
    ubuf_ref[0:POOL_HALO, :] = ubuf_ref[tm:tm + POOL_HALO, :]
    y = jnp.concatenate(mapped, axis=-1) * ps_ref[...]
    gaya_ref[...] = (ga_ref[...] * _dot(y.astype(BF16), wa_ref[...].astype(BF16))).astype(gaya_ref.dtype)


def _mixer_front(x2, g, win, wg, bg, wp, ps, wa, *, seq, q_scale):
    n, d = x2.shape
    tm = TOKEN_TILE
    width = (win.shape[1] - d) // 3
    row = lambda c: pl.BlockSpec((tm, c), lambda i: (i, 0))
    tiles_per_seq = seq // tm
    vt_rows = width + DIFF_HEADS * ONES_ROWS
    kern = functools.partial(_mixer_front_kernel, tiles_per_seq=tiles_per_seq, q_scale=q_scale)
    return pl.pallas_call(
        kern,
        grid=(n // tm,),
        in_specs=[row(d), _resident(g.shape), _resident(win.shape), _resident(wg.shape), _resident(bg.shape),
                  _resident(wp.shape), _resident(ps.shape), _resident(wa.shape)],
        out_specs=[row(width), row(width),
                   pl.BlockSpec((None, vt_rows, tm), lambda i: (i // tiles_per_seq, 0, i % tiles_per_seq)),
                   row(d), row(d)],
        out_shape=[jax.ShapeDtypeStruct((n, width), BF16)] * 2
        + [jax.ShapeDtypeStruct((n // seq, vt_rows, seq), BF16)] + [jax.ShapeDtypeStruct((n, d), BF16)] * 2,
        scratch_shapes=[pltpu.VMEM((POOL_HALO + tm, d), F32), pltpu.VMEM((tm, d), F32)],
        compiler_params=pltpu.CompilerParams(dimension_semantics=("arbitrary",), vmem_limit_bytes=VMEM_LIMIT),
        name="mixer_front",
    )(x2, g, win, wg, bg, wp, ps, wa)


def _rel_bucket(rel):
    nb = REL_BUCKETS // 2
    max_exact = nb // 2
    n = np.abs(rel)
    scaled = np.log(np.maximum(n, 1) / max_exact) / math.log(REL_MAX_DIST / max_exact) * (nb - max_exact)
    nearest = np.rint(scaled)
    scaled = np.where(np.abs(scaled - nearest) < 1e-9, nearest, scaled)
    large = np.minimum(max_exact + np.floor(scaled).astype(np.int64), nb - 1)
    return ((rel > 0) * nb + np.where(n < max_exact, n, large)).astype(np.int32)


def _rel_bias_kernel(table_ref, bucket_ref, out_ref, *, far_bucket, tile_buckets):
    h = pl.program_id(0)
    far = table_ref[far_bucket, h]
    t = bucket_ref.shape[-1]
    half = t // 2
    for tile, buckets in enumerate(tile_buckets):
        bucket = bucket_ref[tile]
        acc = jnp.zeros(bucket.shape, F32)
        for b in buckets:
            acc = jnp.where(bucket == b, table_ref[b, h] - far, acc)
        bias = acc * LOG2E
        for part in range(4):
            src = (part // 2) * half
            out_ref[tile, :, part * half:(part + 1) * half] = bias[:, src:src + half]


def _prev_bias_region():
    t = ATTN_TILE
    key = np.arange(t, dtype=np.int64)[:, None]
    query = np.arange(t, dtype=np.int64)[None, :]
    near = _rel_bucket(key - t - query) != REL_BUCKETS // 2 - 1
    first_row = int(np.nonzero(near.any(axis=1))[0].min()) // 8 * 8
    last_query = int(np.nonzero(near.any(axis=0))[0].max())
    n_cols = t if last_query < t // 2 else 2 * t
    return first_row, n_cols


def _rel_bias_tiles(rel_bias):
    t = ATTN_TILE
    assert t > REL_MAX_DIST
    heads = rel_bias.shape[1]
    key = np.arange(t, dtype=np.int64)[:, None]
    query = np.arange(t, dtype=np.int64)[None, :]
    bucket = np.stack([_rel_bucket(key - t - query), _rel_bucket(key - query)])
    far_bucket = REL_BUCKETS // 2 - 1
    tile_buckets = tuple(tuple(int(b) for b in np.unique(tile) if b != far_bucket) for tile in bucket)
    bucket = jnp.asarray(bucket)
    kern = functools.partial(_rel_bias_kernel, far_bucket=far_bucket, tile_buckets=tile_buckets)
    return pl.pallas_call(
        kern,
        grid=(heads,),
        in_specs=[pl.BlockSpec(memory_space=pltpu.SMEM), pl.BlockSpec((2, t, t), lambda h: (0, 0, 0))],
        out_specs=pl.BlockSpec((None, 2, t, 2 * t), lambda h: (h, 0, 0, 0)),
        out_shape=jax.ShapeDtypeStruct((heads, 2, t, 2 * t), F32),
        name="rel_bias",
    )(rel_bias, bucket)


def _attn_schedule(nq, heads):
    tiles = [(h, i, j) for h in range(heads) for i in reversed(range(nq)) for j in range(i + 1)]
    events, slot_of, free, n_slots = [], {}, [], 0
    pv_pos = 0
    for pos in range(2 * len(tiles)):
        if pos < len(tiles):
            h, i, j = tiles[pos]
            if (h, i) not in slot_of:
                if not free:
                    free.append(n_slots)
                    n_slots += 1
                slot_of[h, i] = free.pop(0)
            events.append(("score", h, i, j, slot_of[h, i]))
        if pv_pos < len(tiles):
            h, i, j = tiles[pv_pos]
            scored = min(pos + 1, len(tiles))
            if scored >= min(tiles.index((h, i, i)) + 1 + ATTN_LEAD, len(tiles)):
                events.append(("pv", h, i, j, slot_of[h, i]))
                pv_pos += 1
                if j == i:
                    free.append(slot_of[h, i])
    assert pv_pos == len(tiles)
    return events, n_slots


def _diff_attn_kernel(base_ref, lam_ref, q_ref, k_ref, vt_ref, nb_ref, sg_ref, o_ref, s_ref, *, lambda_init):
    seq = q_ref.shape[0]
    hd = q_ref.shape[1] // ATTN_HEADS
    t = ATTN_TILE
    nq = seq // t
    events, n_slots = _attn_schedule(nq, ATTN_HEADS)
    assert s_ref.shape[0] == n_slots
    base = base_ref[0]
    half = t // 2
    lane = lax.broadcasted_iota(jnp.int32, (half, hd), 1)
    assert half == 2 * CHUNK
    later_chunk = lax.broadcasted_iota(jnp.int32, (CHUNK, t), 1) % half >= CHUNK

    def mask_first_query_chunk(block):
        return jnp.where(later_chunk, block, NEG_INF)
    lam = (jnp.exp(jnp.sum(lam_ref[0:1, :] * lam_ref[1:2, :], axis=-1, keepdims=True))
           - jnp.exp(jnp.sum(lam_ref[2:3, :] * lam_ref[3:4, :], axis=-1, keepdims=True)) + lambda_init)

    def stacked_q(h, i):
        parts = []
        for part in range(2):
            q = q_ref[i * t + part * half:i * t + (part + 1) * half, h * hd:(h + 1) * hd]
            zero = jnp.zeros_like(q)
            parts += [jnp.where(lane < hd // 2, q, zero), jnp.where(lane >= hd // 2, q, zero)]
        return jnp.concatenate(parts, axis=0)

    def sublane_max(s):
        return jnp.max(s.reshape(s.shape[0] // 8, 8, s.shape[1]), axis=0)

    def score_tile(h, i, j, slot, qq):
        heads = slice(h * hd, (h + 1) * hd)
        if j < i:
            s = _dot_nt(k_ref[j * t:(j + 1) * t, heads], qq)
            if j == i - 1:
                r0, nc = _prev_bias_region()
                corner = s[r0:, :nc] + nb_ref[h, 0, r0:t, 0:nc]
                s = jnp.concatenate([s[:r0], jnp.concatenate([corner, s[r0:, nc:]], axis=1)], axis=0)
            s_ref[base + slot, j] = s
            return sublane_max(s)
        s_top = _dot_nt(k_ref[j * t:j * t + half, heads], qq)
        s_top = s_top + nb_ref[h, 1, 0:half, :]
        s_top = jnp.concatenate(
            [s_top[:CHUNK],
             jnp.concatenate([mask_first_query_chunk(s_top[CHUNK:, :t]), s_top[CHUNK:, t:]], axis=1)], axis=0)
        s_bot = _dot_nt(k_ref[j * t + half:(j + 1) * t, heads], qq[t:])
        s_bot = s_bot + nb_ref[h, 1, half:t, t:2 * t]
        s_bot = jnp.concatenate([s_bot[:CHUNK], mask_first_query_chunk(s_bot[CHUNK:])], axis=0)
        s_ref[base + slot, j, 0:half, :] = s_top
        s_ref[base + slot, j, half:t, t:2 * t] = s_bot
        m_top = sublane_max(s_top)
        m_bot = sublane_max(s_bot)
        return jnp.concatenate([m_top[:, :t], jnp.maximum(m_top[:, t:], m_bot)], axis=1)

    def prob_tile(h, i, j, slot):
        if j < i:
            return jnp.exp2(s_ref[base + slot, j] - mx[h, i]).astype(BF16)
        p_top = jnp.exp2(s_ref[base + slot, j, 0:half, :] - mx[h, i]).astype(BF16)
        p_bot = jnp.exp2(s_ref[base + slot, j, half:t, t:2 * t] - mx[h, i][:, t:]).astype(BF16)
        p_bot = jnp.concatenate([jnp.zeros((half, t), BF16), p_bot], axis=1)
        return jnp.concatenate([p_top, p_bot], axis=0)

    def finish(h, i, acc):
        o_maps = acc[:hd] / acc[hd:hd + 1]
        o = jnp.concatenate([o_maps[:, part * t:part * t + half] - lam * o_maps[:, part * t + half:(part + 1) * t]
                             for part in range(2)], axis=1).T
        o = _rms(o, sg_ref[...]) * (1.0 - lambda_init)
        o_ref[i * t:(i + 1) * t, h * hd:(h + 1) * hd] = o.astype(o_ref.dtype)

    qq, mx, acc = {}, {}, {}
    for kind, h, i, j, slot in events:
        if kind == "score":
            if j == 0:
                qq[h, i] = stacked_q(h, i)
            tile_max = score_tile(h, i, j, slot, qq[h, i])
            mx[h, i] = tile_max if j == 0 else jnp.maximum(mx[h, i], tile_max)
        else:
            if j == 0:
                mx[h, i] = jnp.max(mx[h, i], axis=0, keepdims=True)
            vt = vt_ref[h * (hd + ONES_ROWS):(h + 1) * (hd + ONES_ROWS), j * t:(j + 1) * t]
            pv = _dot(vt, prob_tile(h, i, j, slot))
            acc[h, i] = pv if j == 0 else acc[h, i] + pv
            if j == i:
                finish(h, i, acc.pop((h, i)))


def _diff_attn(lam_vecs, q, k, vt, near_bias, subln_g, *, batch, seq, lambda_init):
    n, width = q.shape
    t = ATTN_TILE
    hd = width // DIFF_HEADS
    nq = seq // t
    q3, k3 = (a.reshape(batch, seq, width) for a in (q, k))
    head_block = pl.BlockSpec((None, seq, ATTN_HEADS * hd), lambda g, b: (b, 0, g))
    vt_block = pl.BlockSpec((None, ATTN_HEADS * (hd + ONES_ROWS), seq), lambda g, b: (b, g, 0))
    n_slots = _attn_schedule(nq, ATTN_HEADS)[1]
    kern = functools.partial(_diff_attn_kernel, lambda_init=lambda_init)
    out = pl.pallas_call(
        kern,
        grid=(DIFF_HEADS // ATTN_HEADS, batch),
        in_specs=[
            pl.BlockSpec(memory_space=pltpu.SMEM),
            pl.BlockSpec(lam_vecs.shape, lambda g, b: (0, 0)),
            head_block, head_block, vt_block,
            pl.BlockSpec((ATTN_HEADS, 2, t, 2 * t), lambda g, b: (g, 0, 0, 0)),
            pl.BlockSpec(subln_g.shape, lambda g, b: (0, 0)),
        ],
        out_specs=head_block,
        out_shape=jax.ShapeDtypeStruct((batch, seq, width), BF16),
        scratch_shapes=[pltpu.VMEM((n_slots, nq, t, 2 * t), F32)],
        compiler_params=pltpu.CompilerParams(dimension_semantics=("arbitrary",) * 2, vmem_limit_bytes=VMEM_LIMIT),
        name="diff_attn",
    )(jnp.zeros((1,), jnp.int32), lam_vecs, q3, k3, vt, near_bias, subln_g)
    return out.reshape(n, width)


def _merge_cross_kernel(x_ref, o_ref, gaya_ref, gb_ref, wb_ref, wmo_ref, g_ref, wq_ref, mem_ref, gm_ref, wkv_ref,
                        wo_ref, out_ref, kv_ref, *, q_scale, tiles_per_seq):
    tm, d = x_ref.shape

    @pl.when(pl.program_id(0) % tiles_per_seq == 0)
    def _():
        mn = _rms(mem_ref[...], gm_ref[...]).astype(BF16)
        kv_ref[...] = _dot(mn, wkv_ref[...].astype(BF16)).astype(BF16)

    y_b = _dot(o_ref[...], wb_ref[...].astype(BF16))
    merged = gaya_ref[...].astype(F32) + gb_ref[...].astype(F32) * y_b
    h = x_ref[...] + _dot(merged.astype(BF16), wmo_ref[...].astype(BF16))

    hn = _rms(h, g_ref[...]).astype(BF16)
    cq = (_dot(hn, wq_ref[...].astype(BF16)) * q_scale).astype(BF16)
    hd = d // CROSS_HEADS
    cols = [slice(a * hd, (a + 1) * hd) for a in range(CROSS_HEADS)]
    scores = [_dot_nt(cq[:, c], kv_ref[:, c]) for c in cols]
    probs = [jnp.exp2(s - jnp.max(s, axis=-1, keepdims=True)) for s in scores]
    heads = [_dot(p.astype(BF16), kv_ref[:, d + c.start:d + c.stop]) / jnp.sum(p, axis=-1, keepdims=True)
             for p, c in zip(probs, cols)]
    o = jnp.concatenate(heads, axis=-1).astype(BF16)
    out_ref[...] = h + _dot(o, wo_ref[...].astype(BF16))


def _merge_cross(x2, o, gaya, gb, wb, wmo, g, wq, mem2, gm, wkv, wo, *, seq, mem_len):
    n, d = x2.shape
    tm = TOKEN_TILE
    tiles_per_seq = seq // tm
    row = lambda c: pl.BlockSpec((tm, c), lambda i: (i, 0))
    kern = functools.partial(_merge_cross_kernel, q_scale=(d // CROSS_HEADS) ** -0.5 * LOG2E,
                             tiles_per_seq=tiles_per_seq)
    return pl.pallas_call(
        kern,
        grid=(n // tm,),
        in_specs=[row(d), row(o.shape[1]), row(d), row(d), _resident(wb.shape), _resident(wmo.shape),
                  _resident(g.shape), _resident(wq.shape),
                  pl.BlockSpec((mem_len, d), lambda i: (i // tiles_per_seq, 0)),
                  _resident(gm.shape), _resident(wkv.shape), _resident(wo.shape)],
        out_specs=row(d),
        out_shape=jax.ShapeDtypeStruct((n, d), F32),
        scratch_shapes=[pltpu.VMEM((mem_len, wkv.shape[1]), BF16)],
        compiler_params=pltpu.CompilerParams(dimension_semantics=("arbitrary",), vmem_limit_bytes=VMEM_LIMIT),
        name="merge_cross",
    )(x2, o, gaya, gb, wb, wmo, g, wq, mem2, gm, wkv, wo)


def _mlp_kernel(h_ref, g_ref, w1_ref, w2_ref, gf_ref, out_ref, *, final_norm):
    h = h_ref[...]
    hn = _rms(h, g_ref[...]).astype(BF16)
    acc = h
    for c in range(w1_ref.shape[1] // FF_CHUNK):
        cols = slice(c * FF_CHUNK, (c + 1) * FF_CHUNK)
        a = jnp.maximum(_dot(hn, w1_ref[:, cols].astype(BF16)), 0.0)
        acc = acc + _dot((a * a).astype(BF16), w2_ref[cols, :].astype(BF16))
    out_ref[...] = _rms(acc, gf_ref[...]) if final_norm else acc


def _mlp(h, g, w1, w2, gf, *, final_norm):
    n, d = h.shape
    tm = MLP_TILE
    row = pl.BlockSpec((tm, d), lambda i: (i, 0))
    kern = functools.partial(_mlp_kernel, final_norm=final_norm)
    return pl.pallas_call(
        kern,
        grid=(n // tm,),
        in_specs=[row, _resident(g.shape), _resident(w1.shape), _resident(w2.shape), _resident(gf.shape)],
        out_specs=row,
        out_shape=jax.ShapeDtypeStruct((n, d), F32),
        compiler_params=pltpu.CompilerParams(dimension_semantics=("arbitrary",), vmem_limit_bytes=VMEM_LIMIT),
        name="mlp",
    )(h, g, w1, w2, gf)


def kernel(x, mem, norm_mix_g, w_in, w_pool_group, pool_scale, w_a_proj, lambda_q1, lambda_k1, lambda_q2, lambda_k2, subln_g, rel_bias, w_b_proj, w_gate, b_gate, w_out, norm_cross_g, norm_mem_g, w_cq, w_ckv, w_co, norm_mlp_g, w_ff1, w_ff2, final_norm_g):
    batch, seq, d = x.shape
    mem_len = mem.shape[1]
    depth = w_in.shape[0]
    pool_width = w_a_proj.shape[1]
    assert pool_width == d and seq % TOKEN_TILE == 0 and seq % ATTN_TILE == 0
    row = lambda a: a.reshape(1, -1).astype(F32)

    near_bias = _rel_bias_tiles(rel_bias.astype(F32))
    h = x.reshape(batch * seq, d)
    mem2 = mem.reshape(batch * mem_len, d)
    for l in range(depth):
        lambda_init = LAMBDA_INIT_BASE - LAMBDA_INIT_AMP * math.exp(-LAMBDA_INIT_RATE * l)
        q, k, vt, gaya, gb = _mixer_front(
            h, row(norm_mix_g[l]), w_in[l], w_gate[l], row(b_gate[l]), w_pool_group[l], row(pool_scale[l]), w_a_proj[l],
            seq=seq, q_scale=DIFF_HEAD_DIM ** -0.5 * LOG2E)
        lam_vecs = jnp.stack([lambda_q1[l], lambda_k1[l], lambda_q2[l], lambda_k2[l]]).astype(F32)
        o = _diff_attn(lam_vecs, q, k, vt, near_bias, row(subln_g[l]), batch=batch, seq=seq, lambda_init=lambda_init)
        h = _merge_cross(h, o, gaya, gb, w_b_proj[l], w_out[l], row(norm_cross_g[l]), w_cq[l], mem2,
                         row(norm_mem_g[l]), w_ckv[l], w_co[l], seq=seq, mem_len=mem_len)
        h = _mlp(h, row(norm_mlp_g[l]), w_ff1[l], w_ff2[l], row(final_norm_g), final_norm=(l == depth - 1))
    return h.reshape(batch, seq, d)
```

```python
import functools
import math

import jax
import jax.numpy as jnp
import numpy as np
from jax import lax
from jax.experimental import pallas as pl
from jax.experimental.pallas import tpu as pltpu

CHUNK = 64
POOL_WINDOWS = (2, 4, 8, 16)
DIFF_HEADS = 8
DIFF_HEAD_DIM = 64
LAMBDA_INIT_BASE = 0.8
LAMBDA_INIT_AMP = 0.6
LAMBDA_INIT_RATE = 0.3
REL_BUCKETS = 32
REL_MAX_DIST = 128
CROSS_HEADS = 4
EPS = 1e-6
NEG_INF = -1e30
LOG2E = math.log2(math.e)

POOL_HALO = 16
TOKEN_TILE = 512
MLP_TILE = 512
ATTN_TILE = 256
ATTN_HEADS = 4
ONES_ROWS = 16
ATTN_LEAD = 2
FF_CHUNK = 1024
VMEM_LIMIT = 56 * 1024 * 1024

BF16 = jnp.bfloat16
F32 = jnp.float32


def _resident(shape):
    n = len(shape)
    return pl.BlockSpec(shape, lambda *_: (0,) * n, pipeline_mode=pl.Buffered(1))


def _rms(x, g):
    return x * lax.rsqrt(jnp.mean(x * x, axis=-1, keepdims=True) + EPS) * g


def _dot(a, b):
    return jnp.dot(a, b, preferred_element_type=F32)


def _dot_nt(a, b):
    return lax.dot_general(a, b, (((1,), (1,)), ((), ())), preferred_element_type=F32)


def _mixer_front_kernel(x_ref, g_ref, win_ref, wg_ref, bg_ref, wp_ref, ps_ref, wa_ref,
                        q_ref, k_ref, vt_ref, gaya_ref, gb_ref, ubuf_ref, ga_ref, *, tiles_per_seq, q_scale):
    tm, d = x_ref.shape
    i = pl.program_id(0)
    tile_in_seq = i % tiles_per_seq

    @pl.when(tile_in_seq == 0)
    def _():
        ubuf_ref[0:POOL_HALO, :] = jnp.zeros((POOL_HALO, d), F32)

    xn = _rms(x_ref[...], g_ref[...]).astype(BF16)

    def w_in(lo, hi):
        return win_ref[:, lo:hi].astype(BF16)

    ubuf_ref[POOL_HALO:POOL_HALO + tm, :] = _dot(xn, w_in(0, d))

    width = q_ref.shape[1]
    q_ref[...] = (_dot(xn, w_in(d, d + width)) * q_scale).astype(BF16)
    k_ref[...] = _dot(xn, w_in(d + width, d + 2 * width)).astype(BF16)
    v = _dot(xn, w_in(d + 2 * width, d + 3 * width))
    hd = width // DIFF_HEADS
    for h in range(DIFF_HEADS):
        r0 = h * (hd + ONES_ROWS)
        vt_ref[r0:r0 + hd, :] = v[:, h * hd:(h + 1) * hd].T.astype(BF16)
        vt_ref[r0 + hd:r0 + hd + ONES_ROWS, :] = jnp.ones((ONES_ROWS, tm), BF16)
    z = _dot(xn, wg_ref[...].astype(BF16)) + bg_ref[...]
    gates = 1.0 / (1.0 + jnp.exp(-z))
    ga_ref[...] = gates[:, :d]
    gb_ref[...] = gates[:, d:].astype(gb_ref.dtype)

    pos = tile_in_seq * tm + lax.broadcasted_iota(jnp.int32, (tm, 1), 0)
    gd = d // len(POOL_WINDOWS)
    mapped = []
    for g, w in enumerate(POOL_WINDOWS):
        cols = slice(g * gd, (g + 1) * gd)
        u_g = ubuf_ref[POOL_HALO:POOL_HALO + tm, cols]
        acc = u_g
        for j in range(1, w):
            acc = acc + ubuf_ref[POOL_HALO - j:POOL_HALO - j + tm, cols]
        inv_count = 1.0 / jnp.minimum(pos + 1, w).astype(F32)
        pooled = acc * inv_count - u_g
        mapped.append(_dot(pooled.astype(BF16), wp_ref[g].astype(BF16)))
    ubuf_ref[0:POOL_HALO, :] = ubuf_ref[tm:tm + POOL_HALO, :]
    y = jnp.concatenate(mapped, axis=-1) * ps_ref[...]
    gaya_ref[...] = (ga_ref[...] * _dot(y.astype(BF16), wa_ref[...].astype(BF16))).astype(gaya_ref.dtype)


def _mixer_front(x2, g, win, wg, bg, wp, ps, wa, *, seq, q_scale):
    n, d = x2.shape
    tm = TOKEN_TILE
    width = (win.shape[1] - d) // 3
    row = lambda c: pl.BlockSpec((tm, c), lambda i: (i, 0))
    tiles_per_seq = seq // tm
    vt_rows = width + DIFF_HEADS * ONES_ROWS
    kern = functools.partial(_mixer_front_kernel, tiles_per_seq=tiles_per_seq, q_scale=q_scale)
    return pl.pallas_call(
        kern,
        grid=(n // tm,),
        in_specs=[row(d), _resident(g.shape), _resident(win.shape), _resident(wg.shape), _resident(bg.shape),
                  _resident(wp.shape), _resident(ps.shape), _resident(wa.shape)],
        out_specs=[row(width), row(width),
                   pl.BlockSpec((None, vt_rows, tm), lambda i: (i // tiles_per_seq, 0, i % tiles_per_seq)),
                   row(d), row(d)],
        out_shape=[jax.ShapeDtypeStruct((n, width), BF16)] * 2
        + [jax.ShapeDtypeStruct((n // seq, vt_rows, seq), BF16)] + [jax.ShapeDtypeStruct((n, d), BF16)] * 2,
        scratch_shapes=[pltpu.VMEM((POOL_HALO + tm, d), F32), pltpu.VMEM((tm, d), F32)],
        compiler_params=pltpu.CompilerParams(dimension_semantics=("arbitrary",), vmem_limit_bytes=VMEM_LIMIT),
        name="mixer_front",
    )(x2, g, win, wg, bg, wp, ps, wa)


def _rel_bucket(rel):
    nb = REL_BUCKETS // 2
    max_exact = nb // 2
    n = np.abs(rel)
    scaled = np.log(np.maximum(n, 1) / max_exact) / math.log(REL_MAX_DIST / max_exact) * (nb - max_exact)
    nearest = np.rint(scaled)
    scaled = np.where(np.abs(scaled - nearest) < 1e-9, nearest, scaled)
    large = np.minimum(max_exact + np.floor(scaled).astype(np.int64), nb - 1)
    return ((rel > 0) * nb + np.where(n < max_exact, n, large)).astype(np.int32)


def _rel_bias_kernel(table_ref, bucket_ref, out_ref, *, far_bucket, tile_buckets):
    h = pl.program_id(0)
    far = table_ref[far_bucket, h]
    t = bucket_ref.shape[-1]
    half = t // 2
    for tile, buckets in enumerate(tile_buckets):
        bucket = bucket_ref[tile]
        acc = jnp.zeros(bucket.shape, F32)
        for b in buckets:
            acc = jnp.where(bucket == b, table_ref[b, h] - far, acc)
        bias = acc * LOG2E
        for part in range(4):
            src = (part // 2) * half
            out_ref[tile, :, part * half:(part + 1) * half] = bias[:, src:src + half]


def _prev_bias_region():
    t = ATTN_TILE
    key = np.arange(t, dtype=np.int64)[:, None]
    query = np.arange(t, dtype=np.int64)[None, :]
    near = _rel_bucket(key - t - query) != REL_BUCKETS // 2 - 1
    first_row = int(np.nonzero(near.any(axis=1))[0].min()) // 8 * 8
    last_query = int(np.nonzero(near.any(axis=0))[0].max())
    n_cols = t if last_query < t // 2 else 2 * t
    return first_row, n_cols


def _rel_bias_tiles(rel_bias):
    t = ATTN_TILE
    assert t > REL_MAX_DIST
    heads = rel_bias.shape[1]
    key = np.arange(t, dtype=np.int64)[:, None]
    query = np.arange(t, dtype=np.int64)[None, :]
    bucket = np.stack([_rel_bucket(key - t - query), _rel_bucket(key - query)])
    far_bucket = REL_BUCKETS // 2 - 1
    tile_buckets = tuple(tuple(int(b) for b in np.unique(tile) if b != far_bucket) for tile in bucket)
    bucket = jnp.asarray(bucket)
    kern = functools.partial(_rel_bias_kernel, far_bucket=far_bucket, tile_buckets=tile_buckets)
    return pl.pallas_call(
        kern,
        grid=(heads,),
        in_specs=[pl.BlockSpec(memory_space=pltpu.SMEM), pl.BlockSpec((2, t, t), lambda h: (0, 0, 0))],
        out_specs=pl.BlockSpec((None, 2, t, 2 * t), lambda h: (h, 0, 0, 0)),
        out_shape=jax.ShapeDtypeStruct((heads, 2, t, 2 * t), F32),
        name="rel_bias",
    )(rel_bias, bucket)


def _attn_schedule(nq, heads):
    tiles = [(h, i, j) for h in range(heads) for i in reversed(range(nq)) for j in range(i + 1)]
    events, slot_of, free, n_slots = [], {}, [], 0
    pv_pos = 0
    for pos in range(2 * len(tiles)):
        if pos < len(tiles):
            h, i, j = tiles[pos]
            if (h, i) not in slot_of:
                if not free:
                    free.append(n_slots)
                    n_slots += 1
                slot_of[h, i] = free.pop(0)
            events.append(("score", h, i, j, slot_of[h, i]))
        if pv_pos < len(tiles):
            h, i, j = tiles[pv_pos]
            scored = min(pos + 1, len(tiles))
            if scored >= min(tiles.index((h, i, i)) + 1 + ATTN_LEAD, len(tiles)):
                events.append(("pv", h, i, j, slot_of[h, i]))
                pv_pos += 1
                if j == i:
                    free.append(slot_of[h, i])
    assert pv_pos == len(tiles)
    return events, n_slots


def _diff_attn_kernel(base_ref, lam_ref, q_ref, k_ref, vt_ref, nb_ref, sg_ref, o_ref, s_ref, *, lambda_init):
    seq = q_ref.shape[0]
    hd = q_ref.shape[1] // ATTN_HEADS
    t = ATTN_TILE
    nq = seq // t
    events, n_slots = _attn_schedule(nq, ATTN_HEADS)
    assert s_ref.shape[0] == n_slots
    base = base_ref[0]
    half = t // 2
    lane = lax.broadcasted_iota(jnp.int32, (half, hd), 1)
    assert half == 2 * CHUNK
    later_chunk = lax.broadcasted_iota(jnp.int32, (CHUNK, t), 1) % half >= CHUNK

    def mask_first_query_chunk(block):
        return jnp.where(later_chunk, block, NEG_INF)
    lam = (jnp.exp(jnp.sum(lam_ref[0:1, :] * lam_ref[1:2, :], axis=-1, keepdims=True))
           - jnp.exp(jnp.sum(lam_ref[2:3, :] * lam_ref[3:4, :], axis=-1, keepdims=True)) + lambda_init)

    def stacked_q(h, i):
        parts = []
        for part in range(2):
            q = q_ref[i * t + part * half:i * t + (part + 1) * half, h * hd:(h + 1) * hd]
            zero = jnp.zeros_like(q)
            parts += [jnp.where(lane < hd // 2, q, zero), jnp.where(lane >= hd // 2, q, zero)]
        return jnp.concatenate(parts, axis=0)

    def sublane_max(s):
        return jnp.max(s.reshape(s.shape[0] // 8, 8, s.shape[1]), axis=0)

    def score_tile(h, i, j, slot, qq):
        heads = slice(h * hd, (h + 1) * hd)
        if j < i:
            s = _dot_nt(k_ref[j * t:(j + 1) * t, heads], qq)
            if j == i - 1:
                r0, nc = _prev_bias_region()
                corner = s[r0:, :nc] + nb_ref[h, 0, r0:t, 0:nc]
                s = jnp.concatenate([s[:r0], jnp.concatenate([corner, s[r0:, nc:]], axis=1)], axis=0)
            s_ref[base + slot, j] = s
            return sublane_max(s)
        s_top = _dot_nt(k_ref[j * t:j * t + half, heads], qq)
        s_top = s_top + nb_ref[h, 1, 0:half, :]
        s_top = jnp.concatenate(
            [s_top[:CHUNK],
             jnp.concatenate([mask_first_query_chunk(s_top[CHUNK:, :t]), s_top[CHUNK:, t:]], axis=1)], axis=0)
        s_bot = _dot_nt(k_ref[j * t + half:(j + 1) * t, heads], qq[t:])
        s_bot = s_bot + nb_ref[h, 1, half:t, t:2 * t]
        s_bot = jnp.concatenate([s_bot[:CHUNK], mask_first_query_chunk(s_bot[CHUNK:])], axis=0)
        s_ref[base + slot, j, 0:half, :] = s_top
        s_ref[base + slot, j, half:t, t:2 * t] = s_bot
        m_top = sublane_max(s_top)
        m_bot = sublane_max(s_bot)
        return jnp.concatenate([m_top[:, :t], jnp.maximum(m_top[:, t:], m_bot)], axis=1)

    def prob_tile(h, i, j, slot):
        if j < i:
            return jnp.exp2(s_ref[base + slot, j] - mx[h, i]).astype(BF16)
        p_top = jnp.exp2(s_ref[base + slot, j, 0:half, :] - mx[h, i]).astype(BF16)
        p_bot = jnp.exp2(s_ref[base + slot, j, half:t, t:2 * t] - mx[h, i][:, t:]).astype(BF16)
        p_bot = jnp.concatenate([jnp.zeros((half, t), BF16), p_bot], axis=1)
        return jnp.concatenate([p_top, p_bot], axis=0)

    def finish(h, i, acc):
        o_maps = acc[:hd] / acc[hd:hd + 1]
        o = jnp.concatenate([o_maps[:, part * t:part * t + half] - lam * o_maps[:, part * t + half:(part + 1) * t]
                             for part in range(2)], axis=1).T
        o = _rms(o, sg_ref[...]) * (1.0 - lambda_init)
        o_ref[i * t:(i + 1) * t, h * hd:(h + 1) * hd] = o.astype(o_ref.dtype)

    qq, mx, acc = {}, {}, {}
    for kind, h, i, j, slot in events:
        if kind == "score":
            if j == 0:
                qq[h, i] = stacked_q(h, i)
            tile_max = score_tile(h, i, j, slot, qq[h, i])
            mx[h, i] = tile_max if j == 0 else jnp.maximum(mx[h, i], tile_max)
        else:
            if j == 0:
                mx[h, i] = jnp.max(mx[h, i], axis=0, keepdims=True)
            vt = vt_ref[h * (hd + ONES_ROWS):(h + 1) * (hd + ONES_ROWS), j * t:(j + 1) * t]
            pv = _dot(vt, prob_tile(h, i, j, slot))
            acc[h, i] = pv if j == 0 else acc[h, i] + pv
            if j == i:
                finish(h, i, acc.pop((h, i)))


def _diff_attn(lam_vecs, q, k, vt, near_bias, subln_g, *, batch, seq, lambda_init):
    n, width = q.shape
    t = ATTN_TILE
    hd = width // DIFF_HEADS
    nq = seq // t
    q3, k3 = (a.reshape(batch, seq, width) for a in (q, k))
    head_block = pl.BlockSpec((None, seq, ATTN_HEADS * hd), lambda g, b: (b, 0, g))
    vt_block = pl.BlockSpec((None, ATTN_HEADS * (hd + ONES_ROWS), seq), lambda g, b: (b, g, 0))
    n_slots = _attn_schedule(nq, ATTN_HEADS)[1]
    kern = functools.partial(_diff_attn_kernel, lambda_init=lambda_init)
    out = pl.pallas_call(
        kern,
        grid=(DIFF_HEADS // ATTN_HEADS, batch),
        in_specs=[
            pl.BlockSpec(memory_space=pltpu.SMEM),
            pl.BlockSpec(lam_vecs.shape, lambda g, b: (0, 0)),
            head_block, head_block, vt_block,
            pl.BlockSpec((ATTN_HEADS, 2, t, 2 * t), lambda g, b: (g, 0, 0, 0)),
            pl.BlockSpec(subln_g.shape, lambda g, b: (0, 0)),
        ],
        out_specs=head_block,
        out_shape=jax.ShapeDtypeStruct((batch, seq, width), BF16),
        scratch_shapes=[pltpu.VMEM((n_slots, nq, t, 2 * t), F32)],
        compiler_params=pltpu.CompilerParams(dimension_semantics=("arbitrary",) * 2, vmem_limit_bytes=VMEM_LIMIT),
        name="diff_attn",
    )(jnp.zeros((1,), jnp.int32), lam_vecs, q3, k3, vt, near_bias, subln_g)
    return out.reshape(n, width)


def _merge_cross_kernel(x_ref, o_ref, gaya_ref, gb_ref, wb_ref, wmo_ref, g_ref, wq_ref, mem_ref, gm_ref, wkv_ref,
                        wo_ref, out_ref, kv_ref, *, q_scale, tiles_per_seq):
    tm, d = x_ref.shape

    @pl.when(pl.program_id(0) % tiles_per_seq == 0)
    def _():
        mn = _rms(mem_ref[...], gm_ref[...]).astype(BF16)
        kv_ref[...] = _dot(mn, wkv_ref[...].astype(BF16)).astype(BF16)

    y_b = _dot(o_ref[...], wb_ref[...].astype(BF16))
    merged = gaya_ref[...].astype(F32) + gb_ref[...].astype(F32) * y_b
    h = x_ref[...] + _dot(merged.astype(BF16), wmo_ref[...].astype(BF16))

    hn = _rms(h, g_ref[...]).astype(BF16)
    cq = (_dot(hn, wq_ref[...].astype(BF16)) * q_scale).astype(BF16)
    hd = d // CROSS_HEADS
    cols = [slice(a * hd, (a + 1) * hd) for a in range(CROSS_HEADS)]
    scores = [_dot_nt(cq[:, c], kv_ref[:, c]) for c in cols]
    probs = [jnp.exp2(s - jnp.max(s, axis=-1, keepdims=True)) for s in scores]
    heads = [_dot(p.astype(BF16), kv_ref[:, d + c.start:d + c.stop]) / jnp.sum(p, axis=-1, keepdims=True)
             for p, c in zip(probs, cols)]
    o = jnp.concatenate(heads, axis=-1).astype(BF16)
    out_ref[...] = h + _dot(o, wo_ref[...].astype(BF16))


def _merge_cross(x2, o, gaya, gb, wb, wmo, g, wq, mem2, gm, wkv, wo, *, seq, mem_len):
    n, d = x2.shape
    tm = TOKEN_TILE
    tiles_per_seq = seq // tm
    row = lambda c: pl.BlockSpec((tm, c), lambda i: (i, 0))
    kern = functools.partial(_merge_cross_kernel, q_scale=(d // CROSS_HEADS) ** -0.5 * LOG2E,
                             tiles_per_seq=tiles_per_seq)
    return pl.pallas_call(
        kern,
        grid=(n // tm,),
        in_specs=[row(d), row(o.shape[1]), row(d), row(d), _resident(wb.shape), _resident(wmo.shape),
                  _resident(g.shape), _resident(wq.shape),
                  pl.BlockSpec((mem_len, d), lambda i: (i // tiles_per_seq, 0)),
                  _resident(gm.shape), _resident(wkv.shape), _resident(wo.shape)],
        out_specs=row(d),
        out_shape=jax.ShapeDtypeStruct((n, d), F32),
        scratch_shapes=[pltpu.VMEM((mem_len, wkv.shape[1]), BF16)],
        compiler_params=pltpu.CompilerParams(dimension_semantics=("arbitrary",), vmem_limit_bytes=VMEM_LIMIT),
        name="merge_cross",
    )(x2, o, gaya, gb, wb, wmo, g, wq, mem2, gm, wkv, wo)


def _mlp_kernel(h_ref, g_ref, w1_ref, w2_ref, gf_ref, out_ref, *, final_norm):
    h = h_ref[...]
    hn = _rms(h, g_ref[...]).astype(BF16)
    acc = h
    for c in range(w1_ref.shape[1] // FF_CHUNK):
        cols = slice(c * FF_CHUNK, (c + 1) * FF_CHUNK)
        a = jnp.maximum(_dot(hn, w1_ref[:, cols].astype(BF16)), 0.0)
        acc = acc + _dot((a * a).astype(BF16), w2_ref[cols, :].astype(BF16))
    out_ref[...] = _rms(acc, gf_ref[...]) if final_norm else acc


def _mlp(h, g, w1, w2, gf, *, final_norm):
    n, d = h.shape
    tm = MLP_TILE
    row = pl.BlockSpec((tm, d), lambda i: (i, 0))
    kern = functools.partial(_mlp_kernel, final_norm=final_norm)
    return pl.pallas_call(
        kern,
        grid=(n // tm,),
        in_specs=[row, _resident(g.shape), _resident(w1.shape), _resident(w2.shape), _resident(gf.shape)],
        out_specs=row,
        out_shape=jax.ShapeDtypeStruct((n, d), F32),
        compiler_params=pltpu.CompilerParams(dimension_semantics=("arbitrary",), vmem_limit_bytes=VMEM_LIMIT),
        name="mlp",
    )(h, g, w1, w2, gf)


def kernel(x, mem, norm_mix_g, w_in, w_pool_group, pool_scale, w_a_proj, lambda_q1, lambda_k1, lambda_q2, lambda_k2, subln_g, rel_bias, w_b_proj, w_gate, b_gate, w_out, norm_cross_g, norm_mem_g, w_cq, w_ckv, w_co, norm_mlp_g, w_ff1, w_ff2, final_norm_g):
    batch, seq, d = x.shape
    mem_len = mem.shape[1]
    depth = w_in.shape[0]
    pool_width = w_a_proj.shape[1]
    assert pool_width == d and seq % TOKEN_TILE == 0 and seq % ATTN_TILE == 0
    row = lambda a: a.reshape(1, -1).astype(F32)

    near_bias = _rel_bias_tiles(rel_bias.astype(F32))
    h = x.reshape(batch * seq, d)
    mem2 = mem.reshape(batch * mem_len, d)
    for l in range(depth):
        lambda_init = LAMBDA_INIT_BASE - LAMBDA_INIT_AMP * math.exp(-LAMBDA_INIT_RATE * l)
        q, k, vt, gaya, gb = _mixer_front(
            h, row(norm_mix_g[l]), w_in[l], w_gate[l], row(b_gate[l]), w_pool_group[l], row(pool_scale[l]), w_a_proj[l],
            seq=seq, q_scale=DIFF_HEAD_DIM ** -0.5 * LOG2E)
        lam_vecs = jnp.stack([lambda_q1[l], lambda_k1[l], lambda_q2[l], lambda_k2[l]]).astype(F32)
        o = _diff_attn(lam_vecs, q, k, vt, near_bias, row(subln_g[l]), batch=batch, seq=seq, lambda_init=lambda_init)
        h = _merge_cross(h, o, gaya, gb, w_b_proj[l], w_out[l], row(norm_cross_g[l]), w_cq[l], mem2,
                         row(norm_mem_g[l]), w_ckv[l], w_co[l], seq=seq, mem_len=mem_len)
        h = _mlp(h, row(norm_mlp_g[l]), w_ff1[l], w_ff2[l], row(final_norm_g), final_norm=(l == depth - 1))
    return h.reshape(batch, seq, d)
```

```python
import functools
import math

import jax
import jax.numpy as jnp
import numpy as np
from jax import lax
from jax.experimental import pallas as pl
from jax.experimental.pallas import tpu as pltpu

CHUNK = 64
POOL_WINDOWS = (2, 4, 8, 16)
DIFF_HEADS = 8
DIFF_HEAD_DIM = 64
LAMBDA_INIT_BASE = 0.8
LAMBDA_INIT_AMP = 0.6
LAMBDA_INIT_RATE = 0.3
REL_BUCKETS = 32
REL_MAX_DIST = 128
CROSS_HEADS = 4
EPS = 1e-6
NEG_INF = -1e30
LOG2E = math.log2(math.e)

POOL_HALO = 16
TOKEN_TILE = 512
MLP_TILE = 512
ATTN_TILE = 256
ATTN_HEADS = 2
ONES_ROWS = 16
ATTN_LEAD = 2
FF_CHUNK = 1024
LANES = 128
VMEM_LIMIT = 56 * 1024 * 1024

BF16 = jnp.bfloat16
F32 = jnp.float32


def _resident(shape):
    n = len(shape)
    return pl.BlockSpec(shape, lambda *_: (0,) * n, pipeline_mode=pl.Buffered(1))


def _rms(x, g):
    return x * lax.rsqrt(jnp.mean(x * x, axis=-1, keepdims=True) + EPS) * g


def _dot(a, b):
    return jnp.dot(a, b, preferred_element_type=F32)


def _dot_nt(a, b):
    return lax.dot_general(a, b, (((1,), (1,)), ((), ())), preferred_element_type=F32)


def _mixer_front_kernel(x_ref, g_ref, win_ref, wg_ref, bg_ref, wp_ref, ps_ref, wa_ref,
                        q_ref, k_ref, vt_ref, gaya_ref, gb_ref, ubuf_ref, ga_ref, *, tiles_per_seq, q_scale):
    tm, d = x_ref.shape
    i = pl.program_id(0)
    tile_in_seq = i % tiles_per_seq

    @pl.when(tile_in_seq == 0)
    def _():
        ubuf_ref[0:POOL_HALO, :] = jnp.zeros((POOL_HALO, d), F32)

    xn = _rms(x_ref[...], g_ref[...]).astype(BF16)

    def w_in(lo, hi):
        return win_ref[:, lo:hi].astype(BF16)

    ubuf_ref[POOL_HALO:POOL_HALO + tm, :] = _dot(xn, w_in(0, d))

    width = q_ref.shape[1]
    q_ref[...] = (_dot(xn, w_in(d, d + width)) * q_scale).astype(BF16)
    k_ref[...] = _dot(xn, w_in(d + width, d + 2 * width)).astype(BF16)
    v = _dot(xn, w_in(d + 2 * width, d + 3 * width))
    hd = width // DIFF_HEADS
    for h in range(DIFF_HEADS):
        r0 = h * (hd + ONES_ROWS)
        vt_ref[r0:r0 + hd, :] = v[:, h * hd:(h + 1) * hd].T.astype(BF16)
        vt_ref[r0 + hd:r0 + hd + ONES_ROWS, :] = jnp.ones((ONES_ROWS, tm), BF16)
    z = _dot(xn, wg_ref[...].astype(BF16)) + bg_ref[...]
    gates = 1.0 / (1.0 + jnp.exp(-z))
    ga_ref[...] = gates[:, :d]
    gb_ref[...] = gates[:, d:].astype(gb_ref.dtype)

    pos = tile_in_seq * tm + lax.broadcasted_iota(jnp.int32, (tm, 1), 0)
    gd = d // len(POOL_WINDOWS)
    mapped = []
    for g, w in enumerate(POOL_WINDOWS):
        cols = slice(g * gd, (g + 1) * gd)
        u_g = ubuf_ref[POOL_HALO:POOL_HALO + tm, cols]
        acc = u_g
        for j in range(1, w):
            acc = acc + ubuf_ref[POOL_HALO - j:POOL_HALO - j + tm, cols]
        inv_count = 1.0 / jnp.minimum(pos + 1, w).astype(F32)
        pooled = acc * inv_count - u_g
        mapped.append(_dot(pooled.astype(BF16), wp_ref[g].astype(BF16)))
    ubuf_ref[0:POOL_HALO, :] = ubuf_ref[tm:tm + POOL_HALO, :]
    y = jnp.concatenate(mapped, axis=-1) * ps_ref[...]
    gaya_ref[...] = (ga_ref[...] * _dot(y.astype(BF16), wa_ref[...].astype(BF16))).astype(gaya_ref.dtype)


def _mixer_front(x2, g, win, wg, bg, wp, ps, wa, *, seq, q_scale):
    n, d = x2.shape
    tm = TOKEN_TILE
    width = (win.shape[1] - d) // 3
    row = lambda c: pl.BlockSpec((tm, c), lambda i: (i, 0))
    tiles_per_seq = seq // tm
    vt_rows = width + DIFF_HEADS * ONES_ROWS
    kern = functools.partial(_mixer_front_kernel, tiles_per_seq=tiles_per_seq, q_scale=q_scale)
    return pl.pallas_call(
        kern,
        grid=(n // tm,),
        in_specs=[row(d), _resident(g.shape), _resident(win.shape), _resident(wg.shape), _resident(bg.shape),
                  _resident(wp.shape), _resident(ps.shape), _resident(wa.shape)],
        out_specs=[row(width), row(width),
                   pl.BlockSpec((None, vt_rows, tm), lambda i: (i // tiles_per_seq, 0, i % tiles_per_seq)),
                   row(d), row(d)],
        out_shape=[jax.ShapeDtypeStruct((n, width), BF16)] * 2
        + [jax.ShapeDtypeStruct((n // seq, vt_rows, seq), BF16)] + [jax.ShapeDtypeStruct((n, d), BF16)] * 2,
        scratch_shapes=[pltpu.VMEM((POOL_HALO + tm, d), F32), pltpu.VMEM((tm, d), F32)],
        compiler_params=pltpu.CompilerParams(dimension_semantics=("arbitrary",), vmem_limit_bytes=VMEM_LIMIT),
        name="mixer_front",
    )(x2, g, win, wg, bg, wp, ps, wa)


def _rel_bucket(rel):
    nb = REL_BUCKETS // 2
    max_exact = nb // 2
    n = np.abs(rel)
    scaled = np.log(np.maximum(n, 1) / max_exact) / math.log(REL_MAX_DIST / max_exact) * (nb - max_exact)
    nearest = np.rint(scaled)
    scaled = np.where(np.abs(scaled - nearest) < 1e-9, nearest, scaled)
    large = np.minimum(max_exact + np.floor(scaled).astype(np.int64), nb - 1)
    return ((rel > 0) * nb + np.where(n < max_exact, n, large)).astype(np.int32)


def _rel_bias_kernel(table_ref, bucket_ref, out_ref, *, far_bucket, tile_buckets):
    h = pl.program_id(0)
    far = table_ref[far_bucket, h]
    t = bucket_ref.shape[-1]
    half = t // 2
    for tile, buckets in enumerate(tile_buckets):
        bucket = bucket_ref[tile]
        acc = jnp.zeros(bucket.shape, F32)
        for b in buckets:
            acc = jnp.where(bucket == b, table_ref[b, h] - far, acc)
        bias = acc * LOG2E
        for part in range(4):
            src = (part // 2) * half
            out_ref[tile, :, part * half:(part + 1) * half] = bias[:, src:src + half]


def _prev_bias_region():
    t = ATTN_TILE
    key = np.arange(t, dtype=np.int64)[:, None]
    query = np.arange(t, dtype=np.int64)[None, :]
    near = _rel_bucket(key - t - query) != REL_BUCKETS // 2 - 1
    first_row = int(np.nonzero(near.any(axis=1))[0].min()) // 8 * 8
    last_query = int(np.nonzero(near.any(axis=0))[0].max())
    n_cols = t if last_query < t // 2 else 2 * t
    return first_row, n_cols


def _rel_bias_tiles(rel_bias):
    t = ATTN_TILE
    assert t > REL_MAX_DIST
    heads = rel_bias.shape[1]
    key = np.arange(t, dtype=np.int64)[:, None]
    query = np.arange(t, dtype=np.int64)[None, :]
    bucket = np.stack([_rel_bucket(key - t - query), _rel_bucket(key - query)])
    far_bucket = REL_BUCKETS // 2 - 1
    tile_buckets = tuple(tuple(int(b) for b in np.unique(tile) if b != far_bucket) for tile in bucket)
    bucket = jnp.asarray(bucket)
    kern = functools.partial(_rel_bias_kernel, far_bucket=far_bucket, tile_buckets=tile_buckets)
    return pl.pallas_call(
        kern,
        grid=(heads,),
        in_specs=[pl.BlockSpec(memory_space=pltpu.SMEM), pl.BlockSpec((2, t, t), lambda h: (0, 0, 0))],
        out_specs=pl.BlockSpec((None, 2, t, 2 * t), lambda h: (h, 0, 0, 0)),
        out_shape=jax.ShapeDtypeStruct((heads, 2, t, 2 * t), F32),
        name="rel_bias",
    )(rel_bias, bucket)


def _attn_schedule(nq, heads):
    tiles = [(h, i, j) for h in range(heads) for i in reversed(range(nq)) for j in range(i + 1)]
    events, slot_of, free, n_slots = [], {}, [], 0
    pv_pos = 0
    for pos in range(2 * len(tiles)):
        if pos < len(tiles):
            h, i, j = tiles[pos]
            if (h, i) not in slot_of:
                if not free:
                    free.append(n_slots)
                    n_slots += 1
                slot_of[h, i] = free.pop(0)
            events.append(("score", h, i, j, slot_of[h, i]))
        if pv_pos < len(tiles):
            h, i, j = tiles[pv_pos]
            scored = min(pos + 1, len(tiles))
            if scored >= min(tiles.index((h, i, i)) + 1 + ATTN_LEAD, len(tiles)):
                events.append(("pv", h, i, j, slot_of[h, i]))
                pv_pos += 1
                if j == i:
                    free.append(slot_of[h, i])
    assert pv_pos == len(tiles)
    return events, n_slots


def _diff_attn_kernel(base_ref, lam_ref, q_ref, k_ref, vt_ref, nb_ref, sg_ref, o_ref, s_ref, *, lambda_init):
    seq = q_ref.shape[0]
    hd = q_ref.shape[1] // ATTN_HEADS
    t = ATTN_TILE
    nq = seq // t
    events, n_slots = _attn_schedule(nq, ATTN_HEADS)
    assert s_ref.shape[0] == n_slots
    base = base_ref[0]
    half = t // 2
    lane = lax.broadcasted_iota(jnp.int32, (half, hd), 1)
    assert half == 2 * CHUNK
    later_chunk = lax.broadcasted_iota(jnp.int32, (CHUNK, t), 1) % half >= CHUNK

    def mask_first_query_chunk(block):
        return jnp.where(later_chunk, block, NEG_INF)
    lam = (jnp.exp(jnp.sum(lam_ref[0:1, :] * lam_ref[1:2, :], axis=-1, keepdims=True))
           - jnp.exp(jnp.sum(lam_ref[2:3, :] * lam_ref[3:4, :], axis=-1, keepdims=True)) + lambda_init)

    def stacked_q(h, i):
        parts = []
        for part in range(2):
            q = q_ref[i * t + part * half:i * t + (part + 1) * half, h * hd:(h + 1) * hd]
            zero = jnp.zeros_like(q)
            parts += [jnp.where(lane < hd // 2, q, zero), jnp.where(lane >= hd // 2, q, zero)]
        return jnp.concatenate(parts, axis=0)

    def sublane_max(s):
        return jnp.max(s.reshape(s.shape[0] // 8, 8, s.shape[1]), axis=0)

    def score_tile(h, i, j, slot, qq):
        heads = slice(h * hd, (h + 1) * hd)
        if j < i:
            s = _dot_nt(k_ref[j * t:(j + 1) * t, heads], qq)
            if j == i - 1:
                r0, nc = _prev_bias_region()
                corner = s[r0:, :nc] + nb_ref[h, 0, r0:t, 0:nc]
                s = jnp.concatenate([s[:r0], jnp.concatenate([corner, s[r0:, nc:]], axis=1)], axis=0)
            s_ref[base + slot, j, :, 0:2 * t] = s
            return sublane_max(s)
        s_top = _dot_nt(k_ref[j * t:j * t + half, heads], qq)
        s_top = s_top + nb_ref[h, 1, 0:half, :]
        s_top = jnp.concatenate(
            [s_top[:CHUNK],
             jnp.concatenate([mask_first_query_chunk(s_top[CHUNK:, :t]), s_top[CHUNK:, t:]], axis=1)], axis=0)
        s_bot = _dot_nt(k_ref[j * t + half:(j + 1) * t, heads], qq[t:])
        s_bot = s_bot + nb_ref[h, 1, half:t, t:2 * t]
        s_bot = jnp.concatenate([s_bot[:CHUNK], mask_first_query_chunk(s_bot[CHUNK:])], axis=0)
        s_ref[base + slot, j, 0:half, 0:2 * t] = s_top
        s_ref[base + slot, j, half:t, t:2 * t] = s_bot
        m_top = sublane_max(s_top)
        m_bot = sublane_max(s_bot)
        return jnp.concatenate([m_top[:, :t], jnp.maximum(m_top[:, t:], m_bot)], axis=1)

    def prob_tile(h, i, j, slot):
        if j < i:
            return jnp.exp2(s_ref[base + slot, j, :, 0:2 * t] - mx[h, i]).astype(BF16)
        p_top = jnp.exp2(s_ref[base + slot, j, 0:half, 0:2 * t] - mx[h, i]).astype(BF16)
        p_bot = jnp.exp2(s_ref[base + slot, j, half:t, t:2 * t] - mx[h, i][:, t:]).astype(BF16)
        p_bot = jnp.concatenate([jnp.zeros((half, t), BF16), p_bot], axis=1)
        return jnp.concatenate([p_top, p_bot], axis=0)

    def finish(h, i, acc):
        o_maps = acc[:hd] / acc[hd:hd + 1]
        o = jnp.concatenate([o_maps[:, part * t:part * t + half] - lam * o_maps[:, part * t + half:(part + 1) * t]
                             for part in range(2)], axis=1).T
        o = _rms(o, sg_ref[...]) * (1.0 - lambda_init)
        o_ref[i * t:(i + 1) * t, h * hd:(h + 1) * hd] = o.astype(o_ref.dtype)

    qq, mx, acc = {}, {}, {}
    for kind, h, i, j, slot in events:
        if kind == "score":
            if j == 0:
                qq[h, i] = stacked_q(h, i)
            tile_max = score_tile(h, i, j, slot, qq[h, i])
            mx[h, i] = tile_max if j == 0 else jnp.maximum(mx[h, i], tile_max)
        else:
            if j == 0:
                mx[h, i] = jnp.max(mx[h, i], axis=0, keepdims=True)
            vt = vt_ref[h * (hd + ONES_ROWS):(h + 1) * (hd + ONES_ROWS), j * t:(j + 1) * t]
            pv = _dot(vt, prob_tile(h, i, j, slot))
            acc[h, i] = pv if j == 0 else acc[h, i] + pv
            if j == i:
                finish(h, i, acc.pop((h, i)))


def _diff_attn(lam_vecs, q, k, vt, near_bias, subln_g, *, batch, seq, lambda_init):
    n, width = q.shape
    t = ATTN_TILE
    hd = width // DIFF_HEADS
    nq = seq // t
    q3, k3 = (a.reshape(batch, seq, width) for a in (q, k))
    head_block = pl.BlockSpec((None, seq, ATTN_HEADS * hd), lambda g, b: (b, 0, g))
    vt_block = pl.BlockSpec((None, ATTN_HEADS * (hd + ONES_ROWS), seq), lambda g, b: (b, g, 0))
    n_slots = _attn_schedule(nq, ATTN_HEADS)[1]
    kern = functools.partial(_diff_attn_kernel, lambda_init=lambda_init)
    out = pl.pallas_call(
        kern,
        grid=(DIFF_HEADS // ATTN_HEADS, batch),
        in_specs=[
            pl.BlockSpec(memory_space=pltpu.SMEM),
            pl.BlockSpec(lam_vecs.shape, lambda g, b: (0, 0)),
            head_block, head_block, vt_block,
            pl.BlockSpec((ATTN_HEADS, 2, t, 2 * t), lambda g, b: (g, 0, 0, 0)),
            pl.BlockSpec(subln_g.shape, lambda g, b: (0, 0)),
        ],
        out_specs=head_block,
        out_shape=jax.ShapeDtypeStruct((batch, seq, width), BF16),
        scratch_shapes=[pltpu.VMEM((n_slots, nq, t, 2 * t + LANES), F32)],
        compiler_params=pltpu.CompilerParams(dimension_semantics=("arbitrary",) * 2, vmem_limit_bytes=VMEM_LIMIT),
        name="diff_attn",
    )(jnp.zeros((1,), jnp.int32), lam_vecs, q3, k3, vt, near_bias, subln_g)
    return out.reshape(n, width)


def _merge_cross_kernel(x_ref, o_ref, gaya_ref, gb_ref, wb_ref, wmo_ref, g_ref, wq_ref, mem_ref, gm_ref, wkv_ref,
                        wo_ref, out_ref, kv_ref, *, q_scale, tiles_per_seq):
    tm, d = x_ref.shape

    @pl.when(pl.program_id(0) % tiles_per_seq == 0)
    def _():
        mn = _rms(mem_ref[...], gm_ref[...]).astype(BF16)
        kv_ref[...] = _dot(mn, wkv_ref[...].astype(BF16)).astype(BF16)

    y_b = _dot(o_ref[...], wb_ref[...].astype(BF16))
    merged = gaya_ref[...].astype(F32) + gb_ref[...].astype(F32) * y_b
    h = x_ref[...] + _dot(merged.astype(BF16), wmo_ref[...].astype(BF16))

    hn = _rms(h, g_ref[...]).astype(BF16)
    cq = (_dot(hn, wq_ref[...].astype(BF16)) * q_scale).astype(BF16)
    hd = d // CROSS_HEADS
    cols = [slice(a * hd, (a + 1) * hd) for a in range(CROSS_HEADS)]
    scores = [_dot_nt(cq[:, c], kv_ref[:, c]) for c in cols]
    probs = [jnp.exp2(s - jnp.max(s, axis=-1, keepdims=True)) for s in scores]
    heads = [_dot(p.astype(BF16), kv_ref[:, d + c.start:d + c.stop]) / jnp.sum(p, axis=-1, keepdims=True)
             for p, c in zip(probs, cols)]
    o = jnp.concatenate(heads, axis=-1).astype(BF16)
    out_ref[...] = h + _dot(o, wo_ref[...].astype(BF16))


def _merge_cross(x2, o, gaya, gb, wb, wmo, g, wq, mem2, gm, wkv, wo, *, seq, mem_len):
    n, d = x2.shape
    tm = TOKEN_TILE
    tiles_per_seq = seq // tm
    row = lambda c: pl.BlockSpec((tm, c), lambda i: (i, 0))
    kern = functools.partial(_merge_cross_kernel, q_scale=(d // CROSS_HEADS) ** -0.5 * LOG2E,
                             tiles_per_seq=tiles_per_seq)
    return pl.pallas_call(
        kern,
        grid=(n // tm,),
        in_specs=[row(d), row(o.shape[1]), row(d), row(d), _resident(wb.shape), _resident(wmo.shape),
                  _resident(g.shape), _resident(wq.shape),
                  pl.BlockSpec((mem_len, d), lambda i: (i // tiles_per_seq, 0)),
                  _resident(gm.shape), _resident(wkv.shape), _resident(wo.shape)],
        out_specs=row(d),
        out_shape=jax.ShapeDtypeStruct((n, d), F32),
        scratch_shapes=[pltpu.VMEM((mem_len, wkv.shape[1]), BF16)],
        compiler_params=pltpu.CompilerParams(dimension_semantics=("arbitrary",), vmem_limit_bytes=VMEM_LIMIT),
        name="merge_cross",
    )(x2, o, gaya, gb, wb, wmo, g, wq, mem2, gm, wkv, wo)


def _mlp_kernel(h_ref, g_ref, w1_ref, w2_ref, gf_ref, out_ref, *, final_norm):
    h = h_ref[...]
    hn = _rms(h, g_ref[...]).astype(BF16)
    acc = h
    for c in range(w1_ref.shape[1] // FF_CHUNK):
        cols = slice(c * FF_CHUNK, (c + 1) * FF_CHUNK)
        a = jnp.maximum(_dot(hn, w1_ref[:, cols].astype(BF16)), 0.0)
        acc = acc + _dot((a * a).astype(BF16), w2_ref[cols, :].astype(BF16))
    out_ref[...] = _rms(acc, gf_ref[...]) if final_norm else acc


def _mlp(h, g, w1, w2, gf, *, final_norm):
    n, d = h.shape
    tm = MLP_TILE
    row = pl.BlockSpec((tm, d), lambda i: (i, 0))
    kern = functools.partial(_mlp_kernel, final_norm=final_norm)
    return pl.pallas_call(
        kern,
        grid=(n // tm,),
        in_specs=[row, _resident(g.shape), _resident(w1.shape), _resident(w2.shape), _resident(gf.shape)],
        out_specs=row,
        out_shape=jax.ShapeDtypeStruct((n, d), F32),
        compiler_params=pltpu.CompilerParams(dimension_semantics=("arbitrary",), vmem_limit_bytes=VMEM_LIMIT),
        name="mlp",
    )(h, g, w1, w2, gf)


def kernel(x, mem, norm_mix_g, w_in, w_pool_group, pool_scale, w_a_proj, lambda_q1, lambda_k1, lambda_q2, lambda_k2, subln_g, rel_bias, w_b_proj, w_gate, b_gate, w_out, norm_cross_g, norm_mem_g, w_cq, w_ckv, w_co, norm_mlp_g, w_ff1, w_ff2, final_norm_g):
    batch, seq, d = x.shape
    mem_len = mem.shape[1]
    depth = w_in.shape[0]
    pool_width = w_a_proj.shape[1]
    assert pool_width == d and seq % TOKEN_TILE == 0 and seq % ATTN_TILE == 0
    row = lambda a: a.reshape(1, -1).astype(F32)

    near_bias = _rel_bias_tiles(rel_bias.astype(F32))
    h = x.reshape(batch * seq, d)
    mem2 = mem.reshape(batch * mem_len, d)
    for l in range(depth):
        lambda_init = LAMBDA_INIT_BASE - LAMBDA_INIT_AMP * math.exp(-LAMBDA_INIT_RATE * l)
        q, k, vt, gaya, gb = _mixer_front(
            h, row(norm_mix_g[l]), w_in[l], w_gate[l], row(b_gate[l]), w_pool_group[l], row(pool_scale[l]), w_a_proj[l],
            seq=seq, q_scale=DIFF_HEAD_DIM ** -0.5 * LOG2E)
        lam_vecs = jnp.stack([lambda_q1[l], lambda_k1[l], lambda_q2[l], lambda_k2[l]]).astype(F32)
        o = _diff_attn(lam_vecs, q, k, vt, near_bias, row(subln_g[l]), batch=batch, seq=seq, lambda_init=lambda_init)
        h = _merge_cross(h, o, gaya, gb, w_b_proj[l], w_out[l], row(norm_cross_g[l]), w_cq[l], mem2,
                         row(norm_mem_g[l]), w_ckv[l], w_co[l], seq=seq, mem_len=mem_len)
        h = _mlp(h, row(norm_mlp_g[l]), w_ff1[l], w_ff2[l], row(final_norm_g), final_norm=(l == depth - 1))
    return h.reshape(batch, seq, d)
```

```python
import functools
import math

import jax
import jax.numpy as jnp
import numpy as np
from jax import lax
from jax.experimental import pallas as pl
from jax.experimental.pallas import tpu as pltpu

CHUNK = 64
POOL_WINDOWS = (2, 4, 8, 16)
DIFF_HEADS = 8
DIFF_HEAD_DIM = 64
LAMBDA_INIT_BASE = 0.8
LAMBDA_INIT_AMP = 0.6
LAMBDA_INIT_RATE = 0.3
REL_BUCKETS = 32
REL_MAX_DIST = 128
CROSS_HEADS = 4
EPS = 1e-6
NEG_INF = -1e30
LOG2E = math.log2(math.e)

POOL_HALO = 16
TOKEN_TILE = 512
MLP_TILE = 512
ATTN_TILE = 256
ATTN_HEADS = 2
ONES_ROWS = 16
ATTN_LEAD = 2
FF_CHUNK = 1024
SUBLANES = 8
VMEM_LIMIT = 56 * 1024 * 1024

BF16 = jnp.bfloat16
F32 = jnp.float32


def _resident(shape):
    n = len(shape)
    return pl.BlockSpec(shape, lambda *_: (0,) * n, pipeline_mode=pl.Buffered(1))


def _rms(x, g):
    return x * lax.rsqrt(jnp.mean(x * x, axis=-1, keepdims=True) + EPS) * g


def _dot(a, b):
    return jnp.dot(a, b, preferred_element_type=F32)


def _dot_nt(a, b):
    return lax.dot_general(a, b, (((1,), (1,)), ((), ())), preferred_element_type=F32)


def _mixer_front_kernel(x_ref, g_ref, win_ref, wg_ref, bg_ref, wp_ref, ps_ref, wa_ref,
                        q_ref, k_ref, vt_ref, gaya_ref, gb_ref, ubuf_ref, ga_ref, *, tiles_per_seq, q_scale):
    tm, d = x_ref.shape
    i = pl.program_id(0)
    tile_in_seq = i % tiles_per_seq

    @pl.when(tile_in_seq == 0)
    def _():
        ubuf_ref[0:POOL_HALO, :] = jnp.zeros((POOL_HALO, d), F32)

    xn = _rms(x_ref[...], g_ref[...]).astype(BF16)

    def w_in(lo, hi):
        return win_ref[:, lo:hi].astype(BF16)

    ubuf_ref[POOL_HALO:POOL_HALO + tm, :] = _dot(xn, w_in(0, d))

    width = q_ref.shape[1]
    q_ref[...] = (_dot(xn, w_in(d, d + width)) * q_scale).astype(BF16)
    k_ref[...] = _dot(xn, w_in(d + width, d + 2 * width)).astype(BF16)
    v = _dot(xn, w_in(d + 2 * width, d + 3 * width))
    hd = width // DIFF_HEADS
    for h in range(DIFF_HEADS):
        r0 = h * (hd + ONES_ROWS)
        vt_ref[r0:r0 + hd, :] = v[:, h * hd:(h + 1) * hd].T.astype(BF16)
        vt_ref[r0 + hd:r0 + hd + ONES_ROWS, :] = jnp.ones((ONES_ROWS, tm), BF16)
    z = _dot(xn, wg_ref[...].astype(BF16)) + bg_ref[...]
    gates = 1.0 / (1.0 + jnp.exp(-z))
    ga_ref[...] = gates[:, :d]
    gb_ref[...] = gates[:, d:].astype(gb_ref.dtype)

    pos = tile_in_seq * tm + lax.broadcasted_iota(jnp.int32, (tm, 1), 0)
    gd = d // len(POOL_WINDOWS)
    mapped = []
    for g, w in enumerate(POOL_WINDOWS):
        cols = slice(g * gd, (g + 1) * gd)
        u_g = ubuf_ref[POOL_HALO:POOL_HALO + tm, cols]
        acc = u_g
        for j in range(1, w):
            acc = acc + ubuf_ref[POOL_HALO - j:POOL_HALO - j + tm, cols]
        inv_count = 1.0 / jnp.minimum(pos + 1, w).astype(F32)
        pooled = acc * inv_count - u_g
        mapped.append(_dot(pooled.astype(BF16), wp_ref[g].astype(BF16)))
    ubuf_ref[0:POOL_HALO, :] = ubuf_ref[tm:tm + POOL_HALO, :]
    y = jnp.concatenate(mapped, axis=-1) * ps_ref[...]
    gaya_ref[...] = (ga_ref[...] * _dot(y.astype(BF16), wa_ref[...].astype(BF16))).astype(gaya_ref.dtype)


def _mixer_front(x2, g, win, wg, bg, wp, ps, wa, *, seq, q_scale):
    n, d = x2.shape
    tm = TOKEN_TILE
    width = (win.shape[1] - d) // 3
    row = lambda c: pl.BlockSpec((tm, c), lambda i: (i, 0))
    tiles_per_seq = seq // tm
    vt_rows = width + DIFF_HEADS * ONES_ROWS
    kern = functools.partial(_mixer_front_kernel, tiles_per_seq=tiles_per_seq, q_scale=q_scale)
    return pl.pallas_call(
        kern,
        grid=(n // tm,),
        in_specs=[row(d), _resident(g.shape), _resident(win.shape), _resident(wg.shape), _resident(bg.shape),
                  _resident(wp.shape), _resident(ps.shape), _resident(wa.shape)],
        out_specs=[row(width), row(width),
                   pl.BlockSpec((None, vt_rows, tm), lambda i: (i // tiles_per_seq, 0, i % tiles_per_seq)),
                   row(d), row(d)],
        out_shape=[jax.ShapeDtypeStruct((n, width), BF16)] * 2
        + [jax.ShapeDtypeStruct((n // seq, vt_rows, seq), BF16)] + [jax.ShapeDtypeStruct((n, d), BF16)] * 2,
        scratch_shapes=[pltpu.VMEM((POOL_HALO + tm, d), F32), pltpu.VMEM((tm, d), F32)],
        compiler_params=pltpu.CompilerParams(dimension_semantics=("arbitrary",), vmem_limit_bytes=VMEM_LIMIT),
        name="mixer_front",
    )(x2, g, win, wg, bg, wp, ps, wa)


def _rel_bucket(rel):
    nb = REL_BUCKETS // 2
    max_exact = nb // 2
    n = np.abs(rel)
    scaled = np.log(np.maximum(n, 1) / max_exact) / math.log(REL_MAX_DIST / max_exact) * (nb - max_exact)
    nearest = np.rint(scaled)
    scaled = np.where(np.abs(scaled - nearest) < 1e-9, nearest, scaled)
    large = np.minimum(max_exact + np.floor(scaled).astype(np.int64), nb - 1)
    return ((rel > 0) * nb + np.where(n < max_exact, n, large)).astype(np.int32)


def _rel_bias_kernel(table_ref, bucket_ref, out_ref, *, far_bucket, tile_buckets):
    h = pl.program_id(0)
    far = table_ref[far_bucket, h]
    t = bucket_ref.shape[-1]
    half = t // 2
    for tile, buckets in enumerate(tile_buckets):
        bucket = bucket_ref[tile]
        acc = jnp.zeros(bucket.shape, F32)
        for b in buckets:
            acc = jnp.where(bucket == b, table_ref[b, h] - far, acc)
        bias = acc * LOG2E
        for part in range(4):
            src = (part // 2) * half
            out_ref[tile, :, part * half:(part + 1) * half] = bias[:, src:src + half]


def _prev_bias_region():
    t = ATTN_TILE
    key = np.arange(t, dtype=np.int64)[:, None]
    query = np.arange(t, dtype=np.int64)[None, :]
    near = _rel_bucket(key - t - query) != REL_BUCKETS // 2 - 1
    first_row = int(np.nonzero(near.any(axis=1))[0].min()) // SUBLANES * SUBLANES
    last_query = int(np.nonzero(near.any(axis=0))[0].max())
    n_cols = t if last_query < t // 2 else 2 * t
    return first_row, n_cols


def _rel_bias_tiles(rel_bias):
    t = ATTN_TILE
    assert t > REL_MAX_DIST
    heads = rel_bias.shape[1]
    key = np.arange(t, dtype=np.int64)[:, None]
    query = np.arange(t, dtype=np.int64)[None, :]
    bucket = np.stack([_rel_bucket(key - t - query), _rel_bucket(key - query)])
    far_bucket = REL_BUCKETS // 2 - 1
    tile_buckets = tuple(tuple(int(b) for b in np.unique(tile) if b != far_bucket) for tile in bucket)
    bucket = jnp.asarray(bucket)
    kern = functools.partial(_rel_bias_kernel, far_bucket=far_bucket, tile_buckets=tile_buckets)
    return pl.pallas_call(
        kern,
        grid=(heads,),
        in_specs=[pl.BlockSpec(memory_space=pltpu.SMEM), pl.BlockSpec((2, t, t), lambda h: (0, 0, 0))],
        out_specs=pl.BlockSpec((None, 2, t, 2 * t), lambda h: (h, 0, 0, 0)),
        out_shape=jax.ShapeDtypeStruct((heads, 2, t, 2 * t), F32),
        name="rel_bias",
    )(rel_bias, bucket)


def _attn_schedule(nq, heads):
    tiles = [(h, i, j) for h in range(heads) for i in reversed(range(nq)) for j in range(i + 1)]
    events, slot_of, free, n_slots = [], {}, [], 0
    pv_pos = 0
    for pos in range(2 * len(tiles)):
        if pos < len(tiles):
            h, i, j = tiles[pos]
            if (h, i) not in slot_of:
                if not free:
                    free.append(n_slots)
                    n_slots += 1
                slot_of[h, i] = free.pop(0)
            events.append(("score", h, i, j, slot_of[h, i]))
        if pv_pos < len(tiles):
            h, i, j = tiles[pv_pos]
            scored = min(pos + 1, len(tiles))
            if scored >= min(tiles.index((h, i, i)) + 1 + ATTN_LEAD, len(tiles)):
                events.append(("pv", h, i, j, slot_of[h, i]))
                pv_pos += 1
                if j == i:
                    free.append(slot_of[h, i])
    assert pv_pos == len(tiles)
    return events, n_slots


def _diff_attn_kernel(base_ref, lam_ref, q_ref, k_ref, vt_ref, nb_ref, sg_ref, o_ref, s_ref, *, lambda_init):
    seq = q_ref.shape[0]
    hd = q_ref.shape[1] // ATTN_HEADS
    t = ATTN_TILE
    nq = seq // t
    events, n_slots = _attn_schedule(nq, ATTN_HEADS)
    assert s_ref.shape[0] == n_slots
    base = base_ref[0]
    half = t // 2
    lane = lax.broadcasted_iota(jnp.int32, (half, hd), 1)
    assert half == 2 * CHUNK
    later_chunk = lax.broadcasted_iota(jnp.int32, (CHUNK, t), 1) % half >= CHUNK

    def mask_first_query_chunk(block):
        return jnp.where(later_chunk, block, NEG_INF)
    lam = (jnp.exp(jnp.sum(lam_ref[0:1, :] * lam_ref[1:2, :], axis=-1, keepdims=True))
           - jnp.exp(jnp.sum(lam_ref[2:3, :] * lam_ref[3:4, :], axis=-1, keepdims=True)) + lambda_init)

    def stacked_q(h, i):
        parts = []
        for part in range(2):
            q = q_ref[i * t + part * half:i * t + (part + 1) * half, h * hd:(h + 1) * hd]
            zero = jnp.zeros_like(q)
            parts += [jnp.where(lane < hd // 2, q, zero), jnp.where(lane >= hd // 2, q, zero)]
        return jnp.concatenate(parts, axis=0)

    def sublane_max(s):
        return jnp.max(s.reshape(s.shape[0] // SUBLANES, SUBLANES, s.shape[1]), axis=0)

    def score_tile(h, i, j, slot, qq):
        heads = slice(h * hd, (h + 1) * hd)
        if j < i:
            s = _dot_nt(k_ref[j * t:(j + 1) * t, heads], qq)
            if j == i - 1:
                r0, nc = _prev_bias_region()
                corner = s[r0:, :nc] + nb_ref[h, 0, r0:t, 0:nc]
                s = jnp.concatenate([s[:r0], jnp.concatenate([corner, s[r0:, nc:]], axis=1)], axis=0)
            s_ref[base + slot, j] = s
            return sublane_max(s)
        s_top = _dot_nt(k_ref[j * t:j * t + half, heads], qq)
        s_top = s_top + nb_ref[h, 1, 0:half, :]
        s_top = jnp.concatenate(
            [s_top[:CHUNK],
             jnp.concatenate([mask_first_query_chunk(s_top[CHUNK:, :t]), s_top[CHUNK:, t:]], axis=1)], axis=0)
        s_bot = _dot_nt(k_ref[j * t + half:(j + 1) * t, heads], qq[t:])
        s_bot = s_bot + nb_ref[h, 1, half:t, t:2 * t]
        s_bot = jnp.concatenate([s_bot[:CHUNK], mask_first_query_chunk(s_bot[CHUNK:])], axis=0)
        s_ref[base + slot, j, 0:half, :] = s_top
        s_ref[base + slot, j, half:t, t:2 * t] = s_bot
        m_top = sublane_max(s_top)
        m_bot = sublane_max(s_bot)
        return jnp.concatenate([m_top[:, :t], jnp.maximum(m_top[:, t:], m_bot)], axis=1)

    def prob_tile(h, i, j, slot):
        if j < i:
            return jnp.exp2(s_ref[base + slot, j] - mx[h, i]).astype(BF16)
        p_top = jnp.exp2(s_ref[base + slot, j, 0:half, :] - mx[h, i]).astype(BF16)
        p_bot = jnp.exp2(s_ref[base + slot, j, half:t, t:2 * t] - mx[h, i][:, t:]).astype(BF16)
        p_bot = jnp.concatenate([jnp.zeros((half, t), BF16), p_bot], axis=1)
        return jnp.concatenate([p_top, p_bot], axis=0)

    def finish(h, i, acc):
        o_maps = acc[:hd] / acc[hd:hd + 1]
        o = jnp.concatenate([o_maps[:, part * t:part * t + half] - lam * o_maps[:, part * t + half:(part + 1) * t]
                             for part in range(2)], axis=1).T
        o = _rms(o, sg_ref[...]) * (1.0 - lambda_init)
        o_ref[i * t:(i + 1) * t, h * hd:(h + 1) * hd] = o.astype(o_ref.dtype)

    qq, mx, acc = {}, {}, {}
    for kind, h, i, j, slot in events:
        if kind == "score":
            if j == 0:
                qq[h, i] = stacked_q(h, i)
            tile_max = score_tile(h, i, j, slot, qq[h, i])
            mx[h, i] = tile_max if j == 0 else jnp.maximum(mx[h, i], tile_max)
        else:
            if j == 0:
                mx[h, i] = jnp.max(mx[h, i], axis=0, keepdims=True)
            vt = vt_ref[h * (hd + ONES_ROWS):(h + 1) * (hd + ONES_ROWS), j * t:(j + 1) * t]
            pv = _dot(vt, prob_tile(h, i, j, slot))
            acc[h, i] = pv if j == 0 else acc[h, i] + pv
            if j == i:
                finish(h, i, acc.pop((h, i)))


def _diff_attn(lam_vecs, q, k, vt, near_bias, subln_g, *, batch, seq, lambda_init):
    n, width = q.shape
    t = ATTN_TILE
    hd = width // DIFF_HEADS
    nq = seq // t
    q3, k3 = (a.reshape(batch, seq, width) for a in (q, k))
    head_block = pl.BlockSpec((None, seq, ATTN_HEADS * hd), lambda g, b: (b, 0, g))
    vt_block = pl.BlockSpec((None, ATTN_HEADS * (hd + ONES_ROWS), seq), lambda g, b: (b, g, 0))
    n_slots = _attn_schedule(nq, ATTN_HEADS)[1]
    kern = functools.partial(_diff_attn_kernel, lambda_init=lambda_init)
    out = pl.pallas_call(
        kern,
        grid=(DIFF_HEADS // ATTN_HEADS, batch),
        in_specs=[
            pl.BlockSpec(memory_space=pltpu.SMEM),
            pl.BlockSpec(lam_vecs.shape, lambda g, b: (0, 0)),
            head_block, head_block, vt_block,
            pl.BlockSpec((ATTN_HEADS, 2, t, 2 * t), lambda g, b: (g, 0, 0, 0)),
            pl.BlockSpec(subln_g.shape, lambda g, b: (0, 0)),
        ],
        out_specs=head_block,
        out_shape=jax.ShapeDtypeStruct((batch, seq, width), BF16),
        scratch_shapes=[pltpu.VMEM((n_slots, nq, t, 2 * t), F32)],
        compiler_params=pltpu.CompilerParams(dimension_semantics=("arbitrary",) * 2, vmem_limit_bytes=VMEM_LIMIT),
        name="diff_attn",
    )(jnp.zeros((1,), jnp.int32), lam_vecs, q3, k3, vt, near_bias, subln_g)
    return out.reshape(n, width)


def _merge_cross_kernel(x_ref, o_ref, gaya_ref, gb_ref, wb_ref, wmo_ref, g_ref, wq_ref, mem_ref, gm_ref, wkv_ref,
                        wo_ref, out_ref, kv_ref, *, q_scale, tiles_per_seq):
    tm, d = x_ref.shape

    @pl.when(pl.program_id(0) % tiles_per_seq == 0)
    def _():
        mn = _rms(mem_ref[...], gm_ref[...]).astype(BF16)
        kv_ref[...] = _dot(mn, wkv_ref[...].astype(BF16)).astype(BF16)

    y_b = _dot(o_ref[...], wb_ref[...].astype(BF16))
    merged = gaya_ref[...].astype(F32) + gb_ref[...].astype(F32) * y_b
    h = x_ref[...] + _dot(merged.astype(BF16), wmo_ref[...].astype(BF16))

    hn = _rms(h, g_ref[...]).astype(BF16)
    cq = (_dot(hn, wq_ref[...].astype(BF16)) * q_scale).astype(BF16)
    hd = d // CROSS_HEADS
    cols = [slice(a * hd, (a + 1) * hd) for a in range(CROSS_HEADS)]
    scores = [_dot_nt(cq[:, c], kv_ref[:, c]) for c in cols]
    probs = [jnp.exp2(s - jnp.max(s, axis=-1, keepdims=True)) for s in scores]
    heads = [_dot(p.astype(BF16), kv_ref[:, d + c.start:d + c.stop]) / jnp.sum(p, axis=-1, keepdims=True)
             for p, c in zip(probs, cols)]
    o = jnp.concatenate(heads, axis=-1).astype(BF16)
    out_ref[...] = h + _dot(o, wo_ref[...].astype(BF16))


def _merge_cross(x2, o, gaya, gb, wb, wmo, g, wq, mem2, gm, wkv, wo, *, seq, mem_len):
    n, d = x2.shape
    tm = TOKEN_TILE
    tiles_per_seq = seq // tm
    row = lambda c: pl.BlockSpec((tm, c), lambda i: (i, 0))
    kern = functools.partial(_merge_cross_kernel, q_scale=(d // CROSS_HEADS) ** -0.5 * LOG2E,
                             tiles_per_seq=tiles_per_seq)
    return pl.pallas_call(
        kern,
        grid=(n // tm,),
        in_specs=[row(d), row(o.shape[1]), row(d), row(d), _resident(wb.shape), _resident(wmo.shape),
                  _resident(g.shape), _resident(wq.shape),
                  pl.BlockSpec((mem_len, d), lambda i: (i // tiles_per_seq, 0)),
                  _resident(gm.shape), _resident(wkv.shape), _resident(wo.shape)],
        out_specs=row(d),
        out_shape=jax.ShapeDtypeStruct((n, d), F32),
        scratch_shapes=[pltpu.VMEM((mem_len, wkv.shape[1]), BF16)],
        compiler_params=pltpu.CompilerParams(dimension_semantics=("arbitrary",), vmem_limit_bytes=VMEM_LIMIT),
        name="merge_cross",
    )(x2, o, gaya, gb, wb, wmo, g, wq, mem2, gm, wkv, wo)


def _mlp_kernel(h_ref, g_ref, w1_ref, w2_ref, gf_ref, out_ref, *, final_norm):
    h = h_ref[...]
    hn = _rms(h, g_ref[...]).astype(BF16)
    acc = h
    for c in range(w1_ref.shape[1] // FF_CHUNK):
        cols = slice(c * FF_CHUNK, (c + 1) * FF_CHUNK)
        a = jnp.maximum(_dot(hn, w1_ref[:, cols].astype(BF16)), 0.0)
        acc = acc + _dot((a * a).astype(BF16), w2_ref[cols, :].astype(BF16))
    out_ref[...] = _rms(acc, gf_ref[...]) if final_norm else acc


def _mlp(h, g, w1, w2, gf, *, final_norm):
    n, d = h.shape
    tm = MLP_TILE
    row = pl.BlockSpec((tm, d), lambda i: (i, 0))
    kern = functools.partial(_mlp_kernel, final_norm=final_norm)
    return pl.pallas_call(
        kern,
        grid=(n // tm,),
        in_specs=[row, _resident(g.shape), _resident(w1.shape), _resident(w2.shape), _resident(gf.shape)],
        out_specs=row,
        out_shape=jax.ShapeDtypeStruct((n, d), F32),
        compiler_params=pltpu.CompilerParams(dimension_semantics=("arbitrary",), vmem_limit_bytes=VMEM_LIMIT),
        name="mlp",
    )(h, g, w1, w2, gf)


def kernel(x, mem, norm_mix_g, w_in, w_pool_group, pool_scale, w_a_proj, lambda_q1, lambda_k1, lambda_q2, lambda_k2, subln_g, rel_bias, w_b_proj, w_gate, b_gate, w_out, norm_cross_g, norm_mem_g, w_cq, w_ckv, w_co, norm_mlp_g, w_ff1, w_ff2, final_norm_g):
    batch, seq, d = x.shape
    mem_len = mem.shape[1]
    depth = w_in.shape[0]
    pool_width = w_a_proj.shape[1]
    assert pool_width == d and seq % TOKEN_TILE == 0 and seq % MLP_TILE == 0 and seq % ATTN_TILE == 0
    row = lambda a: a.reshape(1, -1).astype(F32)

    near_bias = _rel_bias_tiles(rel_bias.astype(F32))
    h = x.reshape(batch * seq, d)
    mem2 = mem.reshape(batch * mem_len, d)
    for l in range(depth):
        lambda_init = LAMBDA_INIT_BASE - LAMBDA_INIT_AMP * math.exp(-LAMBDA_INIT_RATE * l)
        q, k, vt, gaya, gb = _mixer_front(
            h, row(norm_mix_g[l]), w_in[l], w_gate[l], row(b_gate[l]), w_pool_group[l], row(pool_scale[l]), w_a_proj[l],
            seq=seq, q_scale=DIFF_HEAD_DIM ** -0.5 * LOG2E)
        lam_vecs = jnp.stack([lambda_q1[l], lambda_k1[l], lambda_q2[l], lambda_k2[l]]).astype(F32)
        o = _diff_attn(lam_vecs, q, k, vt, near_bias, row(subln_g[l]), batch=batch, seq=seq, lambda_init=lambda_init)
        h = _merge_cross(h, o, gaya, gb, w_b_proj[l], w_out[l], row(norm_cross_g[l]), w_cq[l], mem2,
                         row(norm_mem_g[l]), w_ckv[l], w_co[l], seq=seq, mem_len=mem_len)
        h = _mlp(h, row(norm_mlp_g[l]), w_ff1[l], w_ff2[l], row(final_norm_g), final_norm=(l == depth - 1))
    return h.reshape(batch, seq, d)
```

```python
import functools
import math

import jax
import jax.numpy as jnp
import numpy as np
from jax import lax
from jax.experimental import pallas as pl
from jax.experimental.pallas import tpu as pltpu

CHUNK = 64
POOL_WINDOWS = (2, 4, 8, 16)
DIFF_HEADS = 8
DIFF_HEAD_DIM = 64
LAMBDA_INIT_BASE = 0.8
LAMBDA_INIT_AMP = 0.6
LAMBDA_INIT_RATE = 0.3
REL_BUCKETS = 32
REL_MAX_DIST = 128
CROSS_HEADS = 4
EPS = 1e-6
NEG_INF = -1e30
LOG2E = math.log2(math.e)

POOL_HALO = 16
TOKEN_TILE = 512
MLP_TILE = 512
ATTN_TILE = 256
ATTN_HEADS = 2
ONES_ROWS = 16
ATTN_LEAD = 2
FF_CHUNK = 1024
SUBLANES = 8
LANES = 128
VMEM_LIMIT = 56 * 1024 * 1024

BF16 = jnp.bfloat16
F32 = jnp.float32


def _resident(shape):
    n = len(shape)
    return pl.BlockSpec(shape, lambda *_: (0,) * n, pipeline_mode=pl.Buffered(1))


def _rms(x, g):
    return x * lax.rsqrt(jnp.mean(x * x, axis=-1, keepdims=True) + EPS) * g


def _dot(a, b):
    return jnp.dot(a, b, preferred_element_type=F32)


def _dot_nt(a, b):
    return lax.dot_general(a, b, (((1,), (1,)), ((), ())), preferred_element_type=F32)


def _mixer_front_kernel(x_ref, g_ref, win_ref, wg_ref, bg_ref, wp_ref, ps_ref, wa_ref,
                        q_ref, k_ref, vt_ref, gaya_ref, gb_ref, ubuf_ref, ga_ref, *, tiles_per_seq, q_scale):
    tm, d = x_ref.shape
    i = pl.program_id(0)
    tile_in_seq = i % tiles_per_seq

    @pl.when(tile_in_seq == 0)
    def _():
        ubuf_ref[0:POOL_HALO, :] = jnp.zeros((POOL_HALO, d), F32)

    xn = _rms(x_ref[...], g_ref[...]).astype(BF16)

    def w_in(lo, hi):
        return win_ref[:, lo:hi].astype(BF16)

    ubuf_ref[POOL_HALO:POOL_HALO + tm, :] = _dot(xn, w_in(0, d))

    width = q_ref.shape[1]
    q_ref[...] = (_dot(xn, w_in(d, d + width)) * q_scale).astype(BF16)
    k_ref[...] = _dot(xn, w_in(d + width, d + 2 * width)).astype(BF16)
    v = _dot(xn, w_in(d + 2 * width, d + 3 * width))
    hd = width // DIFF_HEADS
    for h in range(DIFF_HEADS):
        r0 = h * (hd + ONES_ROWS)
        vt_ref[r0:r0 + hd, :] = v[:, h * hd:(h + 1) * hd].T.astype(BF16)
        vt_ref[r0 + hd:r0 + hd + ONES_ROWS, :] = jnp.ones((ONES_ROWS, tm), BF16)
    z = _dot(xn, wg_ref[...].astype(BF16)) + bg_ref[...]
    gates = 1.0 / (1.0 + jnp.exp(-z))
    ga_ref[...] = gates[:, :d]
    gb_ref[...] = gates[:, d:].astype(gb_ref.dtype)

    pos = tile_in_seq * tm + lax.broadcasted_iota(jnp.int32, (tm, 1), 0)
    gd = d // len(POOL_WINDOWS)
    mapped = []
    for g, w in enumerate(POOL_WINDOWS):
        cols = slice(g * gd, (g + 1) * gd)
        u_g = ubuf_ref[POOL_HALO:POOL_HALO + tm, cols]
        acc = u_g
        for j in range(1, w):
            acc = acc + ubuf_ref[POOL_HALO - j:POOL_HALO - j + tm, cols]
        inv_count = 1.0 / jnp.minimum(pos + 1, w).astype(F32)
        pooled = acc * inv_count - u_g
        mapped.append(_dot(pooled.astype(BF16), wp_ref[g].astype(BF16)))
    ubuf_ref[0:POOL_HALO, :] = ubuf_ref[tm:tm + POOL_HALO, :]
    y = jnp.concatenate(mapped, axis=-1) * ps_ref[...]
    gaya_ref[...] = (ga_ref[...] * _dot(y.astype(BF16), wa_ref[...].astype(BF16))).astype(gaya_ref.dtype)


def _mixer_front(x2, g, win, wg, bg, wp, ps, wa, *, seq, q_scale):
    n, d = x2.shape
    tm = TOKEN_TILE
    width = (win.shape[1] - d) // 3
    row = lambda c: pl.BlockSpec((tm, c), lambda i: (i, 0))
    tiles_per_seq = seq // tm
    vt_rows = width + DIFF_HEADS * ONES_ROWS
    kern = functools.partial(_mixer_front_kernel, tiles_per_seq=tiles_per_seq, q_scale=q_scale)
    return pl.pallas_call(
        kern,
        grid=(n // tm,),
        in_specs=[row(d), _resident(g.shape), _resident(win.shape), _resident(wg.shape), _resident(bg.shape),
                  _resident(wp.shape), _resident(ps.shape), _resident(wa.shape)],
        out_specs=[row(width), row(width),
                   pl.BlockSpec((None, vt_rows, tm), lambda i: (i // tiles_per_seq, 0, i % tiles_per_seq)),
                   row(d), row(d)],
        out_shape=[jax.ShapeDtypeStruct((n, width), BF16)] * 2
        + [jax.ShapeDtypeStruct((n // seq, vt_rows, seq), BF16)] + [jax.ShapeDtypeStruct((n, d), BF16)] * 2,
        scratch_shapes=[pltpu.VMEM((POOL_HALO + tm, d), F32), pltpu.VMEM((tm, d), F32)],
        compiler_params=pltpu.CompilerParams(dimension_semantics=("arbitrary",), vmem_limit_bytes=VMEM_LIMIT),
        name="mixer_front",
    )(x2, g, win, wg, bg, wp, ps, wa)


def _rel_bucket(rel):
    nb = REL_BUCKETS // 2
    max_exact = nb // 2
    n = np.abs(rel)
    scaled = np.log(np.maximum(n, 1) / max_exact) / math.log(REL_MAX_DIST / max_exact) * (nb - max_exact)
    nearest = np.rint(scaled)
    scaled = np.where(np.abs(scaled - nearest) < 1e-9, nearest, scaled)
    large = np.minimum(max_exact + np.floor(scaled).astype(np.int64), nb - 1)
    return ((rel > 0) * nb + np.where(n < max_exact, n, large)).astype(np.int32)


def _rel_bias_kernel(table_ref, bucket_ref, out_ref, *, far_bucket, tile_buckets):
    h = pl.program_id(0)
    far = table_ref[far_bucket, h]
    t = bucket_ref.shape[-1]
    half = t // 2
    for tile, buckets in enumerate(tile_buckets):
        bucket = bucket_ref[tile]
        acc = jnp.zeros(bucket.shape, F32)
        for b in buckets:
            acc = jnp.where(bucket == b, table_ref[b, h] - far, acc)
        bias = acc * LOG2E
        for part in range(4):
            src = (part // 2) * half
            out_ref[tile, :, part * half:(part + 1) * half] = bias[:, src:src + half]


def _prev_bias_region():
    t = ATTN_TILE
    key = np.arange(t, dtype=np.int64)[:, None]
    query = np.arange(t, dtype=np.int64)[None, :]
    near = _rel_bucket(key - t - query) != REL_BUCKETS // 2 - 1
    first_row = int(np.nonzero(near.any(axis=1))[0].min()) // SUBLANES * SUBLANES
    last_query = int(np.nonzero(near.any(axis=0))[0].max())
    n_cols = t if last_query < t // 2 else 2 * t
    return first_row, n_cols


def _rel_bias_tiles(rel_bias):
    t = ATTN_TILE
    assert t > REL_MAX_DIST
    heads = rel_bias.shape[1]
    key = np.arange(t, dtype=np.int64)[:, None]
    query = np.arange(t, dtype=np.int64)[None, :]
    bucket = np.stack([_rel_bucket(key - t - query), _rel_bucket(key - query)])
    far_bucket = REL_BUCKETS // 2 - 1
    tile_buckets = tuple(tuple(int(b) for b in np.unique(tile) if b != far_bucket) for tile in bucket)
    bucket = jnp.asarray(bucket)
    kern = functools.partial(_rel_bias_kernel, far_bucket=far_bucket, tile_buckets=tile_buckets)
    return pl.pallas_call(
        kern,
        grid=(heads,),
        in_specs=[pl.BlockSpec(memory_space=pltpu.SMEM), pl.BlockSpec((2, t, t), lambda h: (0, 0, 0))],
        out_specs=pl.BlockSpec((None, 2, t, 2 * t), lambda h: (h, 0, 0, 0)),
        out_shape=jax.ShapeDtypeStruct((heads, 2, t, 2 * t), F32),
        name="rel_bias",
    )(rel_bias, bucket)


def _attn_schedule(nq, heads):
    tiles = [(h, i, j) for h in range(heads) for i in reversed(range(nq)) for j in range(i + 1)]
    events, slot_of, free, n_slots = [], {}, [], 0
    pv_pos = 0
    for pos in range(2 * len(tiles)):
        if pos < len(tiles):
            h, i, j = tiles[pos]
            if (h, i) not in slot_of:
                if not free:
                    free.append(n_slots)
                    n_slots += 1
                slot_of[h, i] = free.pop(0)
            events.append(("score", h, i, j, slot_of[h, i]))
        if pv_pos < len(tiles):
            h, i, j = tiles[pv_pos]
            scored = min(pos + 1, len(tiles))
            if scored >= min(tiles.index((h, i, i)) + 1 + ATTN_LEAD, len(tiles)):
                events.append(("pv", h, i, j, slot_of[h, i]))
                pv_pos += 1
                if j == i:
                    free.append(slot_of[h, i])
    assert pv_pos == len(tiles)
    return events, n_slots


def _diff_attn_kernel(base_ref, lam_ref, q_ref, k_ref, vt_ref, nb_ref, sg_ref, o_ref, s_ref, *, lambda_init):
    seq = q_ref.shape[0]
    hd = q_ref.shape[1] // ATTN_HEADS
    t = ATTN_TILE
    nq = seq // t
    events, n_slots = _attn_schedule(nq, ATTN_HEADS)
    assert s_ref.shape[0] == n_slots
    base = base_ref[0]
    half = t // 2
    lane = lax.broadcasted_iota(jnp.int32, (half, hd), 1)
    assert half == 2 * CHUNK
    later_chunk = lax.broadcasted_iota(jnp.int32, (CHUNK, t), 1) % half >= CHUNK

    def mask_first_query_chunk(block):
        return jnp.where(later_chunk, block, NEG_INF)

    lam = (jnp.exp(jnp.sum(lam_ref[0:1, :] * lam_ref[1:2, :], axis=-1, keepdims=True))
           - jnp.exp(jnp.sum(lam_ref[2:3, :] * lam_ref[3:4, :], axis=-1, keepdims=True)) + lambda_init)

    def stacked_q(h, i):
        parts = []
        for part in range(2):
            q = q_ref[i * t + part * half:i * t + (part + 1) * half, h * hd:(h + 1) * hd]
            zero = jnp.zeros_like(q)
            parts += [jnp.where(lane < hd // 2, q, zero), jnp.where(lane >= hd // 2, q, zero)]
        return jnp.concatenate(parts, axis=0)

    def sublane_max(s):
        return jnp.max(s.reshape(s.shape[0] // SUBLANES, SUBLANES, s.shape[1]), axis=0)

    def score_tile(h, i, j, slot, qq):
        heads = slice(h * hd, (h + 1) * hd)
        if j < i:
            s = _dot_nt(k_ref[j * t:(j + 1) * t, heads], qq)
            if j == i - 1:
                r0, nc = _prev_bias_region()
                corner = s[r0:, :nc] + nb_ref[h, 0, r0:t, 0:nc]
                s = jnp.concatenate([s[:r0], jnp.concatenate([corner, s[r0:, nc:]], axis=1)], axis=0)
            s_ref[base + slot, j, :, 0:2 * t] = s
            return sublane_max(s)
        s_top = _dot_nt(k_ref[j * t:j * t + half, heads], qq)
        s_top = s_top + nb_ref[h, 1, 0:half, :]
        s_top = jnp.concatenate(
            [s_top[:CHUNK],
             jnp.concatenate([mask_first_query_chunk(s_top[CHUNK:, :t]), s_top[CHUNK:, t:]], axis=1)], axis=0)
        s_bot = _dot_nt(k_ref[j * t + half:(j + 1) * t, heads], qq[t:])
        s_bot = s_bot + nb_ref[h, 1, half:t, t:2 * t]
        s_bot = jnp.concatenate([s_bot[:CHUNK], mask_first_query_chunk(s_bot[CHUNK:])], axis=0)
        s_ref[base + slot, j, 0:half, 0:2 * t] = s_top
        s_ref[base + slot, j, half:t, t:2 * t] = s_bot
        m_top = sublane_max(s_top)
        m_bot = sublane_max(s_bot)
        return jnp.concatenate([m_top[:, :t], jnp.maximum(m_top[:, t:], m_bot)], axis=1)

    def prob_tile(h, i, j, slot):
        if j < i:
            return jnp.exp2(s_ref[base + slot, j, :, 0:2 * t] - mx[h, i]).astype(BF16)
        p_top = jnp.exp2(s_ref[base + slot, j, 0:half, 0:2 * t] - mx[h, i]).astype(BF16)
        p_bot = jnp.exp2(s_ref[base + slot, j, half:t, t:2 * t] - mx[h, i][:, t:]).astype(BF16)
        p_bot = jnp.concatenate([jnp.zeros((half, t), BF16), p_bot], axis=1)
        return jnp.concatenate([p_top, p_bot], axis=0)

    def finish(h, i, acc):
        o_maps = acc[:hd] / acc[hd:hd + 1]
        o = jnp.concatenate([o_maps[:, part * t:part * t + half] - lam * o_maps[:, part * t + half:(part + 1) * t]
                             for part in range(2)], axis=1).T
        o = _rms(o, sg_ref[...]) * (1.0 - lambda_init)
        o_ref[i * t:(i + 1) * t, h * hd:(h + 1) * hd] = o.astype(o_ref.dtype)

    qq, mx, acc = {}, {}, {}
    for kind, h, i, j, slot in events:
        if kind == "score":
            if j == 0:
                qq[h, i] = stacked_q(h, i)
            tile_max = score_tile(h, i, j, slot, qq[h, i])
            mx[h, i] = tile_max if j == 0 else jnp.maximum(mx[h, i], tile_max)
        else:
            if j == 0:
                mx[h, i] = jnp.max(mx[h, i], axis=0, keepdims=True)
            vt = vt_ref[h * (hd + ONES_ROWS):(h + 1) * (hd + ONES_ROWS), j * t:(j + 1) * t]
            pv = _dot(vt, prob_tile(h, i, j, slot))
            acc[h, i] = pv if j == 0 else acc[h, i] + pv
            if j == i:
                finish(h, i, acc.pop((h, i)))


def _diff_attn(lam_vecs, q, k, vt, near_bias, subln_g, *, batch, seq, lambda_init):
    n, width = q.shape
    t = ATTN_TILE
    hd = width // DIFF_HEADS
    nq = seq // t
    q3, k3 = (a.reshape(batch, seq, width) for a in (q, k))
    head_block = pl.BlockSpec((None, seq, ATTN_HEADS * hd), lambda g, b: (b, 0, g))
    vt_block = pl.BlockSpec((None, ATTN_HEADS * (hd + ONES_ROWS), seq), lambda g, b: (b, g, 0))
    n_slots = _attn_schedule(nq, ATTN_HEADS)[1]
    kern = functools.partial(_diff_attn_kernel, lambda_init=lambda_init)
    out = pl.pallas_call(
        kern,
        grid=(DIFF_HEADS // ATTN_HEADS, batch),
        in_specs=[
            pl.BlockSpec(memory_space=pltpu.SMEM),
            pl.BlockSpec(lam_vecs.shape, lambda g, b: (0, 0)),
            head_block, head_block, vt_block,
            pl.BlockSpec((ATTN_HEADS, 2, t, 2 * t), lambda g, b: (g, 0, 0, 0)),
            pl.BlockSpec(subln_g.shape, lambda g, b: (0, 0)),
        ],
        out_specs=head_block,
        out_shape=jax.ShapeDtypeStruct((batch, seq, width), BF16),
        scratch_shapes=[pltpu.VMEM((n_slots, nq, t, 2 * t + LANES), F32)],
        compiler_params=pltpu.CompilerParams(dimension_semantics=("arbitrary",) * 2, vmem_limit_bytes=VMEM_LIMIT),
        name="diff_attn",
    )(jnp.zeros((1,), jnp.int32), lam_vecs, q3, k3, vt, near_bias, subln_g)
    return out.reshape(n, width)


def _merge_cross_kernel(x_ref, o_ref, gaya_ref, gb_ref, wb_ref, wmo_ref, g_ref, wq_ref, mem_ref, gm_ref, wkv_ref,
                        wo_ref, out_ref, kv_ref, *, q_scale, tiles_per_seq):
    tm, d = x_ref.shape

    @pl.when(pl.program_id(0) % tiles_per_seq == 0)
    def _():
        mn = _rms(mem_ref[...], gm_ref[...]).astype(BF16)
        kv_ref[...] = _dot(mn, wkv_ref[...].astype(BF16)).astype(BF16)

    y_b = _dot(o_ref[...], wb_ref[...].astype(BF16))
    merged = gaya_ref[...].astype(F32) + gb_ref[...].astype(F32) * y_b
    h = x_ref[...] + _dot(merged.astype(BF16), wmo_ref[...].astype(BF16))

    hn = _rms(h, g_ref[...]).astype(BF16)
    cq = (_dot(hn, wq_ref[...].astype(BF16)) * q_scale).astype(BF16)
    hd = d // CROSS_HEADS
    cols = [slice(a * hd, (a + 1) * hd) for a in range(CROSS_HEADS)]
    scores = [_dot_nt(cq[:, c], kv_ref[:, c]) for c in cols]
    probs = [jnp.exp2(s - jnp.max(s, axis=-1, keepdims=True)) for s in scores]
    heads = [_dot(p.astype(BF16), kv_ref[:, d + c.start:d + c.stop]) / jnp.sum(p, axis=-1, keepdims=True)
             for p, c in zip(probs, cols)]
    o = jnp.concatenate(heads, axis=-1).astype(BF16)
    out_ref[...] = h + _dot(o, wo_ref[...].astype(BF16))


def _merge_cross(x2, o, gaya, gb, wb, wmo, g, wq, mem2, gm, wkv, wo, *, seq, mem_len):
    n, d = x2.shape
    tm = TOKEN_TILE
    tiles_per_seq = seq // tm
    row = lambda c: pl.BlockSpec((tm, c), lambda i: (i, 0))
    kern = functools.partial(_merge_cross_kernel, q_scale=(d // CROSS_HEADS) ** -0.5 * LOG2E,
                             tiles_per_seq=tiles_per_seq)
    return pl.pallas_call(
        kern,
        grid=(n // tm,),
        in_specs=[row(d), row(o.shape[1]), row(d), row(d), _resident(wb.shape), _resident(wmo.shape),
                  _resident(g.shape), _resident(wq.shape),
                  pl.BlockSpec((mem_len, d), lambda i: (i // tiles_per_seq, 0)),
                  _resident(gm.shape), _resident(wkv.shape), _resident(wo.shape)],
        out_specs=row(d),
        out_shape=jax.ShapeDtypeStruct((n, d), F32),
        scratch_shapes=[pltpu.VMEM((mem_len, wkv.shape[1]), BF16)],
        compiler_params=pltpu.CompilerParams(dimension_semantics=("arbitrary",), vmem_limit_bytes=VMEM_LIMIT),
        name="merge_cross",
    )(x2, o, gaya, gb, wb, wmo, g, wq, mem2, gm, wkv, wo)


def _mlp_kernel(h_ref, g_ref, w1_ref, w2_ref, gf_ref, out_ref, *, final_norm):
    h = h_ref[...]
    hn = _rms(h, g_ref[...]).astype(BF16)
    acc = h
    for c in range(w1_ref.shape[1] // FF_CHUNK):
        cols = slice(c * FF_CHUNK, (c + 1) * FF_CHUNK)
        a = jnp.maximum(_dot(hn, w1_ref[:, cols].astype(BF16)), 0.0)
        acc = acc + _dot((a * a).astype(BF16), w2_ref[cols, :].astype(BF16))
    out_ref[...] = _rms(acc, gf_ref[...]) if final_norm else acc


def _mlp(h, g, w1, w2, gf, *, final_norm):
    n, d = h.shape
    tm = MLP_TILE
    row = pl.BlockSpec((tm, d), lambda i: (i, 0))
    kern = functools.partial(_mlp_kernel, final_norm=final_norm)
    return pl.pallas_call(
        kern,
        grid=(n // tm,),
        in_specs=[row, _resident(g.shape), _resident(w1.shape), _resident(w2.shape), _resident(gf.shape)],
        out_specs=row,
        out_shape=jax.ShapeDtypeStruct((n, d), F32),
        compiler_params=pltpu.CompilerParams(dimension_semantics=("arbitrary",), vmem_limit_bytes=VMEM_LIMIT),
        name="mlp",
    )(h, g, w1, w2, gf)


def kernel(x, mem, norm_mix_g, w_in, w_pool_group, pool_scale, w_a_proj, lambda_q1, lambda_k1, lambda_q2, lambda_k2, subln_g, rel_bias, w_b_proj, w_gate, b_gate, w_out, norm_cross_g, norm_mem_g, w_cq, w_ckv, w_co, norm_mlp_g, w_ff1, w_ff2, final_norm_g):
    batch, seq, d = x.shape
    mem_len = mem.shape[1]
    depth = w_in.shape[0]
    pool_width = w_a_proj.shape[1]
    assert pool_width == d and seq % TOKEN_TILE == 0 and seq % MLP_TILE == 0 and seq % ATTN_TILE == 0
    row = lambda a: a.reshape(1, -1).astype(F32)

    near_bias = _rel_bias_tiles(rel_bias.astype(F32))
    h = x.reshape(batch * seq, d)
    mem2 = mem.reshape(batch * mem_len, d)
    for l in range(depth):
        lambda_init = LAMBDA_INIT_BASE - LAMBDA_INIT_AMP * math.exp(-LAMBDA_INIT_RATE * l)
        q, k, vt, gaya, gb = _mixer_front(
            h, row(norm_mix_g[l]), w_in[l], w_gate[l], row(b_gate[l]), w_pool_group[l], row(pool_scale[l]), w_a_proj[l],
            seq=seq, q_scale=DIFF_HEAD_DIM ** -0.5 * LOG2E)
        lam_vecs = jnp.stack([lambda_q1[l], lambda_k1[l], lambda_q2[l], lambda_k2[l]]).astype(F32)
        o = _diff_attn(lam_vecs, q, k, vt, near_bias, row(subln_g[l]), batch=batch, seq=seq, lambda_init=lambda_init)
        h = _merge_cross(h, o, gaya, gb, w_b_proj[l], w_out[l], row(norm_cross_g[l]), w_cq[l], mem2,
                         row(norm_mem_g[l]), w_ckv[l], w_co[l], seq=seq, mem_len=mem_len)
        h = _mlp(h, row(norm_mlp_g[l]), w_ff1[l], w_ff2[l], row(final_norm_g), final_norm=(l == depth - 1))
    return h.reshape(batch, seq, d)
```

```python
import functools
import math

import jax
import jax.numpy as jnp
import numpy as np
from jax import lax
from jax.experimental import pallas as pl
from jax.experimental.pallas import tpu as pltpu

CHUNK = 64
POOL_WINDOWS = (2, 4, 8, 16)
DIFF_HEADS = 8
DIFF_HEAD_DIM = 64
LAMBDA_INIT_BASE = 0.8
LAMBDA_INIT_AMP = 0.6
LAMBDA_INIT_RATE = 0.3
REL_BUCKETS = 32
REL_MAX_DIST = 128
CROSS_HEADS = 4
EPS = 1e-6
NEG_INF = -1e30
LOG2E = math.log2(math.e)

POOL_HALO = 16
TOKEN_TILE = 512
MLP_TILE = 512
ATTN_TILE = 256
ATTN_HEADS = 2
ONES_ROWS = 16
ATTN_LEAD = 2
FF_CHUNK = 1024
SUBLANES = 8
LANES = 128
VMEM_LIMIT = 56 * 1024 * 1024

BF16 = jnp.bfloat16
F32 = jnp.float32


def _resident(shape):
    n = len(shape)
    return pl.BlockSpec(shape, lambda *_: (0,) * n, pipeline_mode=pl.Buffered(1))


def _rms(x, g):
    return x * lax.rsqrt(jnp.mean(x * x, axis=-1, keepdims=True) + EPS) * g


def _dot(a, b):
    return jnp.dot(a, b, preferred_element_type=F32)


def _dot_nt(a, b):
    return lax.dot_general(a, b, (((1,), (1,)), ((), ())), preferred_element_type=F32)


def _mixer_front_kernel(x_ref, g_ref, win_ref, wg_ref, bg_ref, wp_ref, ps_ref, wa_ref,
                        q_ref, k_ref, vt_ref, gaya_ref, gb_ref, ubuf_ref, ga_ref, *, tiles_per_seq, q_scale):
    tm, d = x_ref.shape
    i = pl.program_id(0)
    tile_in_seq = i % tiles_per_seq

    @pl.when(tile_in_seq == 0)
    def _():
        ubuf_ref[0:POOL_HALO, :] = jnp.zeros((POOL_HALO, d), F32)

    xn = _rms(x_ref[...], g_ref[...]).astype(BF16)

    def w_in(lo, hi):
        return win_ref[:, lo:hi].astype(BF16)

    ubuf_ref[POOL_HALO:POOL_HALO + tm, :] = _dot(xn, w_in(0, d))

    width = q_ref.shape[1]
    q_ref[...] = (_dot(xn, w_in(d, d + width)) * q_scale).astype(BF16)
    k_ref[...] = _dot(xn, w_in(d + width, d + 2 * width)).astype(BF16)
    v = _dot(xn, w_in(d + 2 * width, d + 3 * width))
    hd = width // DIFF_HEADS
    for h in range(DIFF_HEADS):
        r0 = h * (hd + ONES_ROWS)
        vt_ref[r0:r0 + hd, :] = v[:, h * hd:(h + 1) * hd].T.astype(BF16)
        vt_ref[r0 + hd:r0 + hd + ONES_ROWS, :] = jnp.ones((ONES_ROWS, tm), BF16)

    pos = tile_in_seq * tm + lax.broadcasted_iota(jnp.int32, (tm, 1), 0)
    gd = d // len(POOL_WINDOWS)
    gate_cols = 2 * d // len(POOL_WINDOWS)
    assert d % gate_cols == 0
    mapped = []
    for g, w in enumerate(POOL_WINDOWS):
        lo = g * gate_cols
        z = _dot(xn, wg_ref[:, lo:lo + gate_cols].astype(BF16)) + bg_ref[:, lo:lo + gate_cols]
        gate = 1.0 / (1.0 + jnp.exp(-z))
        if lo < d:
            ga_ref[:, lo:lo + gate_cols] = gate
        else:
            gb_ref[:, lo - d:lo - d + gate_cols] = gate.astype(gb_ref.dtype)
        cols = slice(g * gd, (g + 1) * gd)
        u_g = ubuf_ref[POOL_HALO:POOL_HALO + tm, cols]
        acc = u_g
        for j in range(1, w):
            acc = acc + ubuf_ref[POOL_HALO - j:POOL_HALO - j + tm, cols]
        inv_count = 1.0 / jnp.minimum(pos + 1, w).astype(F32)
        pooled = acc * inv_count - u_g
        mapped.append(_dot(pooled.astype(BF16), wp_ref[g].astype(BF16)))
    ubuf_ref[0:POOL_HALO, :] = ubuf_ref[tm:tm + POOL_HALO, :]
    y = jnp.concatenate(mapped, axis=-1) * ps_ref[...]
    gaya_ref[...] = (ga_ref[...] * _dot(y.astype(BF16), wa_ref[...].astype(BF16))).astype(gaya_ref.dtype)


def _mixer_front(x2, g, win, wg, bg, wp, ps, wa, *, seq, q_scale):
    n, d = x2.shape
    tm = TOKEN_TILE
    width = (win.shape[1] - d) // 3
    row = lambda c: pl.BlockSpec((tm, c), lambda i: (i, 0))
    tiles_per_seq = seq // tm
    vt_rows = width + DIFF_HEADS * ONES_ROWS
    kern = functools.partial(_mixer_front_kernel, tiles_per_seq=tiles_per_seq, q_scale=q_scale)
    return pl.pallas_call(
        kern,
        grid=(n // tm,),
        in_specs=[row(d), _resident(g.shape), _resident(win.shape), _resident(wg.shape), _resident(bg.shape),
                  _resident(wp.shape), _resident(ps.shape), _resident(wa.shape)],
        out_specs=[row(width), row(width),
                   pl.BlockSpec((None, vt_rows, tm), lambda i: (i // tiles_per_seq, 0, i % tiles_per_seq)),
                   row(d), row(d)],
        out_shape=[jax.ShapeDtypeStruct((n, width), BF16)] * 2
        + [jax.ShapeDtypeStruct((n // seq, vt_rows, seq), BF16)] + [jax.ShapeDtypeStruct((n, d), BF16)] * 2,
        scratch_shapes=[pltpu.VMEM((POOL_HALO + tm, d), F32), pltpu.VMEM((tm, d), F32)],
        compiler_params=pltpu.CompilerParams(dimension_semantics=("arbitrary",), vmem_limit_bytes=VMEM_LIMIT),
        name="mixer_front",
    )(x2, g, win, wg, bg, wp, ps, wa)


def _rel_bucket(rel):
    nb = REL_BUCKETS // 2
    max_exact = nb // 2
    n = np.abs(rel)
    scaled = np.log(np.maximum(n, 1) / max_exact) / math.log(REL_MAX_DIST / max_exact) * (nb - max_exact)
    nearest = np.rint(scaled)
    scaled = np.where(np.abs(scaled - nearest) < 1e-9, nearest, scaled)
    large = np.minimum(max_exact + np.floor(scaled).astype(np.int64), nb - 1)
    return ((rel > 0) * nb + np.where(n < max_exact, n, large)).astype(np.int32)


def _rel_bias_kernel(table_ref, bucket_ref, out_ref, *, far_bucket, tile_buckets):
    h = pl.program_id(0)
    far = table_ref[far_bucket, h]
    t = bucket_ref.shape[-1]
    half = t // 2
    for tile, buckets in enumerate(tile_buckets):
        bucket = bucket_ref[tile]
        acc = jnp.zeros(bucket.shape, F32)
        for b in buckets:
            acc = jnp.where(bucket == b, table_ref[b, h] - far, acc)
        bias = acc * LOG2E
        for part in range(4):
            src = (part // 2) * half
            out_ref[tile, :, part * half:(part + 1) * half] = bias[:, src:src + half]


def _prev_bias_region():
    t = ATTN_TILE
    key = np.arange(t, dtype=np.int64)[:, None]
    query = np.arange(t, dtype=np.int64)[None, :]
    near = _rel_bucket(key - t - query) != REL_BUCKETS // 2 - 1
    first_row = int(np.nonzero(near.any(axis=1))[0].min()) // SUBLANES * SUBLANES
    last_query = int(np.nonzero(near.any(axis=0))[0].max())
    n_cols = t if last_query < t // 2 else 2 * t
    return first_row, n_cols


def _rel_bias_tiles(rel_bias):
    t = ATTN_TILE
    assert t > REL_MAX_DIST
    heads = rel_bias.shape[1]
    key = np.arange(t, dtype=np.int64)[:, None]
    query = np.arange(t, dtype=np.int64)[None, :]
    bucket = np.stack([_rel_bucket(key - t - query), _rel_bucket(key - query)])
    far_bucket = REL_BUCKETS // 2 - 1
    tile_buckets = tuple(tuple(int(b) for b in np.unique(tile) if b != far_bucket) for tile in bucket)
    bucket = jnp.asarray(bucket)
    kern = functools.partial(_rel_bias_kernel, far_bucket=far_bucket, tile_buckets=tile_buckets)
    return pl.pallas_call(
        kern,
        grid=(heads,),
        in_specs=[pl.BlockSpec(memory_space=pltpu.SMEM), pl.BlockSpec((2, t, t), lambda h: (0, 0, 0))],
        out_specs=pl.BlockSpec((None, 2, t, 2 * t), lambda h: (h, 0, 0, 0)),
        out_shape=jax.ShapeDtypeStruct((heads, 2, t, 2 * t), F32),
        name="rel_bias",
    )(rel_bias, bucket)


def _attn_schedule(nq, heads):
    tiles = [(h, i, j) for h in range(heads) for i in reversed(range(nq)) for j in range(i + 1)]
    events, slot_of, free, n_slots = [], {}, [], 0
    pv_pos = 0
    for pos in range(2 * len(tiles)):
        if pos < len(tiles):
            h, i, j = tiles[pos]
            if (h, i) not in slot_of:
                if not free:
                    free.append(n_slots)
                    n_slots += 1
                slot_of[h, i] = free.pop(0)
            events.append(("score", h, i, j, slot_of[h, i]))
        if pv_pos < len(tiles):
            h, i, j = tiles[pv_pos]
            scored = min(pos + 1, len(tiles))
            if scored >= min(tiles.index((h, i, i)) + 1 + ATTN_LEAD, len(tiles)):
                events.append(("pv", h, i, j, slot_of[h, i]))
                pv_pos += 1
                if j == i:
                    free.append(slot_of[h, i])
    assert pv_pos == len(tiles)
    return events, n_slots


def _diff_attn_kernel(base_ref, lam_ref, q_ref, k_ref, vt_ref, nb_ref, sg_ref, o_ref, s_ref, *, lambda_init):
    seq = q_ref.shape[0]
    hd = q_ref.shape[1] // ATTN_HEADS
    t = ATTN_TILE
    nq = seq // t
    events, n_slots = _attn_schedule(nq, ATTN_HEADS)
    assert s_ref.shape[0] == n_slots
    base = base_ref[0]
    half = t // 2
    lane = lax.broadcasted_iota(jnp.int32, (half, hd), 1)
    assert half == 2 * CHUNK
    later_chunk = lax.broadcasted_iota(jnp.int32, (CHUNK, t), 1) % half >= CHUNK

    def mask_first_query_chunk(block):
        return jnp.where(later_chunk, block, NEG_INF)

    lam = (jnp.exp(jnp.sum(lam_ref[0:1, :] * lam_ref[1:2, :], axis=-1, keepdims=True))
           - jnp.exp(jnp.sum(lam_ref[2:3, :] * lam_ref[3:4, :], axis=-1, keepdims=True)) + lambda_init)

    def stacked_q(h, i):
        parts = []
        for part in range(2):
            q = q_ref[i * t + part * half:i * t + (part + 1) * half, h * hd:(h + 1) * hd]
            zero = jnp.zeros_like(q)
            parts += [jnp.where(lane < hd // 2, q, zero), jnp.where(lane >= hd // 2, q, zero)]
        return jnp.concatenate(parts, axis=0)

    def sublane_max(s):
        return jnp.max(s.reshape(s.shape[0] // SUBLANES, SUBLANES, s.shape[1]), axis=0)

    def score_tile(h, i, j, slot, qq):
        heads = slice(h * hd, (h + 1) * hd)
        if j < i:
            s = _dot_nt(k_ref[j * t:(j + 1) * t, heads], qq)
            if j == i - 1:
                r0, nc = _prev_bias_region()
                corner = s[r0:, :nc] + nb_ref[h, 0, r0:t, 0:nc]
                s = jnp.concatenate([s[:r0], jnp.concatenate([corner, s[r0:, nc:]], axis=1)], axis=0)
            s_ref[base + slot, j, :, 0:2 * t] = s
            return sublane_max(s)
        s_top = _dot_nt(k_ref[j * t:j * t + half, heads], qq)
        s_top = s_top + nb_ref[h, 1, 0:half, :]
        s_top = jnp.concatenate(
            [s_top[:CHUNK],
             jnp.concatenate([mask_first_query_chunk(s_top[CHUNK:, :t]), s_top[CHUNK:, t:]], axis=1)], axis=0)
        s_bot = _dot_nt(k_ref[j * t + half:(j + 1) * t, heads], qq[t:])
        s_bot = s_bot + nb_ref[h, 1, half:t, t:2 * t]
        s_bot = jnp.concatenate([s_bot[:CHUNK], mask_first_query_chunk(s_bot[CHUNK:])], axis=0)
        s_ref[base + slot, j, 0:half, 0:2 * t] = s_top
        s_ref[base + slot, j, half:t, t:2 * t] = s_bot
        m_top = sublane_max(s_top)
        m_bot = sublane_max(s_bot)
        return jnp.concatenate([m_top[:, :t], jnp.maximum(m_top[:, t:], m_bot)], axis=1)

    def prob_tile(h, i, j, slot):
        if j < i:
            return jnp.exp2(s_ref[base + slot, j, :, 0:2 * t] - mx[h, i]).astype(BF16)
        p_top = jnp.exp2(s_ref[base + slot, j, 0:half, 0:2 * t] - mx[h, i]).astype(BF16)
        p_bot = jnp.exp2(s_ref[base + slot, j, half:t, t:2 * t] - mx[h, i][:, t:]).astype(BF16)
        p_bot = jnp.concatenate([jnp.zeros((half, t), BF16), p_bot], axis=1)
        return jnp.concatenate([p_top, p_bot], axis=0)

    def finish(h, i, acc):
        o_maps = acc[:hd] / acc[hd:hd + 1]
        o = jnp.concatenate([o_maps[:, part * t:part * t + half] - lam * o_maps[:, part * t + half:(part + 1) * t]
                             for part in range(2)], axis=1).T
        o = _rms(o, sg_ref[...]) * (1.0 - lambda_init)
        o_ref[i * t:(i + 1) * t, h * hd:(h + 1) * hd] = o.astype(o_ref.dtype)

    qq, mx, acc = {}, {}, {}
    for kind, h, i, j, slot in events:
        if kind == "score":
            if j == 0:
                qq[h, i] = stacked_q(h, i)
            tile_max = score_tile(h, i, j, slot, qq[h, i])
            mx[h, i] = tile_max if j == 0 else jnp.maximum(mx[h, i], tile_max)
        else:
            if j == 0:
                mx[h, i] = jnp.max(mx[h, i], axis=0, keepdims=True)
            vt = vt_ref[h * (hd + ONES_ROWS):(h + 1) * (hd + ONES_ROWS), j * t:(j + 1) * t]
            pv = _dot(vt, prob_tile(h, i, j, slot))
            acc[h, i] = pv if j == 0 else acc[h, i] + pv
            if j == i:
                finish(h, i, acc.pop((h, i)))


def _diff_attn(lam_vecs, q, k, vt, near_bias, subln_g, *, batch, seq, lambda_init):
    n, width = q.shape
    t = ATTN_TILE
    hd = width // DIFF_HEADS
    nq = seq // t
    q3, k3 = (a.reshape(batch, seq, width) for a in (q, k))
    head_block = pl.BlockSpec((None, seq, ATTN_HEADS * hd), lambda g, b: (b, 0, g))
    vt_block = pl.BlockSpec((None, ATTN_HEADS * (hd + ONES_ROWS), seq), lambda g, b: (b, g, 0))
    n_slots = _attn_schedule(nq, ATTN_HEADS)[1]
    kern = functools.partial(_diff_attn_kernel, lambda_init=lambda_init)
    out = pl.pallas_call(
        kern,
        grid=(DIFF_HEADS // ATTN_HEADS, batch),
        in_specs=[
            pl.BlockSpec(memory_space=pltpu.SMEM),
            pl.BlockSpec(lam_vecs.shape, lambda g, b: (0, 0)),
            head_block, head_block, vt_block,
            pl.BlockSpec((ATTN_HEADS, 2, t, 2 * t), lambda g, b: (g, 0, 0, 0)),
            pl.BlockSpec(subln_g.shape, lambda g, b: (0, 0)),
        ],
        out_specs=head_block,
        out_shape=jax.ShapeDtypeStruct((batch, seq, width), BF16),
        scratch_shapes=[pltpu.VMEM((n_slots, nq, t, 2 * t + LANES), F32)],
        compiler_params=pltpu.CompilerParams(dimension_semantics=("arbitrary",) * 2, vmem_limit_bytes=VMEM_LIMIT),
        name="diff_attn",
    )(jnp.zeros((1,), jnp.int32), lam_vecs, q3, k3, vt, near_bias, subln_g)
    return out.reshape(n, width)


def _merge_cross_kernel(x_ref, o_ref, gaya_ref, gb_ref, wb_ref, wmo_ref, g_ref, wq_ref, mem_ref, gm_ref, wkv_ref,
                        wo_ref, out_ref, kv_ref, *, q_scale, tiles_per_seq):
    tm, d = x_ref.shape

    @pl.when(pl.program_id(0) % tiles_per_seq == 0)
    def _():
        mn = _rms(mem_ref[...], gm_ref[...]).astype(BF16)
        kv_ref[...] = _dot(mn, wkv_ref[...].astype(BF16)).astype(BF16)

    y_b = _dot(o_ref[...], wb_ref[...].astype(BF16))
    merged = gaya_ref[...].astype(F32) + gb_ref[...].astype(F32) * y_b
    h = x_ref[...] + _dot(merged.astype(BF16), wmo_ref[...].astype(BF16))

    hn = _rms(h, g_ref[...]).astype(BF16)
    cq = (_dot(hn, wq_ref[...].astype(BF16)) * q_scale).astype(BF16)
    hd = d // CROSS_HEADS
    cols = [slice(a * hd, (a + 1) * hd) for a in range(CROSS_HEADS)]
    scores = [_dot_nt(cq[:, c], kv_ref[:, c]) for c in cols]
    probs = [jnp.exp2(s - jnp.max(s, axis=-1, keepdims=True)) for s in scores]
    heads = [_dot(p.astype(BF16), kv_ref[:, d + c.start:d + c.stop]) / jnp.sum(p, axis=-1, keepdims=True)
             for p, c in zip(probs, cols)]
    o = jnp.concatenate(heads, axis=-1).astype(BF16)
    out_ref[...] = h + _dot(o, wo_ref[...].astype(BF16))


def _merge_cross(x2, o, gaya, gb, wb, wmo, g, wq, mem2, gm, wkv, wo, *, seq, mem_len):
    n, d = x2.shape
    tm = TOKEN_TILE
    tiles_per_seq = seq // tm
    row = lambda c: pl.BlockSpec((tm, c), lambda i: (i, 0))
    kern = functools.partial(_merge_cross_kernel, q_scale=(d // CROSS_HEADS) ** -0.5 * LOG2E,
                             tiles_per_seq=tiles_per_seq)
    return pl.pallas_call(
        kern,
        grid=(n // tm,),
        in_specs=[row(d), row(o.shape[1]), row(d), row(d), _resident(wb.shape), _resident(wmo.shape),
                  _resident(g.shape), _resident(wq.shape),
                  pl.BlockSpec((mem_len, d), lambda i: (i // tiles_per_seq, 0)),
                  _resident(gm.shape), _resident(wkv.shape), _resident(wo.shape)],
        out_specs=row(d),
        out_shape=jax.ShapeDtypeStruct((n, d), F32),
        scratch_shapes=[pltpu.VMEM((mem_len, wkv.shape[1]), BF16)],
        compiler_params=pltpu.CompilerParams(dimension_semantics=("arbitrary",), vmem_limit_bytes=VMEM_LIMIT),
        name="merge_cross",
    )(x2, o, gaya, gb, wb, wmo, g, wq, mem2, gm, wkv, wo)


def _mlp_kernel(h_ref, g_ref, w1_ref, w2_ref, gf_ref, out_ref, *, final_norm):
    h = h_ref[...]
    hn = _rms(h, g_ref[...]).astype(BF16)
    acc = h
    for c in range(w1_ref.shape[1] // FF_CHUNK):
        cols = slice(c * FF_CHUNK, (c + 1) * FF_CHUNK)
        a = jnp.maximum(_dot(hn, w1_ref[:, cols].astype(BF16)), 0.0)
        acc = acc + _dot((a * a).astype(BF16), w2_ref[cols, :].astype(BF16))
    out_ref[...] = _rms(acc, gf_ref[...]) if final_norm else acc


def _mlp(h, g, w1, w2, gf, *, final_norm):
    n, d = h.shape
    tm = MLP_TILE
    row = pl.BlockSpec((tm, d), lambda i: (i, 0))
    kern = functools.partial(_mlp_kernel, final_norm=final_norm)
    return pl.pallas_call(
        kern,
        grid=(n // tm,),
        in_specs=[row, _resident(g.shape), _resident(w1.shape), _resident(w2.shape), _resident(gf.shape)],
        out_specs=row,
        out_shape=jax.ShapeDtypeStruct((n, d), F32),
        compiler_params=pltpu.CompilerParams(dimension_semantics=("arbitrary",), vmem_limit_bytes=VMEM_LIMIT),
        name="mlp",
    )(h, g, w1, w2, gf)


def kernel(x, mem, norm_mix_g, w_in, w_pool_group, pool_scale, w_a_proj, lambda_q1, lambda_k1, lambda_q2, lambda_k2, subln_g, rel_bias, w_b_proj, w_gate, b_gate, w_out, norm_cross_g, norm_mem_g, w_cq, w_ckv, w_co, norm_mlp_g, w_ff1, w_ff2, final_norm_g):
    batch, seq, d = x.shape
    mem_len = mem.shape[1]
    depth = w_in.shape[0]
    pool_width = w_a_proj.shape[1]
    assert pool_width == d and seq % TOKEN_TILE == 0 and seq % MLP_TILE == 0 and seq % ATTN_TILE == 0
    row = lambda a: a.reshape(1, -1).astype(F32)

    near_bias = _rel_bias_tiles(rel_bias.astype(F32))
    h = x.reshape(batch * seq, d)
    mem2 = mem.reshape(batch * mem_len, d)
    for l in range(depth):
        lambda_init = LAMBDA_INIT_BASE - LAMBDA_INIT_AMP * math.exp(-LAMBDA_INIT_RATE * l)
        q, k, vt, gaya, gb = _mixer_front(
            h, row(norm_mix_g[l]), w_in[l], w_gate[l], row(b_gate[l]), w_pool_group[l], row(pool_scale[l]), w_a_proj[l],
            seq=seq, q_scale=DIFF_HEAD_DIM ** -0.5 * LOG2E)
        lam_vecs = jnp.stack([lambda_q1[l], lambda_k1[l], lambda_q2[l], lambda_k2[l]]).astype(F32)
        o = _diff_attn(lam_vecs, q, k, vt, near_bias, row(subln_g[l]), batch=batch, seq=seq, lambda_init=lambda_init)
        h = _merge_cross(h, o, gaya, gb, w_b_proj[l], w_out[l], row(norm_cross_g[l]), w_cq[l], mem2,
                         row(norm_mem_g[l]), w_ckv[l], w_co[l], seq=seq, mem_len=mem_len)
        h = _mlp(h, row(norm_mlp_g[l]), w_ff1[l], w_ff2[l], row(final_norm_g), final_norm=(l == depth - 1))
    return h.reshape(batch, seq, d)
```
